```python
import math
import jax, jax.numpy as jnp
from jax import lax
import numpy as np

D_MODEL = 1024
BATCH = 8
SEQ = 2048
DEPTH = 2
DEC_BATCH = 128
DEC_SEQ = 4
PAST_LEN = 16384
PAGE_SIZE = 128

N_META = 16
S5_WIDTH = 768
S5_GROUP = 16
S5_GROUPS = S5_WIDTH // S5_GROUP
S5_STATE = 64
LRU_WIDTH = 768
LRU_BLOCK = 64
LRU_BLOCKS = LRU_WIDTH // LRU_BLOCK
LRU_CONV = 4
LRU_C = 8.0
RET_HEADS = 8
RET_DK = 64
RET_DV = 128
RET_QK = RET_HEADS * RET_DK
RET_V = RET_HEADS * RET_DV
RET_CHUNK = 128
ROPE_BASE = 10000.0
N_BRANCH = 3
D_FF = 2816
FFN_CONV = 3
EPS = 1e-6
IN_SPLITS = (S5_WIDTH, LRU_WIDTH, LRU_WIDTH, RET_QK, RET_QK, RET_V, RET_V, N_BRANCH * D_MODEL)
D_IN = S5_WIDTH + 2 * LRU_WIDTH + 2 * RET_QK + 2 * RET_V + N_BRANCH * D_MODEL

kernel_name = 'hybrid_s5_rglru_retention_step'

F32 = jnp.float32


def rmsnorm(x, g):
    xf = x.astype(F32)
    y = xf * lax.rsqrt(jnp.mean(xf * xf, axis=-1, keepdims=True) + EPS)
    return (y * g.astype(F32)).astype(x.dtype)


def causal_dwconv(x, buf, w, b):
    width = w.shape[0]
    L = x.shape[1]
    xp = jnp.concatenate([buf.astype(x.dtype), x], axis=1)
    y = b + sum(xp[:, j:j + L] * w[j] for j in range(width))
    return y, xp[:, L:]


def _lin_combine(c1, c2):
    a1, b1 = c1
    a2, b2 = c2
    return a2 * a1, a2 * b1 + b2


def _cplx_combine(c1, c2):
    ar1, ai1, br1, bi1 = c1
    ar2, ai2, br2, bi2 = c2
    return (ar2 * ar1 - ai2 * ai1, ar2 * ai1 + ai2 * ar1,
            ar2 * br1 - ai2 * bi1 + br2, ar2 * bi1 + ai2 * br1 + bi2)


def s5_mixer(u, h0_re, h0_im, a_re, a_im, log_step, b_re, b_im, c_re, c_im, d_skip, w_glu, b_glu):
    Bsz, L, _ = u.shape
    uf = u.astype(F32)
    ug = uf.reshape(Bsz, L, S5_GROUPS, S5_GROUP)
    dt = jnp.exp(log_step.astype(F32))[:, None]
    lr = a_re.astype(F32)
    li = a_im.astype(F32)
    mag = jnp.exp(lr * dt)
    abar_re = mag * jnp.cos(li * dt)
    abar_im = mag * jnp.sin(li * dt)
    den = lr * lr + li * li
    nr = abar_re - 1.0
    cr = (nr * lr + abar_im * li) / den
    ci = (abar_im * lr - nr * li) / den
    bre = b_re.astype(F32)
    bim = b_im.astype(F32)
    bbar_re = cr[..., None] * bre - ci[..., None] * bim
    bbar_im = cr[..., None] * bim + ci[..., None] * bre
    bu_re = jnp.einsum('blgi,gpi->blgp', ug, bbar_re)
    bu_im = jnp.einsum('blgi,gpi->blgp', ug, bbar_im)
    h0r = h0_re.astype(F32)
    h0i = h0_im.astype(F32)
    bu_re = bu_re.at[:, 0].add(abar_re * h0r - abar_im * h0i)
    bu_im = bu_im.at[:, 0].add(abar_re * h0i + abar_im * h0r)
    ar = jnp.broadcast_to(abar_re[None, None], (1, L) + abar_re.shape)
    ai = jnp.broadcast_to(abar_im[None, None], (1, L) + abar_im.shape)
    _, _, hr, hi = lax.associative_scan(_cplx_combine, (ar, ai, bu_re, bu_im), axis=1)
    y = (jnp.einsum('blgp,gip->blgi', hr, c_re.astype(F32))
         - jnp.einsum('blgp,gip->blgi', hi, c_im.astype(F32)))
    y = y.reshape(Bsz, L, S5_WIDTH) + d_skip.astype(F32) * uf
    z = jax.nn.gelu(y)
    out = z * jax.nn.sigmoid(z @ w_glu.astype(F32) + b_glu.astype(F32))
    return out.astype(u.dtype), hr[:, -1], hi[:, -1]


def rglru_mixer(xb, gate, h0, conv_buf, conv_w, conv_b, w_a, b_a, w_x, b_x, lam):
    Bsz, L, _ = xb.shape
    xc, new_buf = causal_dwconv(xb, conv_buf, conv_w, conv_b)
    xcf = xc.astype(F32)
    xh = xcf.reshape(Bsz, L, LRU_BLOCKS, LRU_BLOCK)
    r = jax.nn.sigmoid(jnp.einsum('blhi,hij->blhj', xh, w_a.astype(F32)).reshape(Bsz, L, LRU_WIDTH) + b_a.astype(F32))
    i = jax.nn.sigmoid(jnp.einsum('blhi,hij->blhj', xh, w_x.astype(F32)).reshape(Bsz, L, LRU_WIDTH) + b_x.astype(F32))
    log_a = -LRU_C * r * jax.nn.softplus(-lam.astype(F32))
    a = jnp.exp(log_a)
    mult = jnp.sqrt(-jnp.expm1(2.0 * log_a))
    bvals = mult * (i * xcf)
    bvals = bvals.at[:, 0].add(a[:, 0] * h0.astype(F32))
    _, h = lax.associative_scan(_lin_combine, (a, bvals), axis=1)
    y = h * jax.nn.gelu(gate.astype(F32))
    return y.astype(xb.dtype), h[:, -1], new_buf


def rope(x, pos):
    half = x.shape[-1] // 2
    inv = jnp.power(ROPE_BASE, -jnp.arange(half, dtype=F32) / half)
    ang = pos.astype(F32)[:, None] * inv
    cos = jnp.cos(ang)[None, :, None, :]
    sin = jnp.sin(ang)[None, :, None, :]
    x1 = x[..., :half]
    x2 = x[..., half:]
    return jnp.concatenate([x1 * cos - x2 * sin, x2 * cos + x1 * sin], axis=-1)


def retention_chunk(s, qkv, log_gamma):
    q, k, v = qkv
    C = q.shape[2]
    n = jnp.arange(C, dtype=F32)
    diff = n[:, None] - n[None, :]
    decay = jnp.where(diff >= 0, jnp.exp(log_gamma[:, None, None] * jnp.maximum(diff, 0.0)), 0.0)
    scores = jnp.einsum('bhnd,bhmd->bhnm', q, k) * decay
    o = jnp.einsum('bhnm,bhme->bhne', scores, v)
    o = o + jnp.einsum('bhnd,bhde->bhne', q, s) * jnp.exp(log_gamma[:, None] * (n + 1.0))[..., None]
    k_dec = k * jnp.exp(log_gamma[:, None] * (C - 1.0 - n))[..., None]
    s_new = jnp.exp(log_gamma * C)[:, None, None] * s + jnp.einsum('bhmd,bhme->bhde', k_dec, v)
    return s_new, o


def retention_mixer(q, k, v, g, s0, norm_g, pos, lead, chunk):
    Bsz, L, _ = q.shape
    log_gamma = jnp.log1p(-jnp.exp2(-5.0 - jnp.arange(RET_HEADS, dtype=F32)))
    qh = rope(q.astype(F32).reshape(Bsz, L, RET_HEADS, RET_DK), pos).transpose(0, 2, 1, 3)
    kh = (rope(k.astype(F32).reshape(Bsz, L, RET_HEADS, RET_DK), pos) * RET_DK ** -0.5).transpose(0, 2, 1, 3)
    vh = v.astype(F32).reshape(Bsz, L, RET_HEADS, RET_DV).transpose(0, 2, 1, 3)
    step = lambda st, qkv: retention_chunk(st, qkv, log_gamma)
    s = s0.astype(F32)
    outs = []
    if lead > 0:
        s, o_lead = step(s, (qh[:, :, :lead], kh[:, :, :lead], vh[:, :, :lead]))
        outs.append(o_lead)
    n_chunks = (L - lead) // chunk

    def to_chunks(t):
        return t[:, :, lead:].reshape(Bsz, RET_HEADS, n_chunks, chunk, t.shape[-1]).transpose(2, 0, 1, 3, 4)

    s, o = lax.scan(step, s, (to_chunks(qh), to_chunks(kh), to_chunks(vh)))
    outs.append(o.transpose(1, 2, 0, 3, 4).reshape(Bsz, RET_HEADS, L - lead, RET_DV))
    o = jnp.concatenate(outs, axis=2).transpose(0, 2, 1, 3)
    mu = jnp.mean(o, axis=-1, keepdims=True)
    var = jnp.mean(jnp.square(o - mu), axis=-1, keepdims=True)
    o = (o - mu) * lax.rsqrt(var + EPS) * norm_g.astype(F32).reshape(RET_HEADS, RET_DV)
    y = jax.nn.silu(g.astype(F32)) * o.reshape(Bsz, L, RET_V)
    return y.astype(q.dtype), s


def layer(x, st, p, pos, lead, chunk):
    s5_re0, s5_im0, lru_h0, lru_buf0, ret_s0, ffn_buf0 = st
    Bsz, L, _ = x.shape
    h = rmsnorm(x, p['g_pre_mix'])
    z = h @ p['w_in']
    offs = np.cumsum(IN_SPLITS)[:-1].tolist()
    u_a, x_b, g_b, q, k, v, g_c, gates = jnp.split(z, offs, axis=-1)
    y_a, s5_re, s5_im = s5_mixer(u_a, s5_re0, s5_im0, p['s5_a_re'], p['s5_a_im'], p['s5_log_step'],
                                 p['s5_b_re'], p['s5_b_im'], p['s5_c_re'], p['s5_c_im'], p['s5_d'],
                                 p['s5_w_glu'], p['s5_b_glu'])
    y_b, lru_h, lru_buf = rglru_mixer(x_b, g_b, lru_h0, lru_buf0, p['lru_conv_w'], p['lru_conv_b'],
                                      p['lru_w_a'], p['lru_b_a'], p['lru_w_x'], p['lru_b_x'], p['lru_lam'])
    y_c, ret_s = retention_mixer(q, k, v, g_c, ret_s0, p['ret_norm_g'], pos, lead, chunk)
    gt = jax.nn.sigmoid(gates.reshape(Bsz, L, N_BRANCH, D_MODEL))
    m = (gt[:, :, 0] * (y_a @ p['w_branch_a'])
         + gt[:, :, 1] * (y_b @ p['w_branch_b'])
         + gt[:, :, 2] * (y_c @ p['w_branch_c']))
    x = x + rmsnorm(m @ p['w_out'], p['g_post_mix'])
    h = rmsnorm(x, p['g_pre_ffn'])
    up = h @ p['ffn_w_up']
    gf, uf = jnp.split(up, 2, axis=-1)
    gc, ffn_buf = causal_dwconv(gf, ffn_buf0, p['ffn_conv_w'], p['ffn_conv_b'])
    f = (jax.nn.gelu(gc) * uf) @ p['ffn_w_down']
    x = x + rmsnorm(f, p['g_post_ffn'])
    return x, (s5_re, s5_im, lru_h, lru_buf, ret_s, ffn_buf)


def run_group(x, states, params, pos, lead, chunk):
    new = []
    for l in range(DEPTH):
        p = {name: arr[l] for name, arr in params.items()}
        st = tuple(s[l] for s in states)
        x, st_new = layer(x, st, p, pos, lead, chunk)
        new.append(st_new)
    stacked = tuple(jnp.stack([n[i] for n in new]) for i in range(len(states)))
    return x, stacked


def setup_inputs(seed: int = 0):
    key = jax.random.key(seed)
    ks = iter(jax.random.split(key, 48))
    nrm = lambda shape, scale: scale * jax.random.normal(next(ks), shape, F32)
    x_prompt = nrm((BATCH, SEQ, D_MODEL), 1.0)
    x_sample = nrm((DEC_BATCH, DEC_SEQ, D_MODEL), 1.0)
    state_s5_re = nrm((DEPTH, DEC_BATCH, S5_GROUPS, S5_STATE), 0.5)
    state_s5_im = nrm((DEPTH, DEC_BATCH, S5_GROUPS, S5_STATE), 0.5)
    state_lru = nrm((DEPTH, DEC_BATCH, LRU_WIDTH), 0.5)
    cache_lru_conv = nrm((DEPTH, DEC_BATCH, LRU_CONV - 1, LRU_WIDTH), 1.0)
    state_ret = nrm((DEPTH, DEC_BATCH, RET_HEADS, RET_DK, RET_DV), 0.5)
    cache_ffn_conv = nrm((DEPTH, DEC_BATCH, FFN_CONV - 1, D_FF), 1.0)
    meta_tokens = nrm((N_META, D_MODEL), 1.0)
    g_pre_mix = 1.0 + nrm((DEPTH, D_MODEL), 0.02)
    g_post_mix = 1.0 + nrm((DEPTH, D_MODEL), 0.02)
    g_pre_ffn = 1.0 + nrm((DEPTH, D_MODEL), 0.02)
    g_post_ffn = 1.0 + nrm((DEPTH, D_MODEL), 0.02)
    w_in = nrm((DEPTH, D_MODEL, D_IN), D_MODEL ** -0.5)
    s5_a_re = -0.5 + nrm((DEPTH, S5_GROUPS, S5_STATE), 0.02)
    s5_a_im = jnp.pi * jnp.arange(S5_STATE, dtype=F32) + nrm((DEPTH, S5_GROUPS, S5_STATE), 0.02)
    s5_log_step = jax.random.uniform(next(ks), (DEPTH, S5_GROUPS), F32, math.log(1e-3), math.log(1e-1))
    s5_b_re = nrm((DEPTH, S5_GROUPS, S5_STATE, S5_GROUP), (2 * S5_GROUP) ** -0.5)
    s5_b_im = nrm((DEPTH, S5_GROUPS, S5_STATE, S5_GROUP), (2 * S5_GROUP) ** -0.5)
    s5_c_re = nrm((DEPTH, S5_GROUPS, S5_GROUP, S5_STATE), S5_STATE ** -0.5)
    s5_c_im = nrm((DEPTH, S5_GROUPS, S5_GROUP, S5_STATE), S5_STATE ** -0.5)
    s5_d = nrm((DEPTH, S5_WIDTH), 1.0)
    s5_w_glu = nrm((DEPTH, S5_WIDTH, S5_WIDTH), S5_WIDTH ** -0.5)
    s5_b_glu = nrm((DEPTH, S5_WIDTH), 0.01)
    lru_conv_w = nrm((DEPTH, LRU_CONV, LRU_WIDTH), LRU_CONV ** -0.5)
    lru_conv_b = nrm((DEPTH, LRU_WIDTH), 0.01)
    lru_w_a = nrm((DEPTH, LRU_BLOCKS, LRU_BLOCK, LRU_BLOCK), LRU_BLOCK ** -0.5)
    lru_b_a = nrm((DEPTH, LRU_WIDTH), 0.01)
    lru_w_x = nrm((DEPTH, LRU_BLOCKS, LRU_BLOCK, LRU_BLOCK), LRU_BLOCK ** -0.5)
    lru_b_x = nrm((DEPTH, LRU_WIDTH), 0.01)
    a_c = jax.random.uniform(next(ks), (DEPTH, LRU_WIDTH), F32, 0.9, 0.999) ** (1.0 / LRU_C)
    lru_lam = jnp.log(a_c) - jnp.log1p(-a_c)
    ret_norm_g = 1.0 + nrm((DEPTH, RET_V), 0.02)
    w_branch_a = nrm((DEPTH, S5_WIDTH, D_MODEL), S5_WIDTH ** -0.5)
    w_branch_b = nrm((DEPTH, LRU_WIDTH, D_MODEL), LRU_WIDTH ** -0.5)
    w_branch_c = nrm((DEPTH, RET_V, D_MODEL), RET_V ** -0.5)
    w_out = nrm((DEPTH, D_MODEL, D_MODEL), D_MODEL ** -0.5)
    ffn_w_up = nrm((DEPTH, D_MODEL, 2 * D_FF), D_MODEL ** -0.5)
    ffn_conv_w = nrm((DEPTH, FFN_CONV, D_FF), FFN_CONV ** -0.5)
    ffn_conv_b = nrm((DEPTH, D_FF), 0.01)
    ffn_w_down = nrm((DEPTH, D_FF, D_MODEL), D_FF ** -0.5)
    return {'x_prompt': x_prompt, 'x_sample': x_sample,
            'state_s5_re': state_s5_re, 'state_s5_im': state_s5_im, 'state_lru': state_lru,
            'cache_lru_conv': cache_lru_conv, 'state_ret': state_ret, 'cache_ffn_conv': cache_ffn_conv,
            'meta_tokens': meta_tokens,
            'g_pre_mix': g_pre_mix, 'g_post_mix': g_post_mix, 'g_pre_ffn': g_pre_ffn, 'g_post_ffn': g_post_ffn,
            'w_in': w_in, 's5_a_re': s5_a_re, 's5_a_im': s5_a_im, 's5_log_step': s5_log_step,
            's5_b_re': s5_b_re, 's5_b_im': s5_b_im, 's5_c_re': s5_c_re, 's5_c_im': s5_c_im,
            's5_d': s5_d, 's5_w_glu': s5_w_glu, 's5_b_glu': s5_b_glu,
            'lru_conv_w': lru_conv_w, 'lru_conv_b': lru_conv_b, 'lru_w_a': lru_w_a, 'lru_b_a': lru_b_a,
            'lru_w_x': lru_w_x, 'lru_b_x': lru_b_x, 'lru_lam': lru_lam, 'ret_norm_g': ret_norm_g,
            'w_branch_a': w_branch_a, 'w_branch_b': w_branch_b, 'w_branch_c': w_branch_c, 'w_out': w_out,
            'ffn_w_up': ffn_w_up, 'ffn_conv_w': ffn_conv_w, 'ffn_conv_b': ffn_conv_b, 'ffn_w_down': ffn_w_down}


def reference(x_prompt, x_sample, state_s5_re, state_s5_im, state_lru, cache_lru_conv, state_ret,
              cache_ffn_conv, meta_tokens, g_pre_mix, g_post_mix, g_pre_ffn, g_post_ffn, w_in,
              s5_a_re, s5_a_im, s5_log_step, s5_b_re, s5_b_im, s5_c_re, s5_c_im, s5_d, s5_w_glu, s5_b_glu,
              lru_conv_w, lru_conv_b, lru_w_a, lru_b_a, lru_w_x, lru_b_x, lru_lam, ret_norm_g,
              w_branch_a, w_branch_b, w_branch_c, w_out, ffn_w_up, ffn_conv_w, ffn_conv_b, ffn_w_down):
    params = {'g_pre_mix': g_pre_mix, 'g_post_mix': g_post_mix, 'g_pre_ffn': g_pre_ffn, 'g_post_ffn': g_post_ffn,
              'w_in': w_in, 's5_a_re': s5_a_re, 's5_a_im': s5_a_im, 's5_log_step': s5_log_step,
              's5_b_re': s5_b_re, 's5_b_im': s5_b_im, 's5_c_re': s5_c_re, 's5_c_im': s5_c_im,
              's5_d': s5_d, 's5_w_glu': s5_w_glu, 's5_b_glu': s5_b_glu,
              'lru_conv_w': lru_conv_w, 'lru_conv_b': lru_conv_b, 'lru_w_a': lru_w_a, 'lru_b_a': lru_b_a,
              'lru_w_x': lru_w_x, 'lru_b_x': lru_b_x, 'lru_lam': lru_lam, 'ret_norm_g': ret_norm_g,
              'w_branch_a': w_branch_a, 'w_branch_b': w_branch_b, 'w_branch_c': w_branch_c, 'w_out': w_out,
              'ffn_w_up': ffn_w_up, 'ffn_conv_w': ffn_conv_w, 'ffn_conv_b': ffn_conv_b, 'ffn_w_down': ffn_w_down}
    bp, sp, _ = x_prompt.shape
    meta = jnp.broadcast_to(meta_tokens.astype(x_prompt.dtype)[None], (bp, N_META, D_MODEL))
    xp = jnp.concatenate([meta, x_prompt], axis=1)
    zero_states = (jnp.zeros((DEPTH, bp, S5_GROUPS, S5_STATE), F32),
                   jnp.zeros((DEPTH, bp, S5_GROUPS, S5_STATE), F32),
                   jnp.zeros((DEPTH, bp, LRU_WIDTH), F32),
                   jnp.zeros((DEPTH, bp, LRU_CONV - 1, LRU_WIDTH), x_prompt.dtype),
                   jnp.zeros((DEPTH, bp, RET_HEADS, RET_DK, RET_DV), F32),
                   jnp.zeros((DEPTH, bp, FFN_CONV - 1, D_FF), x_prompt.dtype))
    pos_p = jnp.arange(N_META + sp)
    yp, (p_s5_re, p_s5_im, p_lru_h, p_lru_conv, p_ret, p_ffn_conv) = run_group(
        xp, zero_states, params, pos_p, N_META, RET_CHUNK)
    y_prompt = yp[:, N_META:]
    ds = x_sample.shape[1]
    pos_s = PAST_LEN + jnp.arange(ds)
    y_sample, (s_s5_re, s_s5_im, s_lru_h, s_lru_conv, s_ret, s_ffn_conv) = run_group(
        x_sample, (state_s5_re, state_s5_im, state_lru, cache_lru_conv, state_ret, cache_ffn_conv),
        params, pos_s, 0, ds)
    return (y_prompt, y_sample, p_s5_re, p_s5_im, p_lru_h, p_lru_conv, p_ret, p_ffn_conv,
            s_s5_re, s_s5_im, s_lru_h, s_lru_conv, s_ret, s_ffn_conv)
```

```python
import functools
import math

import jax
import jax.numpy as jnp
import numpy as np
from jax import lax
from jax.experimental import pallas as pl
from jax.experimental.pallas import tpu as pltpu

F32 = jnp.float32
BF16 = jnp.bfloat16

D_MODEL = 1024
DEPTH = 2
PAST_LEN = 16384
N_META = 16
S5_WIDTH = 768
S5_GROUP = 16
S5_GROUPS = S5_WIDTH // S5_GROUP
S5_STATE = 64
S5_N = S5_GROUPS * S5_STATE
LRU_WIDTH = 768
LRU_BLOCK = 64
LRU_BLOCKS = LRU_WIDTH // LRU_BLOCK
LRU_CONV = 4
LRU_C = 8.0
RET_HEADS = 8
RET_DK = 64
RET_DV = 128
RET_QK = RET_HEADS * RET_DK
RET_V = RET_HEADS * RET_DV
RET_CHUNK = 128
ROPE_BASE = 10000.0
N_BRANCH = 3
D_FF = 2816
FFN_CONV = 3
EPS = 1e-6
D_IN = S5_WIDTH + 2 * LRU_WIDTH + 2 * RET_QK + 2 * RET_V + N_BRANCH * D_MODEL

Z_GATES = 0
Z_RET = N_BRANCH * D_MODEL
Z_RET_W = 2 * RET_V + 2 * RET_QK
Z_UA = Z_RET + Z_RET_W
Z_XB = Z_UA + S5_WIDTH
Z_GB = Z_XB + LRU_WIDTH

MXU_K = 256
S5_KT = S5_WIDTH // MXU_K
S5_NT = S5_N // S5_KT
LRU_KT = LRU_WIDTH // MXU_K

SUBLANES = 8
VMEM_LIMIT = 56 * 1024 * 1024


def _cparams(sem):
    return pltpu.CompilerParams(dimension_semantics=sem, vmem_limit_bytes=VMEM_LIMIT)


def _pick_rows(n_steps, bb, target_rows):
    best = None
    for tt in range(1, n_steps + 1):
        if n_steps % tt:
            continue
        rows = tt * bb
        if rows % 16 == 0 and rows <= target_rows:
            best = rows
    if best is None:
        best = n_steps * bb
    return best


def _rms(x, g):
    return x * lax.rsqrt(jnp.mean(x * x, axis=-1, keepdims=True) + EPS) * g


def _sigmoid(x):
    return 1.0 / (1.0 + jnp.exp(-x))


def _norm_matmul_kernel(x_ref, g_ref, w_ref, o_ref, xn_ref):
    @pl.when(pl.program_id(1) == 0)
    def _():
        xn_ref[...] = _rms(x_ref[...], g_ref[...]).astype(BF16)

    o_ref[...] = jnp.dot(xn_ref[...], w_ref[...],
                         preferred_element_type=F32).astype(o_ref.dtype)


def _norm_matmul(x, g, w, tm, tn):
    rows, d = x.shape
    n = w.shape[1]
    return pl.pallas_call(
        _norm_matmul_kernel,
        grid=(rows // tm, n // tn),
        in_specs=[pl.BlockSpec((tm, d), lambda i, j: (i, 0)),
                  pl.BlockSpec((1, d), lambda i, j: (0, 0)),
                  pl.BlockSpec((d, tn), lambda i, j: (0, j))],
        out_specs=pl.BlockSpec((tm, tn), lambda i, j: (i, j)),
        out_shape=jax.ShapeDtypeStruct((rows, n), BF16),
        scratch_shapes=[pltpu.VMEM((tm, d), BF16)],
        compiler_params=_cparams(("arbitrary", "arbitrary")),
        name="norm_in_proj",
    )(x, g, w)


def _s5_disc_kernel(lre_ref, lim_ref, lstep_ref, bre_ref, bim_ref, wb_ref, are_ref, aim_ref):
    dt = jnp.exp(lstep_ref[...])
    lr = lre_ref[...]
    li = lim_ref[...]
    mag = jnp.exp(lr * dt)
    a_re = mag * jnp.cos(li * dt)
    a_im = mag * jnp.sin(li * dt)
    den = lr * lr + li * li
    nr = a_re - 1.0
    cr = (nr * lr + a_im * li) / den
    ci = (a_im * lr - nr * li) / den
    are_ref[...] = jnp.broadcast_to(a_re, are_ref.shape)
    aim_ref[...] = jnp.broadcast_to(a_im, aim_ref.shape)
    b_r = bre_ref[0]
    b_i = bim_ref[0]
    wb_ref[0, :, 0:S5_NT] = (cr * b_r - ci * b_i).astype(BF16)
    wb_ref[0, :, S5_NT:2 * S5_NT] = (cr * b_i + ci * b_r).astype(BF16)


def _s5_discretize(p):
    vec = pl.BlockSpec((1, S5_NT), lambda k: (0, k))
    blk = pl.BlockSpec((1, MXU_K, S5_NT), lambda k: (k, 0, 0))
    return pl.pallas_call(
        _s5_disc_kernel,
        grid=(S5_KT,),
        in_specs=[vec, vec, vec, blk, blk],
        out_specs=[pl.BlockSpec((1, MXU_K, 2 * S5_NT), lambda k: (k, 0, 0)),
                   pl.BlockSpec((SUBLANES, S5_NT), lambda k: (0, k)),
                   pl.BlockSpec((SUBLANES, S5_NT), lambda k: (0, k))],
        out_shape=[jax.ShapeDtypeStruct((S5_KT, MXU_K, 2 * S5_NT), BF16),
                   jax.ShapeDtypeStruct((SUBLANES, S5_N), F32),
                   jax.ShapeDtypeStruct((SUBLANES, S5_N), F32)],
        compiler_params=_cparams(("arbitrary",)),
        name="s5_discretize",
    )(p['s5_lre'], p['s5_lim'], p['s5_lstep'], p['s5_bre'], p['s5_bim'])


def _s5_kernel(u_ref, wb_ref, are_ref, aim_ref, cre_ref, cim_ref,
               d_ref, wglu_ref, bglu_ref, h0re_ref, h0im_ref,
               y_ref, hre_out, him_out,
               hre_ref, him_ref, sre_ref, sim_ref, *, bb, tc):
    @pl.when(pl.program_id(0) == 0)
    def _():
        hre_ref[...] = h0re_ref[...]
        him_ref[...] = h0im_ref[...]

    u = u_ref[...]
    for kt in range(S5_KT):
        r = jnp.dot(u[:, kt * MXU_K:(kt + 1) * MXU_K], wb_ref[kt], preferred_element_type=F32)
        sre_ref[:, kt * S5_NT:(kt + 1) * S5_NT] = r[:, 0:S5_NT]
        sim_ref[:, kt * S5_NT:(kt + 1) * S5_NT] = r[:, S5_NT:2 * S5_NT]

    def scan_tile(rt, carry):
        r0 = rt * SUBLANES
        for kt in range(S5_KT):
            sl = slice(kt * S5_NT, (kt + 1) * S5_NT)
            ar = are_ref[:, sl]
            ai = aim_ref[:, sl]
            hr0 = hre_ref[pl.ds(r0, SUBLANES), sl]
            hi0 = him_ref[pl.ds(r0, SUBLANES), sl]

            def step(t, h):
                hr, hi = h
                row = pl.multiple_of(t * bb + r0, SUBLANES)
                nr_ = ar * hr - ai * hi + sre_ref[pl.ds(row, SUBLANES), sl]
                ni_ = ar * hi + ai * hr + sim_ref[pl.ds(row, SUBLANES), sl]
                sre_ref[pl.ds(row, SUBLANES), sl] = nr_
                sim_ref[pl.ds(row, SUBLANES), sl] = ni_
                return nr_, ni_

            hr1, hi1 = lax.fori_loop(0, tc, step, (hr0, hi0), unroll=2 if tc % 2 == 0 else 1)
            hre_ref[pl.ds(r0, SUBLANES), sl] = hr1
            him_ref[pl.ds(r0, SUBLANES), sl] = hi1
        return carry

    if bb == SUBLANES:
        scan_tile(0, 0)
    else:
        lax.fori_loop(0, bb // SUBLANES, scan_tile, 0)

    hre_out[...] = hre_ref[...]
    him_out[...] = him_ref[...]

    ys = []
    for kt in range(S5_KT):
        sl = slice(kt * S5_NT, (kt + 1) * S5_NT)
        ys.append(jnp.dot(sre_ref[:, sl].astype(BF16), cre_ref[kt], preferred_element_type=F32)
                  - jnp.dot(sim_ref[:, sl].astype(BF16), cim_ref[kt], preferred_element_type=F32))
    y = jnp.concatenate(ys, axis=-1) + d_ref[...] * u.astype(F32)
    z = jax.nn.gelu(y)
    gate = jnp.dot(z.astype(BF16), wglu_ref[...], preferred_element_type=F32) + bglu_ref[...]
    y_ref[...] = (z * _sigmoid(gate)).astype(y_ref.dtype)


def _s5_mixer(z, h0_re, h0_im, p, bb, n_steps, target_rows):
    rows_total = z.shape[0]
    rows = _pick_rows(n_steps, bb, target_rows)
    tc = rows // bb
    const2 = lambda i: (0, 0)
    const3 = lambda i: (0, 0, 0)
    kern = functools.partial(_s5_kernel, bb=bb, tc=tc)
    return pl.pallas_call(
        kern,
        grid=(rows_total // rows,),
        in_specs=[pl.BlockSpec((rows, S5_WIDTH), lambda i: (i, Z_UA // S5_WIDTH)),
                  pl.BlockSpec((S5_KT, MXU_K, 2 * S5_NT), const3),
                  pl.BlockSpec((SUBLANES, S5_N), const2),
                  pl.BlockSpec((SUBLANES, S5_N), const2),
                  pl.BlockSpec((S5_KT, S5_NT, MXU_K), const3),
                  pl.BlockSpec((S5_KT, S5_NT, MXU_K), const3),
                  pl.BlockSpec((1, S5_WIDTH), const2),
                  pl.BlockSpec((S5_WIDTH, S5_WIDTH), const2),
                  pl.BlockSpec((1, S5_WIDTH), const2),
                  pl.BlockSpec((bb, S5_N), const2),
                  pl.BlockSpec((bb, S5_N), const2)],
        out_specs=[pl.BlockSpec((rows, S5_WIDTH), lambda i: (i, 0)),
                   pl.BlockSpec((bb, S5_N), const2),
                   pl.BlockSpec((bb, S5_N), const2)],
        out_shape=[jax.ShapeDtypeStruct((rows_total, S5_WIDTH), BF16),
                   jax.ShapeDtypeStruct((bb, S5_N), F32),
                   jax.ShapeDtypeStruct((bb, S5_N), F32)],
        scratch_shapes=[pltpu.VMEM((bb, S5_N), F32),
                        pltpu.VMEM((bb, S5_N), F32),
                        pltpu.VMEM((rows, S5_N), F32),
                        pltpu.VMEM((rows, S5_N), F32)],
        compiler_params=_cparams(("arbitrary",)),
        name="s5_mixer",
    )(z, p['s5_wb'], p['s5_are'], p['s5_aim'], p['s5_cre'], p['s5_cim'],
      p['s5_d'], p['s5_w_glu'], p['s5_b_glu'], h0_re, h0_im)


def _lru_kernel(xb_ref, gb_ref, cache_ref, cw_ref, cb_ref, wg_ref, ba_ref, bx_ref, lam_ref, h0_ref,
                y_ref, h_out, cache_out,
                ext_ref, a_ref, b_ref, h_ref, *, bb, tc):
    rows = tc * bb
    halo = (LRU_CONV - 1) * bb

    @pl.when(pl.program_id(0) == 0)
    def _():
        ext_ref[0:halo, :] = cache_ref[...]
        h_ref[...] = h0_ref[...]

    ext_ref[halo:halo + rows, :] = xb_ref[...].astype(F32)
    xc = cb_ref[...] + cw_ref[0:1, :] * ext_ref[0:rows, :]
    for j in range(1, LRU_CONV):
        xc = xc + cw_ref[j:j + 1, :] * ext_ref[j * bb:j * bb + rows, :]
    new_cache = ext_ref[rows:rows + halo, :]
    cache_out[...] = new_cache
    ext_ref[0:halo, :] = new_cache

    xcb = xc.astype(BF16)
    rs, is_ = [], []
    for kt in range(LRU_KT):
        g = jnp.dot(xcb[:, kt * MXU_K:(kt + 1) * MXU_K], wg_ref[kt], preferred_element_type=F32)
        rs.append(g[:, 0:MXU_K])
        is_.append(g[:, MXU_K:2 * MXU_K])
    r = _sigmoid(jnp.concatenate(rs, axis=-1) + ba_ref[...])
    ig = _sigmoid(jnp.concatenate(is_, axis=-1) + bx_ref[...])
    nl = -lam_ref[...]
    softplus = jnp.maximum(nl, 0.0) + jnp.log1p(jnp.exp(-jnp.abs(nl)))
    log_a = (-LRU_C) * r * softplus
    a = jnp.exp(log_a)
    mult = jnp.sqrt(-jnp.tanh(log_a) * (a * a + 1.0))
    a_ref[...] = a
    b_ref[...] = mult * (ig * xc)

    def scan_tile(rt, carry):
        r0 = rt * SUBLANES
        h0 = h_ref[pl.ds(r0, SUBLANES), :]

        def step(t, h):
            row = pl.multiple_of(t * bb + r0, SUBLANES)
            h = a_ref[pl.ds(row, SUBLANES), :] * h + b_ref[pl.ds(row, SUBLANES), :]
            b_ref[pl.ds(row, SUBLANES), :] = h
            return h

        h1 = lax.fori_loop(0, tc, step, h0, unroll=2 if tc % 2 == 0 else 1)
        h_ref[pl.ds(r0, SUBLANES), :] = h1
        return carry

    if bb == SUBLANES:
        scan_tile(0, 0)
    else:
        lax.fori_loop(0, bb // SUBLANES, scan_tile, 0)

    h_out[...] = h_ref[...]
    y_ref[...] = (b_ref[...] * jax.nn.gelu(gb_ref[...].astype(F32))).astype(y_ref.dtype)


def _lru_mixer(z, h0, cache, p, bb, n_steps, target_rows):
    rows_total = z.shape[0]
    rows = _pick_rows(n_steps, bb, target_rows)
    tc = rows // bb
    halo = (LRU_CONV - 1) * bb
    const2 = lambda i: (0, 0)
    vec = pl.BlockSpec((1, LRU_WIDTH), const2)
    kern = functools.partial(_lru_kernel, bb=bb, tc=tc)
    return pl.pallas_call(
        kern,
        grid=(rows_total // rows,),
        in_specs=[pl.BlockSpec((rows, LRU_WIDTH), lambda i: (i, Z_XB // LRU_WIDTH)),
                  pl.BlockSpec((rows, LRU_WIDTH), lambda i: (i, Z_GB // LRU_WIDTH)),
                  pl.BlockSpec((halo, LRU_WIDTH), const2),
                  pl.BlockSpec((LRU_CONV, LRU_WIDTH), const2),
                  vec,
                  pl.BlockSpec((LRU_KT, MXU_K, 2 * MXU_K), lambda i: (0, 0, 0)),
                  vec, vec, vec,
                  pl.BlockSpec((bb, LRU_WIDTH), const2)],
        out_specs=[pl.BlockSpec((rows, LRU_WIDTH), lambda i: (i, 0)),
                   pl.BlockSpec((bb, LRU_WIDTH), const2),
                   pl.BlockSpec((halo, LRU_WIDTH), const2)],
        out_shape=[jax.ShapeDtypeStruct((rows_total, LRU_WIDTH), BF16),
                   jax.ShapeDtypeStruct((bb, LRU_WIDTH), F32),
                   jax.ShapeDtypeStruct((halo, LRU_WIDTH), F32)],
        scratch_shapes=[pltpu.VMEM((rows + halo, LRU_WIDTH), F32),
                        pltpu.VMEM((rows, LRU_WIDTH), F32),
                        pltpu.VMEM((rows, LRU_WIDTH), F32),
                        pltpu.VMEM((bb, LRU_WIDTH), F32)],
        compiler_params=_cparams(("arbitrary",)),
        name="rglru_mixer",
    )(z, z, cache, p['lru_conv_w'], p['lru_conv_b'], p['lru_wg'], p['lru_b_a'], p['lru_b_x'],
      p['lru_lam'], h0)


def _ret_kernel(z_ref, cos_ref, sin_ref, decay_ref, kdec_ref, cross_ref, sdec_ref, ng_ref, s0_ref,
                y_ref, s_out, *, resident_state):
    slot = pl.program_id(1) if resident_state else 0
    state_in = s_out if resident_state else s0_ref

    if resident_state:
        @pl.when(pl.program_id(0) == 0)
        def _():
            s_out[slot] = s0_ref[0]

    blk = z_ref[0]
    v = blk[:, 0:RET_V]
    g = blk[:, RET_V:2 * RET_V].astype(F32)
    q = blk[:, 2 * RET_V:2 * RET_V + RET_QK].astype(F32)
    k = blk[:, 2 * RET_V + RET_QK:2 * RET_V + 2 * RET_QK].astype(F32)
    cos = cos_ref[...]
    sin = sin_ref[...]
    half = RET_DK // 2
    lane = lax.broadcasted_iota(jnp.int32, q.shape, 1)
    first = (lane % RET_DK) < half

    def rope(x):
        swapped = jnp.where(first, pltpu.roll(x, RET_QK - half, 1), pltpu.roll(x, half, 1))
        return x * cos + swapped * sin

    qr = rope(q).astype(BF16)
    kr = rope(k) * (RET_DK ** -0.5)
    kd = (kr * kdec_ref[...]).astype(BF16)
    kr = kr.astype(BF16)
    cross = cross_ref[0]
    sdec = sdec_ref[0]
    ng = ng_ref[...]
    outs = []
    for h in range(RET_HEADS):
        qh = qr[:, h * RET_DK:(h + 1) * RET_DK]
        kh = kr[:, h * RET_DK:(h + 1) * RET_DK]
        kdh = kd[:, h * RET_DK:(h + 1) * RET_DK]
        vh = v[:, h * RET_DV:(h + 1) * RET_DV]
        s = state_in[slot, h]
        sc = lax.dot_general(qh, kh, (((1,), (1,)), ((), ())), preferred_element_type=F32)
        sc = (sc * decay_ref[h]).astype(BF16)
        o = jnp.dot(sc, vh, preferred_element_type=F32)
        o = o + jnp.dot(qh, s.astype(BF16), preferred_element_type=F32) * cross[:, h * RET_DV:(h + 1) * RET_DV]
        s_new = sdec[h:h + 1, :] * s + lax.dot_general(kdh, vh, (((0,), (0,)), ((), ())),
                                                        preferred_element_type=F32)
        s_out[slot, h] = s_new
        mu = jnp.mean(o, axis=-1, keepdims=True)
        oc = o - mu
        var = jnp.mean(oc * oc, axis=-1, keepdims=True)
        outs.append(oc * lax.rsqrt(var + EPS) * ng[:, h * RET_DV:(h + 1) * RET_DV])
    on = jnp.concatenate(outs, axis=-1)
    y_ref[0] = (g * _sigmoid(g) * on).astype(y_ref.dtype)


def _ret_mixer(zr, s0, norm_g, tabs):
    nb, lp, _ = zr.shape
    c = tabs['chunk']
    nc = lp // c
    resident = nc > 1
    kern = functools.partial(_ret_kernel, resident_state=resident)
    state_block = (nb if resident else 1, RET_HEADS, RET_DK, RET_DV)
    state_map = (lambda i, b: (0, 0, 0, 0)) if resident else (lambda i, b: (b, 0, 0, 0))
    return pl.pallas_call(
        kern,
        grid=(nc, nb),
        in_specs=[pl.BlockSpec((1, c, Z_RET_W), lambda i, b: (b, i, 0)),
                  pl.BlockSpec((c, RET_QK), lambda i, b: (i, 0)),
                  pl.BlockSpec((c, RET_QK), lambda i, b: (i, 0)),
                  pl.BlockSpec((RET_HEADS, c, c), lambda i, b: (0, 0, 0)),
                  pl.BlockSpec((c, RET_QK), lambda i, b: (0, 0)),
                  pl.BlockSpec((1, c, RET_V), lambda i, b: (i, 0, 0)),
                  pl.BlockSpec((1, RET_HEADS, RET_DV), lambda i, b: (i, 0, 0)),
                  pl.BlockSpec((1, RET_V), lambda i, b: (0, 0)),
                  pl.BlockSpec((1, RET_HEADS, RET_DK, RET_DV), lambda i, b: (b, 0, 0, 0))],
        out_specs=[pl.BlockSpec((1, c, RET_V), lambda i, b: (b, i, 0)),
                   pl.BlockSpec(state_block, state_map)],
        out_shape=[jax.ShapeDtypeStruct((nb, lp, RET_V), BF16),
                   jax.ShapeDtypeStruct((nb, RET_HEADS, RET_DK, RET_DV), F32)],
        compiler_params=_cparams(("arbitrary", "arbitrary")),
        name="retention_mixer",
    )(zr, tabs['cos'], tabs['sin'], tabs['decay'], tabs['kdec'], tabs['cross'], tabs['sdec'], norm_g, s0)


def _ret_tables(pos, n_real_per_chunk, chunk):
    nc = len(n_real_per_chunk)
    half = RET_DK // 2
    inv = jnp.power(ROPE_BASE, -jnp.arange(half, dtype=F32) / half)
    ang = jnp.asarray(pos, F32)[:, None] * inv
    cos_h = jnp.concatenate([jnp.cos(ang), jnp.cos(ang)], axis=-1)
    sin_h = jnp.concatenate([-jnp.sin(ang), jnp.sin(ang)], axis=-1)
    cos = jnp.tile(cos_h, (1, RET_HEADS))
    sin = jnp.tile(sin_h, (1, RET_HEADS))
    log_gamma = jnp.log1p(-jnp.exp2(-5.0 - jnp.arange(RET_HEADS, dtype=F32)))
    n = jnp.arange(chunk, dtype=F32)
    diff = n[:, None] - n[None, :]
    decay = jnp.where(diff >= 0, jnp.exp(log_gamma[:, None, None] * jnp.maximum(diff, 0.0)), 0.0)
    kdec = jnp.exp(log_gamma[:, None] * (chunk - 1.0 - n))
    kdec = jnp.repeat(kdec.T, RET_DK, axis=1)
    cross, sdec = [], []
    for nr in n_real_per_chunk:
        m = n - (chunk - nr)
        cr = jnp.exp(log_gamma[:, None] * (jnp.maximum(m, 0.0) + 1.0))
        cross.append(jnp.repeat(cr.T, RET_DV, axis=1))
        sdec.append(jnp.broadcast_to(jnp.exp(log_gamma * nr)[:, None], (RET_HEADS, RET_DV)))
    return {'chunk': chunk, 'cos': cos, 'sin': sin, 'decay': decay, 'kdec': kdec,
            'cross': jnp.stack(cross), 'sdec': jnp.stack(sdec)}


def _merge_kernel(x_ref, gt_ref, ya_ref, yb_ref, yc_ref, wa_ref, wb_ref, wc_ref, wo_ref, g_ref, o_ref):
    gate = lambda j: _sigmoid(gt_ref[:, j * D_MODEL:(j + 1) * D_MODEL].astype(F32))
    m = gate(0) * jnp.dot(ya_ref[...], wa_ref[...], preferred_element_type=F32)
    m = m + gate(1) * jnp.dot(yb_ref[...], wb_ref[...], preferred_element_type=F32)
    m = m + gate(2) * jnp.dot(yc_ref[...], wc_ref[...], preferred_element_type=F32)
    mo = jnp.dot(m.astype(BF16), wo_ref[...], preferred_element_type=F32)
    o_ref[...] = x_ref[...] + _rms(mo, g_ref[...])


def _merge(x, z, ya, yb, yc, p, tm):
    rows = x.shape[0]
    row = lambda i: (i, 0)
    const = lambda i: (0, 0)
    return pl.pallas_call(
        _merge_kernel,
        grid=(rows // tm,),
        in_specs=[pl.BlockSpec((tm, D_MODEL), row),
                  pl.BlockSpec((tm, N_BRANCH * D_MODEL), row),
                  pl.BlockSpec((tm, S5_WIDTH), row),
                  pl.BlockSpec((tm, LRU_WIDTH), row),
                  pl.BlockSpec((tm, RET_V), row),
                  pl.BlockSpec((S5_WIDTH, D_MODEL), const),
                  pl.BlockSpec((LRU_WIDTH, D_MODEL), const),
                  pl.BlockSpec((RET_V, D_MODEL), const),
                  pl.BlockSpec((D_MODEL, D_MODEL), const),
                  pl.BlockSpec((1, D_MODEL), const)],
        out_specs=pl.BlockSpec((tm, D_MODEL), row),
        out_shape=jax.ShapeDtypeStruct((rows, D_MODEL), F32),
        compiler_params=_cparams(("arbitrary",)),
        name="branch_merge",
    )(x, z, ya, yb, yc, p['w_branch_a'], p['w_branch_b'], p['w_branch_c'], p['w_out'], p['g_post_mix'])


def _ffn_kernel(x_ref, gpre_ref, wg_ref, wu_ref, cw_ref, cb_ref, wd_ref, gpost_ref, cache_ref,
                o_ref, cache_out, xn_ref, acc_ref, ext_ref, *, bb, n_chunks):
    i = pl.program_id(0)
    c = pl.program_id(1)
    rows = x_ref.shape[0]
    halo = (FFN_CONV - 1) * bb

    @pl.when(c == 0)
    def _():
        xn_ref[...] = _rms(x_ref[...], gpre_ref[...]).astype(BF16)

    @pl.when(i == 0)
    def _():
        ext_ref[c, 0:halo, :] = cache_ref[...]

    xn = xn_ref[...]
    gf = jnp.dot(xn, wg_ref[...], preferred_element_type=F32)
    uf = jnp.dot(xn, wu_ref[...], preferred_element_type=F32)
    ext_ref[c, halo:halo + rows, :] = gf
    gc = cb_ref[...] + cw_ref[0:1, :] * ext_ref[c, 0:rows, :]
    for j in range(1, FFN_CONV):
        gc = gc + cw_ref[j:j + 1, :] * ext_ref[c, j * bb:j * bb + rows, :]
    new_cache = ext_ref[c, rows:rows + halo, :]
    cache_out[c] = new_cache
    ext_ref[c, 0:halo, :] = new_cache
    act = (jax.nn.gelu(gc) * uf).astype(BF16)
    part = jnp.dot(act, wd_ref[...], preferred_element_type=F32)

    @pl.when(c == 0)
    def _():
        acc_ref[...] = part

    @pl.when(c > 0)
    def _():
        acc_ref[...] += part

    @pl.when(c == n_chunks - 1)
    def _():
        o_ref[...] = x_ref[...] + _rms(acc_ref[...], gpost_ref[...])


def _ffn(x, cache, p, bb, tm, tf):
    rows = x.shape[0]
    halo = (FFN_CONV - 1) * bb
    n_chunks = D_FF // tf
    kern = functools.partial(_ffn_kernel, bb=bb, n_chunks=n_chunks)
    return pl.pallas_call(
        kern,
        grid=(rows // tm, n_chunks),
        in_specs=[pl.BlockSpec((tm, D_MODEL), lambda i, c: (i, 0)),
                  pl.BlockSpec((1, D_MODEL), lambda i, c: (0, 0)),
                  pl.BlockSpec((D_MODEL, tf), lambda i, c: (0, c)),
                  pl.BlockSpec((D_MODEL, tf), lambda i, c: (0, c + n_chunks)),
                  pl.BlockSpec((FFN_CONV, tf), lambda i, c: (0, c)),
                  pl.BlockSpec((1, tf), lambda i, c: (0, c)),
                  pl.BlockSpec((tf, D_MODEL), lambda i, c: (c, 0)),
                  pl.BlockSpec((1, D_MODEL), lambda i, c: (0, 0)),
                  pl.BlockSpec((halo, tf), lambda i, c: (0, c))],
        out_specs=[pl.BlockSpec((tm, D_MODEL), lambda i, c: (i, 0)),
                   pl.BlockSpec((n_chunks, halo, tf), lambda i, c: (0, 0, 0))],
        out_shape=[jax.ShapeDtypeStruct((rows, D_MODEL), F32),
                   jax.ShapeDtypeStruct((n_chunks, halo, tf), F32)],
        scratch_shapes=[pltpu.VMEM((tm, D_MODEL), BF16),
                        pltpu.VMEM((tm, D_MODEL), F32),
                        pltpu.VMEM((n_chunks, tm + halo, tf), F32)],
        compiler_params=_cparams(("arbitrary", "arbitrary")),
        name="conv_ffn",
    )(x, p['g_pre_ffn'], p['ffn_w_up'], p['ffn_w_up'], p['ffn_conv_w'], p['ffn_conv_b'],
      p['ffn_w_down'], p['g_post_ffn'], cache)


def _block_diag(w, tiles):
    nb, r, c = w.shape
    per = nb // tiles
    w4 = w.reshape(tiles, per, r, c)
    eye = jnp.eye(per, dtype=w.dtype)
    return jnp.einsum('kgrc,gh->kgrhc', w4, eye).reshape(tiles, per * r, per * c)


def _prep_layer(P, l):
    row = lambda a: a[l].reshape(1, -1).astype(F32)
    w_in = P['w_in'][l]
    o = np.cumsum([0, S5_WIDTH, LRU_WIDTH, LRU_WIDTH, RET_QK, RET_QK, RET_V, RET_V, N_BRANCH * D_MODEL])
    ua, xb, gb, q, k, v, gc, gates = [w_in[:, o[i]:o[i + 1]] for i in range(8)]
    p = {
        'g_pre_mix': row(P['g_pre_mix']), 'g_post_mix': row(P['g_post_mix']),
        'g_pre_ffn': row(P['g_pre_ffn']), 'g_post_ffn': row(P['g_post_ffn']),
        'w_in': jnp.concatenate([gates, v, gc, q, k, ua, xb, gb], axis=1).astype(BF16),
        's5_lre': row(P['s5_a_re']), 's5_lim': row(P['s5_a_im']),
        's5_lstep': jnp.repeat(P['s5_log_step'][l], S5_STATE).reshape(1, S5_N),
        's5_bre': _block_diag(jnp.swapaxes(P['s5_b_re'][l], 1, 2), S5_KT),
        's5_bim': _block_diag(jnp.swapaxes(P['s5_b_im'][l], 1, 2), S5_KT),
        's5_cre': _block_diag(jnp.swapaxes(P['s5_c_re'][l], 1, 2), S5_KT).astype(BF16),
        's5_cim': _block_diag(jnp.swapaxes(P['s5_c_im'][l], 1, 2), S5_KT).astype(BF16),
        's5_d': row(P['s5_d']), 's5_w_glu': P['s5_w_glu'][l].astype(BF16), 's5_b_glu': row(P['s5_b_glu']),
        'lru_conv_w': P['lru_conv_w'][l], 'lru_conv_b': row(P['lru_conv_b']),
        'lru_wg': jnp.concatenate([_block_diag(P['lru_w_a'][l], LRU_KT),
                                   _block_diag(P['lru_w_x'][l], LRU_KT)], axis=-1).astype(BF16),
        'lru_b_a': row(P['lru_b_a']), 'lru_b_x': row(P['lru_b_x']), 'lru_lam': row(P['lru_lam']),
        'ret_norm_g': row(P['ret_norm_g']),
        'w_branch_a': P['w_branch_a'][l].astype(BF16), 'w_branch_b': P['w_branch_b'][l].astype(BF16),
        'w_branch_c': P['w_branch_c'][l].astype(BF16), 'w_out': P['w_out'][l].astype(BF16),
        'ffn_w_up': P['ffn_w_up'][l].astype(BF16), 'ffn_conv_w': P['ffn_conv_w'][l],
        'ffn_conv_b': row(P['ffn_conv_b']), 'ffn_w_down': P['ffn_w_down'][l].astype(BF16),
    }
    p['s5_wb'], p['s5_are'], p['s5_aim'] = _s5_discretize(p)
    return p


def _run_group(x, states, params, bb, n_steps, ret_pad, tabs, tiles):
    s5_re, s5_im, lru_h, lru_cache, ret_s, ffn_cache = states
    new = [[] for _ in range(6)]
    for l in range(DEPTH):
        p = params[l]
        z = _norm_matmul(x, p['g_pre_mix'], p['w_in'], tiles['tm_in'], tiles['tn_in'])
        ya, hre, him = _s5_mixer(z, s5_re[l], s5_im[l], p, bb, n_steps, tiles['rows_scan'])
        yb, hl, lcache = _lru_mixer(z, lru_h[l], lru_cache[l], p, bb, n_steps, tiles['rows_scan'])
        zr = z[:, Z_RET:Z_RET + Z_RET_W].reshape(n_steps, bb, Z_RET_W).transpose(1, 0, 2)
        zr = jnp.pad(zr, ((0, 0), (ret_pad, 0), (0, 0)))
        yc, rs = _ret_mixer(zr, ret_s[l], p['ret_norm_g'], tabs)
        yc = yc[:, ret_pad:].transpose(1, 0, 2).reshape(n_steps * bb, RET_V)
        x = _merge(x, z, ya, yb, yc, p, tiles['tm'])
        x, fcache = _ffn(x, ffn_cache[l], p, bb, tiles['tm'], tiles['tf'])
        fcache = jnp.swapaxes(fcache, 0, 1).reshape(-1, D_FF)
        for lst, val in zip(new, (hre, him, hl, lcache, rs, fcache)):
            lst.append(val)
    return x, new


def kernel(x_prompt, x_sample, state_s5_re, state_s5_im, state_lru, cache_lru_conv, state_ret, cache_ffn_conv, meta_tokens, g_pre_mix, g_post_mix, g_pre_ffn, g_post_ffn, w_in, s5_a_re, s5_a_im, s5_log_step, s5_b_re, s5_b_im, s5_c_re, s5_c_im, s5_d, s5_w_glu, s5_b_glu, lru_conv_w, lru_conv_b, lru_w_a, lru_b_a, lru_w_x, lru_b_x, lru_lam, ret_norm_g, w_branch_a, w_branch_b, w_branch_c, w_out, ffn_w_up, ffn_conv_w, ffn_conv_b, ffn_w_down):
    P = {'g_pre_mix': g_pre_mix, 'g_post_mix': g_post_mix, 'g_pre_ffn': g_pre_ffn, 'g_post_ffn': g_post_ffn,
         'w_in': w_in, 's5_a_re': s5_a_re, 's5_a_im': s5_a_im, 's5_log_step': s5_log_step,
         's5_b_re': s5_b_re, 's5_b_im': s5_b_im, 's5_c_re': s5_c_re, 's5_c_im': s5_c_im,
         's5_d': s5_d, 's5_w_glu': s5_w_glu, 's5_b_glu': s5_b_glu,
         'lru_conv_w': lru_conv_w, 'lru_conv_b': lru_conv_b, 'lru_w_a': lru_w_a, 'lru_b_a': lru_b_a,
         'lru_w_x': lru_w_x, 'lru_b_x': lru_b_x, 'lru_lam': lru_lam, 'ret_norm_g': ret_norm_g,
         'w_branch_a': w_branch_a, 'w_branch_b': w_branch_b, 'w_branch_c': w_branch_c, 'w_out': w_out,
         'ffn_w_up': ffn_w_up, 'ffn_conv_w': ffn_conv_w, 'ffn_conv_b': ffn_conv_b, 'ffn_w_down': ffn_w_down}
    params = [_prep_layer(P, l) for l in range(DEPTH)]

    bp, sp, _ = x_prompt.shape
    lp = N_META + sp
    meta = jnp.broadcast_to(meta_tokens.astype(F32)[:, None, :], (N_META, bp, D_MODEL))
    xp = jnp.concatenate([meta, jnp.swapaxes(x_prompt, 0, 1)], axis=0).reshape(lp * bp, D_MODEL)
    zeros = lambda *s: [jnp.zeros(s, F32)] * DEPTH
    st_p = (zeros(bp, S5_N), zeros(bp, S5_N), zeros(bp, LRU_WIDTH),
            zeros((LRU_CONV - 1) * bp, LRU_WIDTH), zeros(bp, RET_HEADS, RET_DK, RET_DV),
            zeros((FFN_CONV - 1) * bp, D_FF))
    pad_p = RET_CHUNK - N_META
    n_chunks_p = sp // RET_CHUNK
    pos_p = np.concatenate([np.zeros(pad_p), np.arange(lp)])
    tabs_p = _ret_tables(pos_p, [N_META] + [RET_CHUNK] * n_chunks_p, RET_CHUNK)
    tiles_p = {'tm_in': _pick_rows(lp, bp, 1400), 'tn_in': 1408, 'rows_scan': 700,
               'tm': _pick_rows(lp, bp, 700), 'tf': 1408}
    yp, new_p = _run_group(xp, st_p, params, bp, lp, pad_p, tabs_p, tiles_p)
    y_prompt = jnp.swapaxes(yp.reshape(lp, bp, D_MODEL)[N_META:], 0, 1)

    bs, ds, _ = x_sample.shape
    xs = jnp.swapaxes(x_sample, 0, 1).reshape(ds * bs, D_MODEL)
    tm_major = lambda c: jnp.swapaxes(c, 0, 1).reshape(-1, c.shape[-1])
    st_s = ([state_s5_re[l].reshape(bs, S5_N) for l in range(DEPTH)],
            [state_s5_im[l].reshape(bs, S5_N) for l in range(DEPTH)],
            [state_lru[l] for l in range(DEPTH)],
            [tm_major(cache_lru_conv[l]) for l in range(DEPTH)],
            [state_ret[l] for l in range(DEPTH)],
            [tm_major(cache_ffn_conv[l]) for l in range(DEPTH)])
    chunk_s = 16
    pad_s = chunk_s - ds
    pos_s = np.concatenate([np.zeros(pad_s), PAST_LEN + np.arange(ds)])
    tabs_s = _ret_tables(pos_s, [ds], chunk_s)
    tiles_s = {'tm_in': ds * bs, 'tn_in': 1408, 'rows_scan': ds * bs, 'tm': ds * bs, 'tf': 1408}
    ys, new_s = _run_group(xs, st_s, params, bs, ds, pad_s, tabs_s, tiles_s)
    y_sample = jnp.swapaxes(ys.reshape(ds, bs, D_MODEL), 0, 1)

    def pack(new, b):
        s5r, s5i, lh, lc, rs, fc = [jnp.stack(v) for v in new]
        return (s5r.reshape(DEPTH, b, S5_GROUPS, S5_STATE), s5i.reshape(DEPTH, b, S5_GROUPS, S5_STATE), lh,
                jnp.swapaxes(lc.reshape(DEPTH, LRU_CONV - 1, b, LRU_WIDTH), 1, 2), rs,
                jnp.swapaxes(fc.reshape(DEPTH, FFN_CONV - 1, b, D_FF), 1, 2))

    return (y_prompt, y_sample) + pack(new_p, bp) + pack(new_s, bs)
```

```python
import functools
import math

import jax
import jax.numpy as jnp
import numpy as np
from jax import lax
from jax.experimental import pallas as pl
from jax.experimental.pallas import tpu as pltpu

F32 = jnp.float32
BF16 = jnp.bfloat16

D_MODEL = 1024
DEPTH = 2
PAST_LEN = 16384
N_META = 16
S5_WIDTH = 768
S5_GROUP = 16
S5_GROUPS = S5_WIDTH // S5_GROUP
S5_STATE = 64
S5_N = S5_GROUPS * S5_STATE
LRU_WIDTH = 768
LRU_BLOCK = 64
LRU_BLOCKS = LRU_WIDTH // LRU_BLOCK
LRU_CONV = 4
LRU_C = 8.0
RET_HEADS = 8
RET_DK = 64
RET_DV = 128
RET_QK = RET_HEADS * RET_DK
RET_V = RET_HEADS * RET_DV
RET_CHUNK = 128
ROPE_BASE = 10000.0
N_BRANCH = 3
D_FF = 2816
FFN_CONV = 3
EPS = 1e-6
D_IN = S5_WIDTH + 2 * LRU_WIDTH + 2 * RET_QK + 2 * RET_V + N_BRANCH * D_MODEL

Z_GATES = 0
Z_RET = N_BRANCH * D_MODEL
Z_RET_W = 2 * RET_V + 2 * RET_QK
Z_UA = Z_RET + Z_RET_W
Z_XB = Z_UA + S5_WIDTH
Z_GB = Z_XB + LRU_WIDTH

MXU_K = 256
S5_KT = S5_WIDTH // MXU_K
S5_NT = S5_N // S5_KT
LRU_KT = LRU_WIDTH // MXU_K

SUBLANES = 8
VMEM_LIMIT = 56 * 1024 * 1024


def _cparams(sem):
    return pltpu.CompilerParams(dimension_semantics=sem, vmem_limit_bytes=VMEM_LIMIT)


def _pick_rows(n_steps, bb, target_rows):
    best = None
    for tt in range(1, n_steps + 1):
        if n_steps % tt:
            continue
        rows = tt * bb
        if rows % 16 == 0 and rows <= target_rows:
            best = rows
    if best is None:
        best = n_steps * bb
    return best


def _rms(x, g):
    return x * lax.rsqrt(jnp.mean(x * x, axis=-1, keepdims=True) + EPS) * g


def _sigmoid(x):
    return 1.0 / (1.0 + jnp.exp(-x))


def _norm_matmul_kernel(x_ref, g_ref, w_ref, o_ref, xn_ref):
    @pl.when(pl.program_id(1) == 0)
    def _():
        xn_ref[...] = _rms(x_ref[...], g_ref[...]).astype(BF16)

    o_ref[...] = jnp.dot(xn_ref[...], w_ref[...],
                         preferred_element_type=F32).astype(o_ref.dtype)


def _norm_matmul(x, g, w, tm, tn):
    rows, d = x.shape
    n = w.shape[1]
    return pl.pallas_call(
        _norm_matmul_kernel,
        grid=(rows // tm, n // tn),
        in_specs=[pl.BlockSpec((tm, d), lambda i, j: (i, 0)),
                  pl.BlockSpec((1, d), lambda i, j: (0, 0)),
                  pl.BlockSpec((d, tn), lambda i, j: (0, j))],
        out_specs=pl.BlockSpec((tm, tn), lambda i, j: (i, j)),
        out_shape=jax.ShapeDtypeStruct((rows, n), BF16),
        scratch_shapes=[pltpu.VMEM((tm, d), BF16)],
        compiler_params=_cparams(("arbitrary", "arbitrary")),
        name="norm_in_proj",
    )(x, g, w)


def _s5_disc_kernel(lre_ref, lim_ref, lstep_ref, bre_ref, bim_ref, wb_ref, are_ref, aim_ref):
    dt = jnp.exp(lstep_ref[...])
    lr = lre_ref[...]
    li = lim_ref[...]
    mag = jnp.exp(lr * dt)
    a_re = mag * jnp.cos(li * dt)
    a_im = mag * jnp.sin(li * dt)
    den = lr * lr + li * li
    nr = a_re - 1.0
    cr = (nr * lr + a_im * li) / den
    ci = (a_im * lr - nr * li) / den
    are_ref[...] = jnp.broadcast_to(a_re, are_ref.shape)
    aim_ref[...] = jnp.broadcast_to(a_im, aim_ref.shape)
    b_r = bre_ref[0]
    b_i = bim_ref[0]
    wb_ref[0, :, 0:S5_NT] = (cr * b_r - ci * b_i).astype(BF16)
    wb_ref[0, :, S5_NT:2 * S5_NT] = (cr * b_i + ci * b_r).astype(BF16)


def _s5_discretize(p):
    vec = pl.BlockSpec((1, S5_NT), lambda k: (0, k))
    blk = pl.BlockSpec((1, MXU_K, S5_NT), lambda k: (k, 0, 0))
    return pl.pallas_call(
        _s5_disc_kernel,
        grid=(S5_KT,),
        in_specs=[vec, vec, vec, blk, blk],
        out_specs=[pl.BlockSpec((1, MXU_K, 2 * S5_NT), lambda k: (k, 0, 0)),
                   pl.BlockSpec((SUBLANES, S5_NT), lambda k: (0, k)),
                   pl.BlockSpec((SUBLANES, S5_NT), lambda k: (0, k))],
        out_shape=[jax.ShapeDtypeStruct((S5_KT, MXU_K, 2 * S5_NT), BF16),
                   jax.ShapeDtypeStruct((SUBLANES, S5_N), F32),
                   jax.ShapeDtypeStruct((SUBLANES, S5_N), F32)],
        compiler_params=_cparams(("arbitrary",)),
        name="s5_discretize",
    )(p['s5_lre'], p['s5_lim'], p['s5_lstep'], p['s5_bre'], p['s5_bim'])


def _s5_kernel(u_ref, wb_ref, are_ref, aim_ref, cre_ref, cim_ref,
               d_ref, wglu_ref, bglu_ref, h0re_ref, h0im_ref,
               y_ref, hre_out, him_out,
               hre_ref, him_ref, sre_ref, sim_ref, *, bb, tc):
    @pl.when(pl.program_id(0) == 0)
    def _():
        hre_ref[...] = h0re_ref[...]
        him_ref[...] = h0im_ref[...]

    u = u_ref[...]
    for kt in range(S5_KT):
        r = jnp.dot(u[:, kt * MXU_K:(kt + 1) * MXU_K], wb_ref[kt], preferred_element_type=F32)
        sre_ref[:, kt * S5_NT:(kt + 1) * S5_NT] = r[:, 0:S5_NT]
        sim_ref[:, kt * S5_NT:(kt + 1) * S5_NT] = r[:, S5_NT:2 * S5_NT]

    def scan_tile(rt, carry):
        r0 = rt * SUBLANES
        for kt in range(S5_KT):
            sl = slice(kt * S5_NT, (kt + 1) * S5_NT)
            ar = are_ref[:, sl]
            ai = aim_ref[:, sl]
            hr0 = hre_ref[pl.ds(r0, SUBLANES), sl]
            hi0 = him_ref[pl.ds(r0, SUBLANES), sl]

            def step(t, h):
                hr, hi = h
                row = pl.multiple_of(t * bb + r0, SUBLANES)
                nr_ = ar * hr - ai * hi + sre_ref[pl.ds(row, SUBLANES), sl]
                ni_ = ar * hi + ai * hr + sim_ref[pl.ds(row, SUBLANES), sl]
                sre_ref[pl.ds(row, SUBLANES), sl] = nr_
                sim_ref[pl.ds(row, SUBLANES), sl] = ni_
                return nr_, ni_

            hr1, hi1 = lax.fori_loop(0, tc, step, (hr0, hi0), unroll=2 if tc % 2 == 0 else 1)
            hre_ref[pl.ds(r0, SUBLANES), sl] = hr1
            him_ref[pl.ds(r0, SUBLANES), sl] = hi1
        return carry

    if bb == SUBLANES:
        scan_tile(0, 0)
    else:
        lax.fori_loop(0, bb // SUBLANES, scan_tile, 0)

    hre_out[...] = hre_ref[...]
    him_out[...] = him_ref[...]

    ys = []
    for kt in range(S5_KT):
        sl = slice(kt * S5_NT, (kt + 1) * S5_NT)
        ys.append(jnp.dot(sre_ref[:, sl].astype(BF16), cre_ref[kt], preferred_element_type=F32)
                  - jnp.dot(sim_ref[:, sl].astype(BF16), cim_ref[kt], preferred_element_type=F32))
    y = jnp.concatenate(ys, axis=-1) + d_ref[...] * u.astype(F32)
    z = jax.nn.gelu(y)
    gate = jnp.dot(z.astype(BF16), wglu_ref[...], preferred_element_type=F32) + bglu_ref[...]
    y_ref[...] = (z * _sigmoid(gate)).astype(y_ref.dtype)


def _s5_mixer(z, h0_re, h0_im, p, bb, n_steps, target_rows):
    rows_total = z.shape[0]
    rows = _pick_rows(n_steps, bb, target_rows)
    tc = rows // bb
    const2 = lambda i: (0, 0)
    const3 = lambda i: (0, 0, 0)
    kern = functools.partial(_s5_kernel, bb=bb, tc=tc)
    return pl.pallas_call(
        kern,
        grid=(rows_total // rows,),
        in_specs=[pl.BlockSpec((rows, S5_WIDTH), lambda i: (i, Z_UA // S5_WIDTH)),
                  pl.BlockSpec((S5_KT, MXU_K, 2 * S5_NT), const3),
                  pl.BlockSpec((SUBLANES, S5_N), const2),
                  pl.BlockSpec((SUBLANES, S5_N), const2),
                  pl.BlockSpec((S5_KT, S5_NT, MXU_K), const3),
                  pl.BlockSpec((S5_KT, S5_NT, MXU_K), const3),
                  pl.BlockSpec((1, S5_WIDTH), const2),
                  pl.BlockSpec((S5_WIDTH, S5_WIDTH), const2),
                  pl.BlockSpec((1, S5_WIDTH), const2),
                  pl.BlockSpec((bb, S5_N), const2),
                  pl.BlockSpec((bb, S5_N), const2)],
        out_specs=[pl.BlockSpec((rows, S5_WIDTH), lambda i: (i, 0)),
                   pl.BlockSpec((bb, S5_N), const2),
                   pl.BlockSpec((bb, S5_N), const2)],
        out_shape=[jax.ShapeDtypeStruct((rows_total, S5_WIDTH), BF16),
                   jax.ShapeDtypeStruct((bb, S5_N), F32),
                   jax.ShapeDtypeStruct((bb, S5_N), F32)],
        scratch_shapes=[pltpu.VMEM((bb, S5_N), F32),
                        pltpu.VMEM((bb, S5_N), F32),
                        pltpu.VMEM((rows, S5_N), F32),
                        pltpu.VMEM((rows, S5_N), F32)],
        compiler_params=_cparams(("arbitrary",)),
        name="s5_mixer",
    )(z, p['s5_wb'], p['s5_are'], p['s5_aim'], p['s5_cre'], p['s5_cim'],
      p['s5_d'], p['s5_w_glu'], p['s5_b_glu'], h0_re, h0_im)


def _lru_kernel(xb_ref, gb_ref, cache_ref, cw_ref, cb_ref, wg_ref, ba_ref, bx_ref, lam_ref, h0_ref,
                y_ref, h_out, cache_out,
                ext_ref, a_ref, b_ref, h_ref, *, bb, tc):
    rows = tc * bb
    halo = (LRU_CONV - 1) * bb

    @pl.when(pl.program_id(0) == 0)
    def _():
        ext_ref[0:halo, :] = cache_ref[...]
        h_ref[...] = h0_ref[...]

    ext_ref[halo:halo + rows, :] = xb_ref[...].astype(F32)
    xc = cb_ref[...] + cw_ref[0:1, :] * ext_ref[0:rows, :]
    for j in range(1, LRU_CONV):
        xc = xc + cw_ref[j:j + 1, :] * ext_ref[j * bb:j * bb + rows, :]
    new_cache = ext_ref[rows:rows + halo, :]
    cache_out[...] = new_cache
    ext_ref[0:halo, :] = new_cache

    xcb = xc.astype(BF16)
    rs, is_ = [], []
    for kt in range(LRU_KT):
        g = jnp.dot(xcb[:, kt * MXU_K:(kt + 1) * MXU_K], wg_ref[kt], preferred_element_type=F32)
        rs.append(g[:, 0:MXU_K])
        is_.append(g[:, MXU_K:2 * MXU_K])
    r = _sigmoid(jnp.concatenate(rs, axis=-1) + ba_ref[...])
    ig = _sigmoid(jnp.concatenate(is_, axis=-1) + bx_ref[...])
    nl = -lam_ref[...]
    softplus = jnp.maximum(nl, 0.0) + jnp.log1p(jnp.exp(-jnp.abs(nl)))
    log_a = (-LRU_C) * r * softplus
    a = jnp.exp(log_a)
    mult = jnp.sqrt(-jnp.tanh(log_a) * (a * a + 1.0))
    a_ref[...] = a
    b_ref[...] = mult * (ig * xc)

    def scan_tile(rt, carry):
        r0 = rt * SUBLANES
        h0 = h_ref[pl.ds(r0, SUBLANES), :]

        def step(t, h):
            row = pl.multiple_of(t * bb + r0, SUBLANES)
            h = a_ref[pl.ds(row, SUBLANES), :] * h + b_ref[pl.ds(row, SUBLANES), :]
            b_ref[pl.ds(row, SUBLANES), :] = h
            return h

        h1 = lax.fori_loop(0, tc, step, h0, unroll=2 if tc % 2 == 0 else 1)
        h_ref[pl.ds(r0, SUBLANES), :] = h1
        return carry

    if bb == SUBLANES:
        scan_tile(0, 0)
    else:
        lax.fori_loop(0, bb // SUBLANES, scan_tile, 0)

    h_out[...] = h_ref[...]
    y_ref[...] = (b_ref[...] * jax.nn.gelu(gb_ref[...].astype(F32))).astype(y_ref.dtype)


def _lru_mixer(z, h0, cache, p, bb, n_steps, target_rows):
    rows_total = z.shape[0]
    rows = _pick_rows(n_steps, bb, target_rows)
    tc = rows // bb
    halo = (LRU_CONV - 1) * bb
    const2 = lambda i: (0, 0)
    vec = pl.BlockSpec((1, LRU_WIDTH), const2)
    kern = functools.partial(_lru_kernel, bb=bb, tc=tc)
    return pl.pallas_call(
        kern,
        grid=(rows_total // rows,),
        in_specs=[pl.BlockSpec((rows, LRU_WIDTH), lambda i: (i, Z_XB // LRU_WIDTH)),
                  pl.BlockSpec((rows, LRU_WIDTH), lambda i: (i, Z_GB // LRU_WIDTH)),
                  pl.BlockSpec((halo, LRU_WIDTH), const2),
                  pl.BlockSpec((LRU_CONV, LRU_WIDTH), const2),
                  vec,
                  pl.BlockSpec((LRU_KT, MXU_K, 2 * MXU_K), lambda i: (0, 0, 0)),
                  vec, vec, vec,
                  pl.BlockSpec((bb, LRU_WIDTH), const2)],
        out_specs=[pl.BlockSpec((rows, LRU_WIDTH), lambda i: (i, 0)),
                   pl.BlockSpec((bb, LRU_WIDTH), const2),
                   pl.BlockSpec((halo, LRU_WIDTH), const2)],
        out_shape=[jax.ShapeDtypeStruct((rows_total, LRU_WIDTH), BF16),
                   jax.ShapeDtypeStruct((bb, LRU_WIDTH), F32),
                   jax.ShapeDtypeStruct((halo, LRU_WIDTH), F32)],
        scratch_shapes=[pltpu.VMEM((rows + halo, LRU_WIDTH), F32),
                        pltpu.VMEM((rows, LRU_WIDTH), F32),
                        pltpu.VMEM((rows, LRU_WIDTH), F32),
                        pltpu.VMEM((bb, LRU_WIDTH), F32)],
        compiler_params=_cparams(("arbitrary",)),
        name="rglru_mixer",
    )(z, z, cache, p['lru_conv_w'], p['lru_conv_b'], p['lru_wg'], p['lru_b_a'], p['lru_b_x'],
      p['lru_lam'], h0)


def _ret_sequence(q, k, v, state, cos, sin, decay_ref, kdec, cross, sdec, ng):
    half = RET_DK // 2
    lane = lax.broadcasted_iota(jnp.int32, q.shape, 1)
    first = (lane % RET_DK) < half

    def rope(x):
        swapped = jnp.where(first, pltpu.roll(x, RET_QK - half, 1), pltpu.roll(x, half, 1))
        return x * cos + swapped * sin

    qr = rope(q).astype(BF16)
    kr = rope(k) * (RET_DK ** -0.5)
    kd = (kr * kdec).astype(BF16)
    kr = kr.astype(BF16)
    outs, new_states = [], []
    for h in range(RET_HEADS):
        qh = qr[:, h * RET_DK:(h + 1) * RET_DK]
        kh = kr[:, h * RET_DK:(h + 1) * RET_DK]
        kdh = kd[:, h * RET_DK:(h + 1) * RET_DK]
        vh = v[:, h * RET_DV:(h + 1) * RET_DV]
        s = state(h)
        sc = lax.dot_general(qh, kh, (((1,), (1,)), ((), ())), preferred_element_type=F32)
        sc = (sc * decay_ref[h]).astype(BF16)
        o = jnp.dot(sc, vh, preferred_element_type=F32)
        o = o + jnp.dot(qh, s.astype(BF16), preferred_element_type=F32) * cross[:, h * RET_DV:(h + 1) * RET_DV]
        new_states.append(sdec[h:h + 1, :] * s + lax.dot_general(kdh, vh, (((0,), (0,)), ((), ())),
                                                                 preferred_element_type=F32))
        mu = jnp.mean(o, axis=-1, keepdims=True)
        oc = o - mu
        var = jnp.mean(oc * oc, axis=-1, keepdims=True)
        outs.append(oc * lax.rsqrt(var + EPS) * ng[:, h * RET_DV:(h + 1) * RET_DV])
    return jnp.concatenate(outs, axis=-1), new_states


def _ret_bm_kernel(z_ref, cos_ref, sin_ref, decay_ref, kdec_ref, cross_ref, sdec_ref, ng_ref, s0_ref,
                   y_ref, s_out):
    cos, sin, kdec, cross, sdec, ng = (cos_ref[...], sin_ref[...], kdec_ref[...], cross_ref[...],
                                       sdec_ref[...], ng_ref[...])

    def one(i, carry):
        blk = z_ref[i]
        v = blk[:, 0:RET_V]
        g = blk[:, RET_V:2 * RET_V].astype(F32)
        q = blk[:, 2 * RET_V:2 * RET_V + RET_QK].astype(F32)
        k = blk[:, 2 * RET_V + RET_QK:2 * RET_V + 2 * RET_QK].astype(F32)
        on, new_states = _ret_sequence(q, k, v, lambda h: s0_ref[i, h], cos, sin, decay_ref, kdec, cross,
                                       sdec, ng)
        for h in range(RET_HEADS):
            s_out[i, h] = new_states[h]
        y_ref[i] = (g * _sigmoid(g) * on).astype(y_ref.dtype)
        return carry

    lax.fori_loop(0, z_ref.shape[0], one, 0)


def _ret_mixer_bm(zr, s0, norm_g, tabs, group):
    nb, c, _ = zr.shape
    const2 = lambda i: (0, 0)
    return pl.pallas_call(
        _ret_bm_kernel,
        grid=(nb // group,),
        in_specs=[pl.BlockSpec((group, c, Z_RET_W), lambda i: (i, 0, 0)),
                  pl.BlockSpec((c, RET_QK), const2),
                  pl.BlockSpec((c, RET_QK), const2),
                  pl.BlockSpec((RET_HEADS, c, c), lambda i: (0, 0, 0)),
                  pl.BlockSpec((c, RET_QK), const2),
                  pl.BlockSpec((c, RET_V), const2),
                  pl.BlockSpec((RET_HEADS, RET_DV), const2),
                  pl.BlockSpec((1, RET_V), const2),
                  pl.BlockSpec((group, RET_HEADS, RET_DK, RET_DV), lambda i: (i, 0, 0, 0))],
        out_specs=[pl.BlockSpec((group, c, RET_V), lambda i: (i, 0, 0)),
                   pl.BlockSpec((group, RET_HEADS, RET_DK, RET_DV), lambda i: (i, 0, 0, 0))],
        out_shape=[jax.ShapeDtypeStruct((nb, c, RET_V), BF16),
                   jax.ShapeDtypeStruct((nb, RET_HEADS, RET_DK, RET_DV), F32)],
        compiler_params=_cparams(("arbitrary",)),
        name="retention_step",
    )(zr, tabs['cos'], tabs['sin'], tabs['decay'], tabs['kdec'], tabs['cross'], tabs['sdec'], norm_g, s0)


RET_IN_SLABS = (RET_V + 2 * RET_QK) // 128
RET_OUT_SLABS = RET_V // 128


def _ret_tm_kernel(*refs, steps):
    z_ref, cos_ref, sin_ref, decay_ref, kdec_ref, cross_ref, sdec_ref, ng_ref, s0_ref = refs[:9]
    y_ref, s_out, in_slab, out_slab = refs[-4:]
    nseq = SUBLANES

    @pl.when(pl.program_id(0) == 0)
    def _():
        s_out[...] = s0_ref[...]

    for s in range(RET_IN_SLABS):
        col = s * 128 if s < RET_OUT_SLABS else RET_V + s * 128
        in_slab[s] = z_ref[:, col:col + 128].astype(F32)
    cos, sin, kdec, cross, sdec, ng = (cos_ref[...], sin_ref[...], kdec_ref[...], cross_ref[...],
                                       sdec_ref[...], ng_ref[...])
    nv, nq = RET_V // 128, RET_QK // 128

    def one(b, carry):
        rows = pl.ds(b, steps, stride=nseq)
        gather = lambda lo, n: jnp.concatenate([in_slab[lo + s, rows, :] for s in range(n)], axis=-1)
        v = gather(0, nv).astype(BF16)
        q = gather(nv, nq)
        k = gather(nv + nq, nq)
        on, new_states = _ret_sequence(q, k, v, lambda h: s_out[b, h], cos, sin, decay_ref, kdec, cross,
                                       sdec, ng)
        for h in range(RET_HEADS):
            s_out[b, h] = new_states[h]
        for s in range(RET_OUT_SLABS):
            out_slab[s, rows, :] = on[:, s * 128:(s + 1) * 128]
        return carry

    lax.fori_loop(0, nseq, one, 0)
    g = z_ref[:, RET_V:2 * RET_V].astype(F32)
    on = jnp.concatenate([out_slab[s] for s in range(RET_OUT_SLABS)], axis=-1)
    y_ref[...] = (g * _sigmoid(g) * on).astype(y_ref.dtype)


def _ret_mixer_tm(z, s0, norm_g, tabs, row0, n_chunks, y_prev):
    rows_total = z.shape[0]
    c = tabs['chunk']
    rows = c * SUBLANES
    row_at = lambda i: pl.multiple_of(row0 + rows * i, 128)
    const2 = lambda i: (0, 0)
    in_specs = [pl.BlockSpec((pl.Element(rows), pl.Element(Z_RET_W)), lambda i: (row_at(i), Z_RET)),
                pl.BlockSpec((c, RET_QK), lambda i: (i, 0)),
                pl.BlockSpec((c, RET_QK), lambda i: (i, 0)),
                pl.BlockSpec((RET_HEADS, c, c), lambda i: (0, 0, 0)),
                pl.BlockSpec((c, RET_QK), const2),
                pl.BlockSpec((c, RET_V), const2),
                pl.BlockSpec((RET_HEADS, RET_DV), const2),
                pl.BlockSpec((1, RET_V), const2),
                pl.BlockSpec((SUBLANES, RET_HEADS, RET_DK, RET_DV), lambda i: (0, 0, 0, 0))]
    args = [z, tabs['cos'], tabs['sin'], tabs['decay'], tabs['kdec'], tabs['cross'], tabs['sdec'], norm_g, s0]
    aliases = {}
    if y_prev is not None:
        in_specs.append(pl.BlockSpec(memory_space=pl.ANY))
        args.append(y_prev)
        aliases = {len(args) - 1: 0}
    return pl.pallas_call(
        functools.partial(_ret_tm_kernel, steps=c),
        grid=(n_chunks,),
        in_specs=in_specs,
        out_specs=[pl.BlockSpec((pl.Element(rows), pl.Element(RET_V)), lambda i: (row_at(i), 0)),
                   pl.BlockSpec((SUBLANES, RET_HEADS, RET_DK, RET_DV), lambda i: (0, 0, 0, 0))],
        out_shape=[jax.ShapeDtypeStruct((rows_total, RET_V), BF16),
                   jax.ShapeDtypeStruct((SUBLANES, RET_HEADS, RET_DK, RET_DV), F32)],
        scratch_shapes=[pltpu.VMEM((RET_IN_SLABS, rows, 128), F32),
                        pltpu.VMEM((RET_OUT_SLABS, rows, 128), F32)],
        input_output_aliases=aliases,
        compiler_params=_cparams(("arbitrary",)),
        name="retention_chunks",
    )(*args)


def _ret_tables(pos, chunk, n_real):
    half = RET_DK // 2
    inv = jnp.power(ROPE_BASE, -jnp.arange(half, dtype=F32) / half)
    ang = jnp.asarray(pos, F32)[:, None] * inv
    cos_h = jnp.concatenate([jnp.cos(ang), jnp.cos(ang)], axis=-1)
    sin_h = jnp.concatenate([-jnp.sin(ang), jnp.sin(ang)], axis=-1)
    log_gamma = jnp.log1p(-jnp.exp2(-5.0 - jnp.arange(RET_HEADS, dtype=F32)))
    n = jnp.arange(chunk, dtype=F32)
    diff = n[:, None] - n[None, :]
    decay = jnp.where(diff >= 0, jnp.exp(log_gamma[:, None, None] * jnp.maximum(diff, 0.0)), 0.0)
    kdec = jnp.exp(log_gamma[:, None] * (chunk - 1.0 - n))
    m = jnp.maximum(n - (chunk - n_real), 0.0)
    cross = jnp.exp(log_gamma[:, None] * (m + 1.0))
    return {'chunk': chunk,
            'cos': jnp.tile(cos_h, (1, RET_HEADS)), 'sin': jnp.tile(sin_h, (1, RET_HEADS)),
            'decay': decay,
            'kdec': jnp.repeat(kdec.T, RET_DK, axis=1),
            'cross': jnp.repeat(cross.T, RET_DV, axis=1),
            'sdec': jnp.broadcast_to(jnp.exp(log_gamma * n_real)[:, None], (RET_HEADS, RET_DV))}


def _merge_kernel(x_ref, gt_ref, ya_ref, yb_ref, yc_ref, wa_ref, wb_ref, wc_ref, wo_ref, g_ref, o_ref):
    gate = lambda j: _sigmoid(gt_ref[:, j * D_MODEL:(j + 1) * D_MODEL].astype(F32))
    m = gate(0) * jnp.dot(ya_ref[...], wa_ref[...], preferred_element_type=F32)
    m = m + gate(1) * jnp.dot(yb_ref[...], wb_ref[...], preferred_element_type=F32)
    m = m + gate(2) * jnp.dot(yc_ref[...], wc_ref[...], preferred_element_type=F32)
    mo = jnp.dot(m.astype(BF16), wo_ref[...], preferred_element_type=F32)
    o_ref[...] = x_ref[...] + _rms(mo, g_ref[...])


def _merge(x, z, ya, yb, yc, p, tm):
    rows = x.shape[0]
    row = lambda i: (i, 0)
    const = lambda i: (0, 0)
    return pl.pallas_call(
        _merge_kernel,
        grid=(rows // tm,),
        in_specs=[pl.BlockSpec((tm, D_MODEL), row),
                  pl.BlockSpec((tm, N_BRANCH * D_MODEL), row),
                  pl.BlockSpec((tm, S5_WIDTH), row),
                  pl.BlockSpec((tm, LRU_WIDTH), row),
                  pl.BlockSpec((tm, RET_V), row),
                  pl.BlockSpec((S5_WIDTH, D_MODEL), const),
                  pl.BlockSpec((LRU_WIDTH, D_MODEL), const),
                  pl.BlockSpec((RET_V, D_MODEL), const),
                  pl.BlockSpec((D_MODEL, D_MODEL), const),
                  pl.BlockSpec((1, D_MODEL), const)],
        out_specs=pl.BlockSpec((tm, D_MODEL), row),
        out_shape=jax.ShapeDtypeStruct((rows, D_MODEL), F32),
        compiler_params=_cparams(("arbitrary",)),
        name="branch_merge",
    )(x, z, ya, yb, yc, p['w_branch_a'], p['w_branch_b'], p['w_branch_c'], p['w_out'], p['g_post_mix'])


def _ffn_kernel(x_ref, gpre_ref, wg_ref, wu_ref, cw_ref, cb_ref, wd_ref, gpost_ref, cache_ref,
                o_ref, cache_out, xn_ref, acc_ref, ext_ref, *, bb, n_chunks):
    i = pl.program_id(0)
    c = pl.program_id(1)
    rows = x_ref.shape[0]
    halo = (FFN_CONV - 1) * bb

    @pl.when(c == 0)
    def _():
        xn_ref[...] = _rms(x_ref[...], gpre_ref[...]).astype(BF16)

    @pl.when(i == 0)
    def _():
        ext_ref[c, 0:halo, :] = cache_ref[...]

    xn = xn_ref[...]
    gf = jnp.dot(xn, wg_ref[...], preferred_element_type=F32)
    uf = jnp.dot(xn, wu_ref[...], preferred_element_type=F32)
    ext_ref[c, halo:halo + rows, :] = gf
    gc = cb_ref[...] + cw_ref[0:1, :] * ext_ref[c, 0:rows, :]
    for j in range(1, FFN_CONV):
        gc = gc + cw_ref[j:j + 1, :] * ext_ref[c, j * bb:j * bb + rows, :]
    new_cache = ext_ref[c, rows:rows + halo, :]
    cache_out[c] = new_cache
    ext_ref[c, 0:halo, :] = new_cache
    act = (jax.nn.gelu(gc) * uf).astype(BF16)
    part = jnp.dot(act, wd_ref[...], preferred_element_type=F32)

    @pl.when(c == 0)
    def _():
        acc_ref[...] = part

    @pl.when(c > 0)
    def _():
        acc_ref[...] += part

    @pl.when(c == n_chunks - 1)
    def _():
        o_ref[...] = x_ref[...] + _rms(acc_ref[...], gpost_ref[...])


def _ffn(x, cache, p, bb, tm, tf):
    rows = x.shape[0]
    halo = (FFN_CONV - 1) * bb
    n_chunks = D_FF // tf
    kern = functools.partial(_ffn_kernel, bb=bb, n_chunks=n_chunks)
    return pl.pallas_call(
        kern,
        grid=(rows // tm, n_chunks),
        in_specs=[pl.BlockSpec((tm, D_MODEL), lambda i, c: (i, 0)),
                  pl.BlockSpec((1, D_MODEL), lambda i, c: (0, 0)),
                  pl.BlockSpec((D_MODEL, tf), lambda i, c: (0, c)),
                  pl.BlockSpec((D_MODEL, tf), lambda i, c: (0, c + n_chunks)),
                  pl.BlockSpec((FFN_CONV, tf), lambda i, c: (0, c)),
                  pl.BlockSpec((1, tf), lambda i, c: (0, c)),
                  pl.BlockSpec((tf, D_MODEL), lambda i, c: (c, 0)),
                  pl.BlockSpec((1, D_MODEL), lambda i, c: (0, 0)),
                  pl.BlockSpec((halo, tf), lambda i, c: (0, c))],
        out_specs=[pl.BlockSpec((tm, D_MODEL), lambda i, c: (i, 0)),
                   pl.BlockSpec((n_chunks, halo, tf), lambda i, c: (0, 0, 0))],
        out_shape=[jax.ShapeDtypeStruct((rows, D_MODEL), F32),
                   jax.ShapeDtypeStruct((n_chunks, halo, tf), F32)],
        scratch_shapes=[pltpu.VMEM((tm, D_MODEL), BF16),
                        pltpu.VMEM((tm, D_MODEL), F32),
                        pltpu.VMEM((n_chunks, tm + halo, tf), F32)],
        compiler_params=_cparams(("arbitrary", "arbitrary")),
        name="conv_ffn",
    )(x, p['g_pre_ffn'], p['ffn_w_up'], p['ffn_w_up'], p['ffn_conv_w'], p['ffn_conv_b'],
      p['ffn_w_down'], p['g_post_ffn'], cache)


def _block_diag(w, tiles):
    nb, r, c = w.shape
    per = nb // tiles
    w4 = w.reshape(tiles, per, r, c)
    eye = jnp.eye(per, dtype=w.dtype)
    return jnp.einsum('kgrc,gh->kgrhc', w4, eye).reshape(tiles, per * r, per * c)


def _prep_layer(P, l):
    row = lambda a: a[l].reshape(1, -1).astype(F32)
    w_in = P['w_in'][l]
    o = np.cumsum([0, S5_WIDTH, LRU_WIDTH, LRU_WIDTH, RET_QK, RET_QK, RET_V, RET_V, N_BRANCH * D_MODEL])
    ua, xb, gb, q, k, v, gc, gates = [w_in[:, o[i]:o[i + 1]] for i in range(8)]
    p = {
        'g_pre_mix': row(P['g_pre_mix']), 'g_post_mix': row(P['g_post_mix']),
        'g_pre_ffn': row(P['g_pre_ffn']), 'g_post_ffn': row(P['g_post_ffn']),
        'w_in': jnp.concatenate([gates, v, gc, q, k, ua, xb, gb], axis=1).astype(BF16),
        's5_lre': row(P['s5_a_re']), 's5_lim': row(P['s5_a_im']),
        's5_lstep': jnp.repeat(P['s5_log_step'][l], S5_STATE).reshape(1, S5_N),
        's5_bre': _block_diag(jnp.swapaxes(P['s5_b_re'][l], 1, 2), S5_KT),
        's5_bim': _block_diag(jnp.swapaxes(P['s5_b_im'][l], 1, 2), S5_KT),
        's5_cre': _block_diag(jnp.swapaxes(P['s5_c_re'][l], 1, 2), S5_KT).astype(BF16),
        's5_cim': _block_diag(jnp.swapaxes(P['s5_c_im'][l], 1, 2), S5_KT).astype(BF16),
        's5_d': row(P['s5_d']), 's5_w_glu': P['s5_w_glu'][l].astype(BF16), 's5_b_glu': row(P['s5_b_glu']),
        'lru_conv_w': P['lru_conv_w'][l], 'lru_conv_b': row(P['lru_conv_b']),
        'lru_wg': jnp.concatenate([_block_diag(P['lru_w_a'][l], LRU_KT),
                                   _block_diag(P['lru_w_x'][l], LRU_KT)], axis=-1).astype(BF16),
        'lru_b_a': row(P['lru_b_a']), 'lru_b_x': row(P['lru_b_x']), 'lru_lam': row(P['lru_lam']),
        'ret_norm_g': row(P['ret_norm_g']),
        'w_branch_a': P['w_branch_a'][l].astype(BF16), 'w_branch_b': P['w_branch_b'][l].astype(BF16),
        'w_branch_c': P['w_branch_c'][l].astype(BF16), 'w_out': P['w_out'][l].astype(BF16),
        'ffn_w_up': P['ffn_w_up'][l].astype(BF16), 'ffn_conv_w': P['ffn_conv_w'][l],
        'ffn_conv_b': row(P['ffn_conv_b']), 'ffn_w_down': P['ffn_w_down'][l].astype(BF16),
    }
    p['s5_wb'], p['s5_are'], p['s5_aim'] = _s5_discretize(p)
    return p


def _run_group(x, states, params, bb, n_steps, retention, tiles):
    s5_re, s5_im, lru_h, lru_cache, ret_s, ffn_cache = states
    new = [[] for _ in range(6)]
    for l in range(DEPTH):
        p = params[l]
        z = _norm_matmul(x, p['g_pre_mix'], p['w_in'], tiles['tm_in'], tiles['tn_in'])
        ya, hre, him = _s5_mixer(z, s5_re[l], s5_im[l], p, bb, n_steps, tiles['rows_scan'])
        yb, hl, lcache = _lru_mixer(z, lru_h[l], lru_cache[l], p, bb, n_steps, tiles['rows_scan'])
        yc, rs = retention(z, ret_s[l], p['ret_norm_g'])
        x = _merge(x, z, ya, yb, yc, p, tiles['tm'])
        x, fcache = _ffn(x, ffn_cache[l], p, bb, tiles['tm'], tiles['tf'])
        fcache = jnp.swapaxes(fcache, 0, 1).reshape(-1, D_FF)
        for lst, val in zip(new, (hre, him, hl, lcache, rs, fcache)):
            lst.append(val)
    return x, new


def kernel(x_prompt, x_sample, state_s5_re, state_s5_im, state_lru, cache_lru_conv, state_ret, cache_ffn_conv, meta_tokens, g_pre_mix, g_post_mix, g_pre_ffn, g_post_ffn, w_in, s5_a_re, s5_a_im, s5_log_step, s5_b_re, s5_b_im, s5_c_re, s5_c_im, s5_d, s5_w_glu, s5_b_glu, lru_conv_w, lru_conv_b, lru_w_a, lru_b_a, lru_w_x, lru_b_x, lru_lam, ret_norm_g, w_branch_a, w_branch_b, w_branch_c, w_out, ffn_w_up, ffn_conv_w, ffn_conv_b, ffn_w_down):
    P = {'g_pre_mix': g_pre_mix, 'g_post_mix': g_post_mix, 'g_pre_ffn': g_pre_ffn, 'g_post_ffn': g_post_ffn,
         'w_in': w_in, 's5_a_re': s5_a_re, 's5_a_im': s5_a_im, 's5_log_step': s5_log_step,
         's5_b_re': s5_b_re, 's5_b_im': s5_b_im, 's5_c_re': s5_c_re, 's5_c_im': s5_c_im,
         's5_d': s5_d, 's5_w_glu': s5_w_glu, 's5_b_glu': s5_b_glu,
         'lru_conv_w': lru_conv_w, 'lru_conv_b': lru_conv_b, 'lru_w_a': lru_w_a, 'lru_b_a': lru_b_a,
         'lru_w_x': lru_w_x, 'lru_b_x': lru_b_x, 'lru_lam': lru_lam, 'ret_norm_g': ret_norm_g,
         'w_branch_a': w_branch_a, 'w_branch_b': w_branch_b, 'w_branch_c': w_branch_c, 'w_out': w_out,
         'ffn_w_up': ffn_w_up, 'ffn_conv_w': ffn_conv_w, 'ffn_conv_b': ffn_conv_b, 'ffn_w_down': ffn_w_down}
    params = [_prep_layer(P, l) for l in range(DEPTH)]

    bp, sp, _ = x_prompt.shape
    lp = N_META + sp
    meta = jnp.broadcast_to(meta_tokens.astype(F32)[:, None, :], (N_META, bp, D_MODEL))
    xp = jnp.concatenate([meta, jnp.swapaxes(x_prompt, 0, 1)], axis=0).reshape(lp * bp, D_MODEL)
    zeros = lambda *s: [jnp.zeros(s, F32)] * DEPTH
    st_p = (zeros(bp, S5_N), zeros(bp, S5_N), zeros(bp, LRU_WIDTH),
            zeros((LRU_CONV - 1) * bp, LRU_WIDTH), zeros(bp, RET_HEADS, RET_DK, RET_DV),
            zeros((FFN_CONV - 1) * bp, D_FF))
    assert bp == SUBLANES and sp % RET_CHUNK == 0 and (N_META * bp) % 128 == 0
    tabs_meta = _ret_tables(np.arange(N_META), N_META, N_META)
    tabs_main = _ret_tables(N_META + np.arange(sp), RET_CHUNK, RET_CHUNK)

    def retention_p(z, s0, ng):
        y, s1 = _ret_mixer_tm(z, s0, ng, tabs_meta, 0, 1, None)
        return _ret_mixer_tm(z, s1, ng, tabs_main, N_META * bp, sp // RET_CHUNK, y)

    tiles_p = {'tm_in': _pick_rows(lp, bp, 1400), 'tn_in': 1408, 'rows_scan': 700,
               'tm': _pick_rows(lp, bp, 700), 'tf': 1408}
    yp, new_p = _run_group(xp, st_p, params, bp, lp, retention_p, tiles_p)
    y_prompt = jnp.swapaxes(yp.reshape(lp, bp, D_MODEL)[N_META:], 0, 1)

    bs, ds, _ = x_sample.shape
    xs = jnp.swapaxes(x_sample, 0, 1).reshape(ds * bs, D_MODEL)
    tm_major = lambda c: jnp.swapaxes(c, 0, 1).reshape(-1, c.shape[-1])
    st_s = ([state_s5_re[l].reshape(bs, S5_N) for l in range(DEPTH)],
            [state_s5_im[l].reshape(bs, S5_N) for l in range(DEPTH)],
            [state_lru[l] for l in range(DEPTH)],
            [tm_major(cache_lru_conv[l]) for l in range(DEPTH)],
            [state_ret[l] for l in range(DEPTH)],
            [tm_major(cache_ffn_conv[l]) for l in range(DEPTH)])
    chunk_s = 16
    pad_s = chunk_s - ds
    tabs_s = _ret_tables(np.concatenate([np.zeros(pad_s), PAST_LEN + np.arange(ds)]), chunk_s, ds)

    def retention_s(z, s0, ng):
        zr = z[:, Z_RET:Z_RET + Z_RET_W].reshape(ds, bs, Z_RET_W).transpose(1, 0, 2)
        zr = jnp.pad(zr, ((0, 0), (pad_s, 0), (0, 0)))
        yc, s1 = _ret_mixer_bm(zr, s0, ng, tabs_s, SUBLANES)
        return yc[:, pad_s:].transpose(1, 0, 2).reshape(ds * bs, RET_V), s1

    tiles_s = {'tm_in': ds * bs, 'tn_in': 1408, 'rows_scan': ds * bs, 'tm': ds * bs, 'tf': 1408}
    ys, new_s = _run_group(xs, st_s, params, bs, ds, retention_s, tiles_s)
    y_sample = jnp.swapaxes(ys.reshape(ds, bs, D_MODEL), 0, 1)

    def pack(new, b):
        s5r, s5i, lh, lc, rs, fc = [jnp.stack(v) for v in new]
        return (s5r.reshape(DEPTH, b, S5_GROUPS, S5_STATE), s5i.reshape(DEPTH, b, S5_GROUPS, S5_STATE), lh,
                jnp.swapaxes(lc.reshape(DEPTH, LRU_CONV - 1, b, LRU_WIDTH), 1, 2), rs,
                jnp.swapaxes(fc.reshape(DEPTH, FFN_CONV - 1, b, D_FF), 1, 2))

    return (y_prompt, y_sample) + pack(new_p, bp) + pack(new_s, bs)
```

```python
import functools
import math

import jax
import jax.numpy as jnp
import numpy as np
from jax import lax
from jax.experimental import pallas as pl
from jax.experimental.pallas import tpu as pltpu

F32 = jnp.float32
BF16 = jnp.bfloat16

D_MODEL = 1024
DEPTH = 2
PAST_LEN = 16384
N_META = 16
S5_WIDTH = 768
S5_GROUP = 16
S5_GROUPS = S5_WIDTH // S5_GROUP
S5_STATE = 64
S5_N = S5_GROUPS * S5_STATE
LRU_WIDTH = 768
LRU_BLOCK = 64
LRU_BLOCKS = LRU_WIDTH // LRU_BLOCK
LRU_CONV = 4
LRU_C = 8.0
RET_HEADS = 8
RET_DK = 64
RET_DV = 128
RET_QK = RET_HEADS * RET_DK
RET_V = RET_HEADS * RET_DV
RET_CHUNK = 128
ROPE_BASE = 10000.0
N_BRANCH = 3
D_FF = 2816
FFN_CONV = 3
EPS = 1e-6
D_IN = S5_WIDTH + 2 * LRU_WIDTH + 2 * RET_QK + 2 * RET_V + N_BRANCH * D_MODEL

Z_UA = 0
Z_XB = Z_UA + S5_WIDTH
Z_GB = Z_XB + LRU_WIDTH
Z_RET = Z_GB + LRU_WIDTH
Z_RET_W = 2 * RET_QK + 2 * RET_V
Z_GATES = Z_RET + Z_RET_W
RQ, RK, RV, RG = 0, RET_QK, 2 * RET_QK, 2 * RET_QK + RET_V

MXU_K = 256
S5_KT = S5_WIDTH // MXU_K
S5_NT = S5_N // S5_KT
LRU_KT = LRU_WIDTH // MXU_K

SUBLANES = 8
VMEM_LIMIT = 56 * 1024 * 1024


def _cparams(sem):
    return pltpu.CompilerParams(dimension_semantics=sem, vmem_limit_bytes=VMEM_LIMIT)


def _pick_rows(n_steps, bb, target_rows):
    best = None
    for tt in range(1, n_steps + 1):
        if n_steps % tt:
            continue
        rows = tt * bb
        if rows % 16 == 0 and rows <= target_rows:
            best = rows
    if best is None:
        best = n_steps * bb
    return best


def _rms(x, g):
    return x * lax.rsqrt(jnp.mean(x * x, axis=-1, keepdims=True) + EPS) * g


def _sigmoid(x):
    return 0.5 * jnp.tanh(0.5 * x) + 0.5


def _norm_matmul_kernel(x_ref, g_ref, w_ref, o_ref, xn_ref):
    @pl.when(pl.program_id(1) == 0)
    def _():
        xn_ref[...] = _rms(x_ref[...], g_ref[...]).astype(BF16)

    o_ref[...] = jnp.dot(xn_ref[...], w_ref[...],
                         preferred_element_type=F32).astype(o_ref.dtype)


def _norm_matmul(x, g, w, tm, tn):
    rows, d = x.shape
    n = w.shape[1]
    return pl.pallas_call(
        _norm_matmul_kernel,
        grid=(rows // tm, n // tn),
        in_specs=[pl.BlockSpec((tm, d), lambda i, j: (i, 0)),
                  pl.BlockSpec((1, d), lambda i, j: (0, 0)),
                  pl.BlockSpec((d, tn), lambda i, j: (0, j))],
        out_specs=pl.BlockSpec((tm, tn), lambda i, j: (i, j)),
        out_shape=jax.ShapeDtypeStruct((rows, n), BF16),
        scratch_shapes=[pltpu.VMEM((tm, d), BF16)],
        compiler_params=_cparams(("arbitrary", "arbitrary")),
        name="norm_in_proj",
    )(x, g, w)


def _s5_disc_kernel(lre_ref, lim_ref, lstep_ref, bre_ref, bim_ref, wb_ref, are_ref, aim_ref):
    dt = jnp.exp(lstep_ref[...])
    lr = lre_ref[...]
    li = lim_ref[...]
    mag = jnp.exp(lr * dt)
    a_re = mag * jnp.cos(li * dt)
    a_im = mag * jnp.sin(li * dt)
    den = lr * lr + li * li
    nr = a_re - 1.0
    cr = (nr * lr + a_im * li) / den
    ci = (a_im * lr - nr * li) / den
    are_ref[...] = jnp.broadcast_to(a_re, are_ref.shape)
    aim_ref[...] = jnp.broadcast_to(a_im, aim_ref.shape)
    b_r = bre_ref[0]
    b_i = bim_ref[0]
    wb_ref[0, :, 0:S5_NT] = (cr * b_r - ci * b_i).astype(BF16)
    wb_ref[0, :, S5_NT:2 * S5_NT] = (cr * b_i + ci * b_r).astype(BF16)


def _s5_discretize(p):
    vec = pl.BlockSpec((1, S5_NT), lambda k: (0, k))
    blk = pl.BlockSpec((1, MXU_K, S5_NT), lambda k: (k, 0, 0))
    return pl.pallas_call(
        _s5_disc_kernel,
        grid=(S5_KT,),
        in_specs=[vec, vec, vec, blk, blk],
        out_specs=[pl.BlockSpec((1, MXU_K, 2 * S5_NT), lambda k: (k, 0, 0)),
                   pl.BlockSpec((SUBLANES, S5_NT), lambda k: (0, k)),
                   pl.BlockSpec((SUBLANES, S5_NT), lambda k: (0, k))],
        out_shape=[jax.ShapeDtypeStruct((S5_KT, MXU_K, 2 * S5_NT), BF16),
                   jax.ShapeDtypeStruct((SUBLANES, S5_N), F32),
                   jax.ShapeDtypeStruct((SUBLANES, S5_N), F32)],
        compiler_params=_cparams(("arbitrary",)),
        name="s5_discretize",
    )(p['s5_lre'], p['s5_lim'], p['s5_lstep'], p['s5_bre'], p['s5_bim'])


def _s5_kernel(u_ref, wb_ref, are_ref, aim_ref, cre_ref, cim_ref,
               d_ref, wglu_ref, bglu_ref, h0re_ref, h0im_ref,
               y_ref, hre_out, him_out,
               hre_ref, him_ref, sre_ref, sim_ref, *, bb, tc):
    @pl.when(pl.program_id(0) == 0)
    def _():
        hre_ref[...] = h0re_ref[...]
        him_ref[...] = h0im_ref[...]

    u = u_ref[...]
    for kt in range(S5_KT):
        r = jnp.dot(u[:, kt * MXU_K:(kt + 1) * MXU_K], wb_ref[kt], preferred_element_type=F32)
        sre_ref[:, kt * S5_NT:(kt + 1) * S5_NT] = r[:, 0:S5_NT]
        sim_ref[:, kt * S5_NT:(kt + 1) * S5_NT] = r[:, S5_NT:2 * S5_NT]

    def scan_tile(rt, carry):
        r0 = rt * SUBLANES
        for kt in range(S5_KT):
            sl = slice(kt * S5_NT, (kt + 1) * S5_NT)
            ar = are_ref[:, sl]
            ai = aim_ref[:, sl]
            hr0 = hre_ref[pl.ds(r0, SUBLANES), sl]
            hi0 = him_ref[pl.ds(r0, SUBLANES), sl]

            def step(t, h):
                hr, hi = h
                row = pl.multiple_of(t * bb + r0, SUBLANES)
                nr_ = ar * hr - ai * hi + sre_ref[pl.ds(row, SUBLANES), sl]
                ni_ = ar * hi + ai * hr + sim_ref[pl.ds(row, SUBLANES), sl]
                sre_ref[pl.ds(row, SUBLANES), sl] = nr_
                sim_ref[pl.ds(row, SUBLANES), sl] = ni_
                return nr_, ni_

            hr1, hi1 = lax.fori_loop(0, tc, step, (hr0, hi0), unroll=2 if tc % 2 == 0 else 1)
            hre_ref[pl.ds(r0, SUBLANES), sl] = hr1
            him_ref[pl.ds(r0, SUBLANES), sl] = hi1
        return carry

    if bb == SUBLANES:
        scan_tile(0, 0)
    else:
        lax.fori_loop(0, bb // SUBLANES, scan_tile, 0)

    hre_out[...] = hre_ref[...]
    him_out[...] = him_ref[...]

    ys = []
    for kt in range(S5_KT):
        sl = slice(kt * S5_NT, (kt + 1) * S5_NT)
        ys.append(jnp.dot(sre_ref[:, sl].astype(BF16), cre_ref[kt], preferred_element_type=F32)
                  - jnp.dot(sim_ref[:, sl].astype(BF16), cim_ref[kt], preferred_element_type=F32))
    y = jnp.concatenate(ys, axis=-1) + d_ref[...] * u.astype(F32)
    z = jax.nn.gelu(y)
    gate = jnp.dot(z.astype(BF16), wglu_ref[...], preferred_element_type=F32) + bglu_ref[...]
    y_ref[...] = (z * _sigmoid(gate)).astype(y_ref.dtype)


def _s5_mixer(z, h0_re, h0_im, p, bb, n_steps, target_rows):
    rows_total = z.shape[0]
    rows = _pick_rows(n_steps, bb, target_rows)
    tc = rows // bb
    const2 = lambda i: (0, 0)
    const3 = lambda i: (0, 0, 0)
    kern = functools.partial(_s5_kernel, bb=bb, tc=tc)
    return pl.pallas_call(
        kern,
        grid=(rows_total // rows,),
        in_specs=[pl.BlockSpec((rows, S5_WIDTH), lambda i: (i, Z_UA // S5_WIDTH)),
                  pl.BlockSpec((S5_KT, MXU_K, 2 * S5_NT), const3),
                  pl.BlockSpec((SUBLANES, S5_N), const2),
                  pl.BlockSpec((SUBLANES, S5_N), const2),
                  pl.BlockSpec((S5_KT, S5_NT, MXU_K), const3),
                  pl.BlockSpec((S5_KT, S5_NT, MXU_K), const3),
                  pl.BlockSpec((1, S5_WIDTH), const2),
                  pl.BlockSpec((S5_WIDTH, S5_WIDTH), const2),
                  pl.BlockSpec((1, S5_WIDTH), const2),
                  pl.BlockSpec((bb, S5_N), const2),
                  pl.BlockSpec((bb, S5_N), const2)],
        out_specs=[pl.BlockSpec((rows, S5_WIDTH), lambda i: (i, 0)),
                   pl.BlockSpec((bb, S5_N), const2),
                   pl.BlockSpec((bb, S5_N), const2)],
        out_shape=[jax.ShapeDtypeStruct((rows_total, S5_WIDTH), BF16),
                   jax.ShapeDtypeStruct((bb, S5_N), F32),
                   jax.ShapeDtypeStruct((bb, S5_N), F32)],
        scratch_shapes=[pltpu.VMEM((bb, S5_N), F32),
                        pltpu.VMEM((bb, S5_N), F32),
                        pltpu.VMEM((rows, S5_N), F32),
                        pltpu.VMEM((rows, S5_N), F32)],
        compiler_params=_cparams(("arbitrary",)),
        name="s5_mixer",
    )(z, p['s5_wb'], p['s5_are'], p['s5_aim'], p['s5_cre'], p['s5_cim'],
      p['s5_d'], p['s5_w_glu'], p['s5_b_glu'], h0_re, h0_im)


def _lru_kernel(xb_ref, gb_ref, cache_ref, cw_ref, cb_ref, wg_ref, ba_ref, bx_ref, lam_ref, h0_ref,
                y_ref, h_out, cache_out,
                ext_ref, a_ref, b_ref, h_ref, *, bb, tc):
    rows = tc * bb
    halo = (LRU_CONV - 1) * bb

    @pl.when(pl.program_id(0) == 0)
    def _():
        ext_ref[0:halo, :] = cache_ref[...]
        h_ref[...] = h0_ref[...]

    ext_ref[halo:halo + rows, :] = xb_ref[...].astype(F32)
    xc = cb_ref[...] + cw_ref[0:1, :] * ext_ref[0:rows, :]
    for j in range(1, LRU_CONV):
        xc = xc + cw_ref[j:j + 1, :] * ext_ref[j * bb:j * bb + rows, :]
    new_cache = ext_ref[rows:rows + halo, :]
    cache_out[...] = new_cache
    ext_ref[0:halo, :] = new_cache

    xcb = xc.astype(BF16)
    rs, is_ = [], []
    for kt in range(LRU_KT):
        g = jnp.dot(xcb[:, kt * MXU_K:(kt + 1) * MXU_K], wg_ref[kt], preferred_element_type=F32)
        rs.append(g[:, 0:MXU_K])
        is_.append(g[:, MXU_K:2 * MXU_K])
    r = _sigmoid(jnp.concatenate(rs, axis=-1) + ba_ref[...])
    ig = _sigmoid(jnp.concatenate(is_, axis=-1) + bx_ref[...])
    nl = -lam_ref[...]
    softplus = jnp.maximum(nl, 0.0) + jnp.log1p(jnp.exp(-jnp.abs(nl)))
    log_a = (-LRU_C) * r * softplus
    a = jnp.exp(log_a)
    mult = jnp.sqrt(-jnp.tanh(log_a) * (a * a + 1.0))
    a_ref[...] = a
    b_ref[...] = mult * (ig * xc)

    def scan_tile(rt, carry):
        r0 = rt * SUBLANES
        h0 = h_ref[pl.ds(r0, SUBLANES), :]

        def step(t, h):
            row = pl.multiple_of(t * bb + r0, SUBLANES)
            h = a_ref[pl.ds(row, SUBLANES), :] * h + b_ref[pl.ds(row, SUBLANES), :]
            b_ref[pl.ds(row, SUBLANES), :] = h
            return h

        h1 = lax.fori_loop(0, tc, step, h0, unroll=2 if tc % 2 == 0 else 1)
        h_ref[pl.ds(r0, SUBLANES), :] = h1
        return carry

    if bb == SUBLANES:
        scan_tile(0, 0)
    else:
        lax.fori_loop(0, bb // SUBLANES, scan_tile, 0)

    h_out[...] = h_ref[...]
    y_ref[...] = (b_ref[...] * jax.nn.gelu(gb_ref[...].astype(F32))).astype(y_ref.dtype)


def _lru_mixer(z, h0, cache, p, bb, n_steps, target_rows):
    rows_total = z.shape[0]
    rows = _pick_rows(n_steps, bb, target_rows)
    tc = rows // bb
    halo = (LRU_CONV - 1) * bb
    const2 = lambda i: (0, 0)
    vec = pl.BlockSpec((1, LRU_WIDTH), const2)
    kern = functools.partial(_lru_kernel, bb=bb, tc=tc)
    return pl.pallas_call(
        kern,
        grid=(rows_total // rows,),
        in_specs=[pl.BlockSpec((rows, LRU_WIDTH), lambda i: (i, Z_XB // LRU_WIDTH)),
                  pl.BlockSpec((rows, LRU_WIDTH), lambda i: (i, Z_GB // LRU_WIDTH)),
                  pl.BlockSpec((halo, LRU_WIDTH), const2),
                  pl.BlockSpec((LRU_CONV, LRU_WIDTH), const2),
                  vec,
                  pl.BlockSpec((LRU_KT, MXU_K, 2 * MXU_K), lambda i: (0, 0, 0)),
                  vec, vec, vec,
                  pl.BlockSpec((bb, LRU_WIDTH), const2)],
        out_specs=[pl.BlockSpec((rows, LRU_WIDTH), lambda i: (i, 0)),
                   pl.BlockSpec((bb, LRU_WIDTH), const2),
                   pl.BlockSpec((halo, LRU_WIDTH), const2)],
        out_shape=[jax.ShapeDtypeStruct((rows_total, LRU_WIDTH), BF16),
                   jax.ShapeDtypeStruct((bb, LRU_WIDTH), F32),
                   jax.ShapeDtypeStruct((halo, LRU_WIDTH), F32)],
        scratch_shapes=[pltpu.VMEM((rows + halo, LRU_WIDTH), F32),
                        pltpu.VMEM((rows, LRU_WIDTH), F32),
                        pltpu.VMEM((rows, LRU_WIDTH), F32),
                        pltpu.VMEM((bb, LRU_WIDTH), F32)],
        compiler_params=_cparams(("arbitrary",)),
        name="rglru_mixer",
    )(z, z, cache, p['lru_conv_w'], p['lru_conv_b'], p['lru_wg'], p['lru_b_a'], p['lru_b_x'],
      p['lru_lam'], h0)


def _ret_sequence(q, k, v, state, cos, sin, decay_ref, kdec, cross, sdec, ng):
    half = RET_DK // 2
    lane = lax.broadcasted_iota(jnp.int32, q.shape, 1)
    first = (lane % RET_DK) < half
    mean = lambda x: jnp.mean(x, axis=-1, keepdims=True)

    def rope(x):
        swapped = jnp.where(first, pltpu.roll(x, RET_QK - half, 1), pltpu.roll(x, half, 1))
        return x * cos + swapped * sin

    qr = rope(q).astype(BF16)
    kr = rope(k) * (RET_DK ** -0.5)
    kd = (kr * kdec).astype(BF16)
    kr = kr.astype(BF16)
    outs, new_states = [], []
    for h in range(RET_HEADS):
        qh = qr[:, h * RET_DK:(h + 1) * RET_DK]
        kh = kr[:, h * RET_DK:(h + 1) * RET_DK]
        kdh = kd[:, h * RET_DK:(h + 1) * RET_DK]
        vh = v[:, h * RET_DV:(h + 1) * RET_DV]
        s = state(h)
        sc = lax.dot_general(qh, kh, (((1,), (1,)), ((), ())), preferred_element_type=F32)
        sc = (sc * decay_ref[h]).astype(BF16)
        o = jnp.dot(sc, vh, preferred_element_type=F32)
        o = o + jnp.dot(qh, s.astype(BF16), preferred_element_type=F32) * cross[:, h * RET_DV:(h + 1) * RET_DV]
        new_states.append(sdec[h:h + 1, :] * s + lax.dot_general(kdh, vh, (((0,), (0,)), ((), ())),
                                                                 preferred_element_type=F32))
        oc = o - mean(o)
        outs.append(oc * lax.rsqrt(mean(oc * oc) + EPS) * ng[:, h * RET_DV:(h + 1) * RET_DV])
    return jnp.concatenate(outs, axis=-1), new_states


def _ret_bm_kernel(z_ref, cos_ref, sin_ref, decay_ref, kdec_ref, cross_ref, sdec_ref, ng_ref, s0_ref,
                   y_ref, s_out):
    cos, sin, kdec, cross, sdec, ng = (cos_ref[...], sin_ref[...], kdec_ref[...], cross_ref[...],
                                       sdec_ref[...], ng_ref[...])

    def one(i, carry):
        blk = z_ref[i]
        v = blk[:, RV:RV + RET_V]
        g = blk[:, RG:RG + RET_V].astype(F32)
        q = blk[:, RQ:RQ + RET_QK].astype(F32)
        k = blk[:, RK:RK + RET_QK].astype(F32)
        on, new_states = _ret_sequence(q, k, v, lambda h: s0_ref[i, h], cos, sin, decay_ref, kdec, cross,
                                       sdec, ng)
        for h in range(RET_HEADS):
            s_out[i, h] = new_states[h]
        y_ref[i] = (g * _sigmoid(g) * on).astype(y_ref.dtype)
        return carry

    lax.fori_loop(0, z_ref.shape[0], one, 0, unroll=True)


def _ret_mixer_bm(zr, s0, norm_g, tabs, group):
    nb, c, _ = zr.shape
    const2 = lambda i: (0, 0)
    return pl.pallas_call(
        _ret_bm_kernel,
        grid=(nb // group,),
        in_specs=[pl.BlockSpec((group, c, Z_RET_W), lambda i: (i, 0, 0)),
                  pl.BlockSpec((c, RET_QK), const2),
                  pl.BlockSpec((c, RET_QK), const2),
                  pl.BlockSpec((RET_HEADS, c, c), lambda i: (0, 0, 0)),
                  pl.BlockSpec((c, RET_QK), const2),
                  pl.BlockSpec((c, RET_V), const2),
                  pl.BlockSpec((RET_HEADS, RET_DV), const2),
                  pl.BlockSpec((1, RET_V), const2),
                  pl.BlockSpec((group, RET_HEADS, RET_DK, RET_DV), lambda i: (i, 0, 0, 0))],
        out_specs=[pl.BlockSpec((group, c, RET_V), lambda i: (i, 0, 0)),
                   pl.BlockSpec((group, RET_HEADS, RET_DK, RET_DV), lambda i: (i, 0, 0, 0))],
        out_shape=[jax.ShapeDtypeStruct((nb, c, RET_V), BF16),
                   jax.ShapeDtypeStruct((nb, RET_HEADS, RET_DK, RET_DV), F32)],
        compiler_params=_cparams(("arbitrary",)),
        name="retention_step",
    )(zr, tabs['cos'], tabs['sin'], tabs['decay'], tabs['kdec'], tabs['cross'], tabs['sdec'], norm_g, s0)


RET_V_SLABS = RET_V // 128
RET_QK_SLABS = RET_QK // 128
RET_IN_SLABS = RET_V_SLABS + 2 * RET_QK_SLABS


def _ret_tm_kernel(*refs, steps):
    z_ref, cos_ref, sin_ref, decay_ref, kdec_ref, cross_ref, sdec_ref, ng_ref, s0_ref = refs[:9]
    y_ref, s_out, in_slab, out_slab = refs[-4:]
    nseq = SUBLANES

    @pl.when(pl.program_id(0) == 0)
    def _():
        s_out[...] = s0_ref[...]

    def fill(lo, n, x):
        for s in range(n):
            in_slab[lo + s] = x[:, s * 128:(s + 1) * 128].astype(F32)

    nv, nq = RET_V_SLABS, RET_QK_SLABS
    fill(0, nv, z_ref[:, RV:RV + RET_V])
    fill(nv, nq, z_ref[:, RQ:RQ + RET_QK])
    fill(nv + nq, nq, z_ref[:, RK:RK + RET_QK])
    cos, sin, kdec, cross, sdec, ng = (cos_ref[...], sin_ref[...], kdec_ref[...], cross_ref[...],
                                       sdec_ref[...], ng_ref[...])

    def one(b, carry):
        rows = pl.ds(b, steps, stride=nseq)
        gather = lambda lo, n: jnp.concatenate([in_slab[lo + s, rows, :] for s in range(n)], axis=-1)
        v = gather(0, nv).astype(BF16)
        q = gather(nv, nq)
        k = gather(nv + nq, nq)
        on, new_states = _ret_sequence(q, k, v, lambda h: s_out[b, h], cos, sin, decay_ref, kdec, cross,
                                       sdec, ng)
        for h in range(RET_HEADS):
            s_out[b, h] = new_states[h]
        for s in range(nv):
            out_slab[s, rows, :] = on[:, s * 128:(s + 1) * 128]
        return carry

    lax.fori_loop(0, nseq, one, 0)
    g = z_ref[:, RG:RG + RET_V].astype(F32)
    on = jnp.concatenate([out_slab[s] for s in range(nv)], axis=-1)
    y_ref[...] = (g * _sigmoid(g) * on).astype(y_ref.dtype)


def _ret_mixer_tm(z, s0, norm_g, tabs, row0, n_chunks, y_prev):
    rows_total = z.shape[0]
    c = tabs['chunk']
    rows = c * SUBLANES
    row_at = lambda i: pl.multiple_of(row0 + rows * i, 128)
    const2 = lambda i: (0, 0)
    in_specs = [pl.BlockSpec((pl.Element(rows), pl.Element(Z_RET_W)), lambda i: (row_at(i), Z_RET)),
                pl.BlockSpec((c, RET_QK), lambda i: (i, 0)),
                pl.BlockSpec((c, RET_QK), lambda i: (i, 0)),
                pl.BlockSpec((RET_HEADS, c, c), lambda i: (0, 0, 0)),
                pl.BlockSpec((c, RET_QK), const2),
                pl.BlockSpec((c, RET_V), const2),
                pl.BlockSpec((RET_HEADS, RET_DV), const2),
                pl.BlockSpec((1, RET_V), const2),
                pl.BlockSpec((SUBLANES, RET_HEADS, RET_DK, RET_DV), lambda i: (0, 0, 0, 0))]
    args = [z, tabs['cos'], tabs['sin'], tabs['decay'], tabs['kdec'], tabs['cross'], tabs['sdec'], norm_g, s0]
    aliases = {}
    if y_prev is not None:
        in_specs.append(pl.BlockSpec(memory_space=pl.ANY))
        args.append(y_prev)
        aliases = {len(args) - 1: 0}
    return pl.pallas_call(
        functools.partial(_ret_tm_kernel, steps=c),
        grid=(n_chunks,),
        in_specs=in_specs,
        out_specs=[pl.BlockSpec((pl.Element(rows), pl.Element(RET_V)), lambda i: (row_at(i), 0)),
                   pl.BlockSpec((SUBLANES, RET_HEADS, RET_DK, RET_DV), lambda i: (0, 0, 0, 0))],
        out_shape=[jax.ShapeDtypeStruct((rows_total, RET_V), BF16),
                   jax.ShapeDtypeStruct((SUBLANES, RET_HEADS, RET_DK, RET_DV), F32)],
        scratch_shapes=[pltpu.VMEM((RET_IN_SLABS, rows, 128), F32),
                        pltpu.VMEM((RET_V_SLABS, rows, 128), F32)],
        input_output_aliases=aliases,
        compiler_params=_cparams(("arbitrary",)),
        name="retention_chunks",
    )(*args)


def _ret_tables(pos, chunk, n_real):
    half = RET_DK // 2
    inv = jnp.power(ROPE_BASE, -jnp.arange(half, dtype=F32) / half)
    ang = jnp.asarray(pos, F32)[:, None] * inv
    cos_h = jnp.concatenate([jnp.cos(ang), jnp.cos(ang)], axis=-1)
    sin_h = jnp.concatenate([-jnp.sin(ang), jnp.sin(ang)], axis=-1)
    log_gamma = jnp.log1p(-jnp.exp2(-5.0 - jnp.arange(RET_HEADS, dtype=F32)))
    n = jnp.arange(chunk, dtype=F32)
    diff = n[:, None] - n[None, :]
    decay = jnp.where(diff >= 0, jnp.exp(log_gamma[:, None, None] * jnp.maximum(diff, 0.0)), 0.0)
    kdec = jnp.exp(log_gamma[:, None] * (chunk - 1.0 - n))
    m = jnp.maximum(n - (chunk - n_real), 0.0)
    cross = jnp.exp(log_gamma[:, None] * (m + 1.0))
    return {'chunk': chunk,
            'cos': jnp.tile(cos_h, (1, RET_HEADS)), 'sin': jnp.tile(sin_h, (1, RET_HEADS)),
            'decay': decay,
            'kdec': jnp.repeat(kdec.T, RET_DK, axis=1),
            'cross': jnp.repeat(cross.T, RET_DV, axis=1),
            'sdec': jnp.broadcast_to(jnp.exp(log_gamma * n_real)[:, None], (RET_HEADS, RET_DV))}


def _merge_kernel(x_ref, gt_ref, ya_ref, yb_ref, yc_ref, wa_ref, wb_ref, wc_ref, wo_ref, g_ref, o_ref):
    gate = lambda j: _sigmoid(gt_ref[:, j * D_MODEL:(j + 1) * D_MODEL].astype(F32))
    m = gate(0) * jnp.dot(ya_ref[...], wa_ref[...], preferred_element_type=F32)
    m = m + gate(1) * jnp.dot(yb_ref[...], wb_ref[...], preferred_element_type=F32)
    m = m + gate(2) * jnp.dot(yc_ref[...], wc_ref[...], preferred_element_type=F32)
    mo = jnp.dot(m.astype(BF16), wo_ref[...], preferred_element_type=F32)
    o_ref[...] = x_ref[...] + _rms(mo, g_ref[...])


def _merge(x, z, ya, yb, yc, p, tm):
    rows = x.shape[0]
    row = lambda i: (i, 0)
    const = lambda i: (0, 0)
    return pl.pallas_call(
        _merge_kernel,
        grid=(rows // tm,),
        in_specs=[pl.BlockSpec((tm, D_MODEL), row),
                  pl.BlockSpec((pl.Element(tm), pl.Element(N_BRANCH * D_MODEL)),
                               lambda i: (pl.multiple_of(i * tm, 16), Z_GATES)),
                  pl.BlockSpec((tm, S5_WIDTH), row),
                  pl.BlockSpec((tm, LRU_WIDTH), row),
                  pl.BlockSpec((tm, RET_V), row),
                  pl.BlockSpec((S5_WIDTH, D_MODEL), const),
                  pl.BlockSpec((LRU_WIDTH, D_MODEL), const),
                  pl.BlockSpec((RET_V, D_MODEL), const),
                  pl.BlockSpec((D_MODEL, D_MODEL), const),
                  pl.BlockSpec((1, D_MODEL), const)],
        out_specs=pl.BlockSpec((tm, D_MODEL), row),
        out_shape=jax.ShapeDtypeStruct((rows, D_MODEL), F32),
        compiler_params=_cparams(("arbitrary",)),
        name="branch_merge",
    )(x, z, ya, yb, yc, p['w_branch_a'], p['w_branch_b'], p['w_branch_c'], p['w_out'], p['g_post_mix'])


def _ffn_kernel(x_ref, gpre_ref, wg_ref, wu_ref, cw_ref, cb_ref, wd_ref, gpost_ref, cache_ref,
                o_ref, cache_out, xn_ref, acc_ref, ext_ref, *, bb, n_chunks):
    i = pl.program_id(0)
    c = pl.program_id(1)
    rows = x_ref.shape[0]
    halo = (FFN_CONV - 1) * bb

    @pl.when(c == 0)
    def _():
        xn_ref[...] = _rms(x_ref[...], gpre_ref[...]).astype(BF16)

    @pl.when(i == 0)
    def _():
        ext_ref[c, 0:halo, :] = cache_ref[...]

    xn = xn_ref[...]
    gf = jnp.dot(xn, wg_ref[...], preferred_element_type=F32)
    uf = jnp.dot(xn, wu_ref[...], preferred_element_type=F32)
    ext_ref[c, halo:halo + rows, :] = gf
    gc = cb_ref[...] + cw_ref[0:1, :] * ext_ref[c, 0:rows, :]
    for j in range(1, FFN_CONV):
        gc = gc + cw_ref[j:j + 1, :] * ext_ref[c, j * bb:j * bb + rows, :]
    new_cache = ext_ref[c, rows:rows + halo, :]
    cache_out[c] = new_cache
    ext_ref[c, 0:halo, :] = new_cache
    act = (jax.nn.gelu(gc) * uf).astype(BF16)
    part = jnp.dot(act, wd_ref[...], preferred_element_type=F32)

    @pl.when(c == 0)
    def _():
        acc_ref[...] = part

    @pl.when(c > 0)
    def _():
        acc_ref[...] += part

    @pl.when(c == n_chunks - 1)
    def _():
        o_ref[...] = x_ref[...] + _rms(acc_ref[...], gpost_ref[...])


def _ffn(x, cache, p, bb, tm, tf):
    rows = x.shape[0]
    halo = (FFN_CONV - 1) * bb
    n_chunks = D_FF // tf
    kern = functools.partial(_ffn_kernel, bb=bb, n_chunks=n_chunks)
    return pl.pallas_call(
        kern,
        grid=(rows // tm, n_chunks),
        in_specs=[pl.BlockSpec((tm, D_MODEL), lambda i, c: (i, 0)),
                  pl.BlockSpec((1, D_MODEL), lambda i, c: (0, 0)),
                  pl.BlockSpec((D_MODEL, tf), lambda i, c: (0, c)),
                  pl.BlockSpec((D_MODEL, tf), lambda i, c: (0, c + n_chunks)),
                  pl.BlockSpec((FFN_CONV, tf), lambda i, c: (0, c)),
                  pl.BlockSpec((1, tf), lambda i, c: (0, c)),
                  pl.BlockSpec((tf, D_MODEL), lambda i, c: (c, 0)),
                  pl.BlockSpec((1, D_MODEL), lambda i, c: (0, 0)),
                  pl.BlockSpec((halo, tf), lambda i, c: (0, c))],
        out_specs=[pl.BlockSpec((tm, D_MODEL), lambda i, c: (i, 0)),
                   pl.BlockSpec((n_chunks, halo, tf), lambda i, c: (0, 0, 0))],
        out_shape=[jax.ShapeDtypeStruct((rows, D_MODEL), F32),
                   jax.ShapeDtypeStruct((n_chunks, halo, tf), F32)],
        scratch_shapes=[pltpu.VMEM((tm, D_MODEL), BF16),
                        pltpu.VMEM((tm, D_MODEL), F32),
                        pltpu.VMEM((n_chunks, tm + halo, tf), F32)],
        compiler_params=_cparams(("arbitrary", "arbitrary")),
        name="conv_ffn",
    )(x, p['g_pre_ffn'], p['ffn_w_up'], p['ffn_w_up'], p['ffn_conv_w'], p['ffn_conv_b'],
      p['ffn_w_down'], p['g_post_ffn'], cache)


def _block_diag(w, tiles):
    nb, r, c = w.shape
    per = nb // tiles
    w4 = w.reshape(tiles, per, r, c)
    eye = jnp.eye(per, dtype=w.dtype)
    return jnp.einsum('kgrc,gh->kgrhc', w4, eye).reshape(tiles, per * r, per * c)


def _prep_layer(P, l):
    row = lambda a: a[l].reshape(1, -1).astype(F32)
    p = {
        'g_pre_mix': row(P['g_pre_mix']), 'g_post_mix': row(P['g_post_mix']),
        'g_pre_ffn': row(P['g_pre_ffn']), 'g_post_ffn': row(P['g_post_ffn']),
        'w_in': P['w_in'][l].astype(BF16),
        's5_lre': row(P['s5_a_re']), 's5_lim': row(P['s5_a_im']),
        's5_lstep': jnp.repeat(P['s5_log_step'][l], S5_STATE).reshape(1, S5_N),
        's5_bre': _block_diag(jnp.swapaxes(P['s5_b_re'][l], 1, 2), S5_KT),
        's5_bim': _block_diag(jnp.swapaxes(P['s5_b_im'][l], 1, 2), S5_KT),
        's5_cre': _block_diag(jnp.swapaxes(P['s5_c_re'][l], 1, 2), S5_KT).astype(BF16),
        's5_cim': _block_diag(jnp.swapaxes(P['s5_c_im'][l], 1, 2), S5_KT).astype(BF16),
        's5_d': row(P['s5_d']), 's5_w_glu': P['s5_w_glu'][l].astype(BF16), 's5_b_glu': row(P['s5_b_glu']),
        'lru_conv_w': P['lru_conv_w'][l], 'lru_conv_b': row(P['lru_conv_b']),
        'lru_wg': jnp.concatenate([_block_diag(P['lru_w_a'][l], LRU_KT),
                                   _block_diag(P['lru_w_x'][l], LRU_KT)], axis=-1).astype(BF16),
        'lru_b_a': row(P['lru_b_a']), 'lru_b_x': row(P['lru_b_x']), 'lru_lam': row(P['lru_lam']),
        'ret_norm_g': row(P['ret_norm_g']),
        'w_branch_a': P['w_branch_a'][l].astype(BF16), 'w_branch_b': P['w_branch_b'][l].astype(BF16),
        'w_branch_c': P['w_branch_c'][l].astype(BF16), 'w_out': P['w_out'][l].astype(BF16),
        'ffn_w_up': P['ffn_w_up'][l].astype(BF16), 'ffn_conv_w': P['ffn_conv_w'][l],
        'ffn_conv_b': row(P['ffn_conv_b']), 'ffn_w_down': P['ffn_w_down'][l].astype(BF16),
    }
    p['s5_wb'], p['s5_are'], p['s5_aim'] = _s5_discretize(p)
    return p


def _run_group(x, states, params, bb, n_steps, retention, tiles):
    s5_re, s5_im, lru_h, lru_cache, ret_s, ffn_cache = states
    new = [[] for _ in range(6)]
    for l in range(DEPTH):
        p = params[l]
        z = _norm_matmul(x, p['g_pre_mix'], p['w_in'], tiles['tm_in'], tiles['tn_in'])
        ya, hre, him = _s5_mixer(z, s5_re[l], s5_im[l], p, bb, n_steps, tiles['rows_scan'])
        yb, hl, lcache = _lru_mixer(z, lru_h[l], lru_cache[l], p, bb, n_steps, tiles['rows_scan'])
        yc, rs = retention(z, ret_s[l], p['ret_norm_g'])
        x = _merge(x, z, ya, yb, yc, p, tiles['tm'])
        x, fcache = _ffn(x, ffn_cache[l], p, bb, tiles['tm'], tiles['tf'])
        fcache = jnp.swapaxes(fcache, 0, 1).reshape(-1, D_FF)
        for lst, val in zip(new, (hre, him, hl, lcache, rs, fcache)):
            lst.append(val)
    return x, new


def kernel(x_prompt, x_sample, state_s5_re, state_s5_im, state_lru, cache_lru_conv, state_ret, cache_ffn_conv, meta_tokens, g_pre_mix, g_post_mix, g_pre_ffn, g_post_ffn, w_in, s5_a_re, s5_a_im, s5_log_step, s5_b_re, s5_b_im, s5_c_re, s5_c_im, s5_d, s5_w_glu, s5_b_glu, lru_conv_w, lru_conv_b, lru_w_a, lru_b_a, lru_w_x, lru_b_x, lru_lam, ret_norm_g, w_branch_a, w_branch_b, w_branch_c, w_out, ffn_w_up, ffn_conv_w, ffn_conv_b, ffn_w_down):
    P = {'g_pre_mix': g_pre_mix, 'g_post_mix': g_post_mix, 'g_pre_ffn': g_pre_ffn, 'g_post_ffn': g_post_ffn,
         'w_in': w_in, 's5_a_re': s5_a_re, 's5_a_im': s5_a_im, 's5_log_step': s5_log_step,
         's5_b_re': s5_b_re, 's5_b_im': s5_b_im, 's5_c_re': s5_c_re, 's5_c_im': s5_c_im,
         's5_d': s5_d, 's5_w_glu': s5_w_glu, 's5_b_glu': s5_b_glu,
         'lru_conv_w': lru_conv_w, 'lru_conv_b': lru_conv_b, 'lru_w_a': lru_w_a, 'lru_b_a': lru_b_a,
         'lru_w_x': lru_w_x, 'lru_b_x': lru_b_x, 'lru_lam': lru_lam, 'ret_norm_g': ret_norm_g,
         'w_branch_a': w_branch_a, 'w_branch_b': w_branch_b, 'w_branch_c': w_branch_c, 'w_out': w_out,
         'ffn_w_up': ffn_w_up, 'ffn_conv_w': ffn_conv_w, 'ffn_conv_b': ffn_conv_b, 'ffn_w_down': ffn_w_down}
    params = [_prep_layer(P, l) for l in range(DEPTH)]

    bp, sp, _ = x_prompt.shape
    lp = N_META + sp
    meta = jnp.broadcast_to(meta_tokens.astype(F32)[:, None, :], (N_META, bp, D_MODEL))
    xp = jnp.concatenate([meta, jnp.swapaxes(x_prompt, 0, 1)], axis=0).reshape(lp * bp, D_MODEL)
    zeros = lambda *s: [jnp.zeros(s, F32)] * DEPTH
    st_p = (zeros(bp, S5_N), zeros(bp, S5_N), zeros(bp, LRU_WIDTH),
            zeros((LRU_CONV - 1) * bp, LRU_WIDTH), zeros(bp, RET_HEADS, RET_DK, RET_DV),
            zeros((FFN_CONV - 1) * bp, D_FF))
    assert bp == SUBLANES and sp % RET_CHUNK == 0 and (N_META * bp) % 128 == 0
    tabs_meta = _ret_tables(np.arange(N_META), N_META, N_META)
    tabs_main = _ret_tables(N_META + np.arange(sp), RET_CHUNK, RET_CHUNK)

    def retention_p(z, s0, ng):
        y, s1 = _ret_mixer_tm(z, s0, ng, tabs_meta, 0, 1, None)
        return _ret_mixer_tm(z, s1, ng, tabs_main, N_META * bp, sp // RET_CHUNK, y)

    tiles_p = {'tm_in': _pick_rows(lp, bp, 1400), 'tn_in': 1408, 'rows_scan': 700,
               'tm': _pick_rows(lp, bp, 700), 'tf': 1408}
    yp, new_p = _run_group(xp, st_p, params, bp, lp, retention_p, tiles_p)
    y_prompt = jnp.swapaxes(yp.reshape(lp, bp, D_MODEL), 0, 1)[:, N_META:]

    bs, ds, _ = x_sample.shape
    xs = jnp.swapaxes(x_sample, 0, 1).reshape(ds * bs, D_MODEL)
    tm_major = lambda c: jnp.swapaxes(c, 0, 1).reshape(-1, c.shape[-1])
    st_s = ([state_s5_re[l].reshape(bs, S5_N) for l in range(DEPTH)],
            [state_s5_im[l].reshape(bs, S5_N) for l in range(DEPTH)],
            [state_lru[l] for l in range(DEPTH)],
            [tm_major(cache_lru_conv[l]) for l in range(DEPTH)],
            [state_ret[l] for l in range(DEPTH)],
            [tm_major(cache_ffn_conv[l]) for l in range(DEPTH)])
    chunk_s = 16
    pad_s = chunk_s - ds
    tabs_s = _ret_tables(np.concatenate([np.zeros(pad_s), PAST_LEN + np.arange(ds)]), chunk_s, ds)

    def retention_s(z, s0, ng):
        zr = z[:, Z_RET:Z_RET + Z_RET_W].reshape(ds, bs, Z_RET_W).transpose(1, 0, 2)
        zr = jnp.pad(zr, ((0, 0), (pad_s, 0), (0, 0)))
        yc, s1 = _ret_mixer_bm(zr, s0, ng, tabs_s, SUBLANES)
        return yc[:, pad_s:].transpose(1, 0, 2).reshape(ds * bs, RET_V), s1

    tiles_s = {'tm_in': ds * bs, 'tn_in': 1408, 'rows_scan': ds * bs, 'tm': ds * bs, 'tf': 1408}
    ys, new_s = _run_group(xs, st_s, params, bs, ds, retention_s, tiles_s)
    y_sample = jnp.swapaxes(ys.reshape(ds, bs, D_MODEL), 0, 1)

    def pack(new, b):
        s5r, s5i, lh, lc, rs, fc = [jnp.stack(v) for v in new]
        return (s5r.reshape(DEPTH, b, S5_GROUPS, S5_STATE), s5i.reshape(DEPTH, b, S5_GROUPS, S5_STATE), lh,
                jnp.swapaxes(lc.reshape(DEPTH, LRU_CONV - 1, b, LRU_WIDTH), 1, 2), rs,
                jnp.swapaxes(fc.reshape(DEPTH, FFN_CONV - 1, b, D_FF), 1, 2))

    return (y_prompt, y_sample) + pack(new_p, bp) + pack(new_s, bs)
```

```python
import functools
import math

import jax
import jax.numpy as jnp
import numpy as np
from jax import lax
from jax.experimental import pallas as pl
from jax.experimental.pallas import tpu as pltpu

F32 = jnp.float32
BF16 = jnp.bfloat16

D_MODEL = 1024
DEPTH = 2
PAST_LEN = 16384
N_META = 16
S5_WIDTH = 768
S5_GROUP = 16
S5_GROUPS = S5_WIDTH // S5_GROUP
S5_STATE = 64
S5_N = S5_GROUPS * S5_STATE
LRU_WIDTH = 768
LRU_BLOCK = 64
LRU_BLOCKS = LRU_WIDTH // LRU_BLOCK
LRU_CONV = 4
LRU_C = 8.0
RET_HEADS = 8
RET_DK = 64
RET_DV = 128
RET_QK = RET_HEADS * RET_DK
RET_V = RET_HEADS * RET_DV
RET_CHUNK = 128
ROPE_BASE = 10000.0
N_BRANCH = 3
D_FF = 2816
FFN_CONV = 3
EPS = 1e-6
D_IN = S5_WIDTH + 2 * LRU_WIDTH + 2 * RET_QK + 2 * RET_V + N_BRANCH * D_MODEL

Z_UA = 0
Z_XB = Z_UA + S5_WIDTH
Z_GB = Z_XB + LRU_WIDTH
Z_RET = Z_GB + LRU_WIDTH
Z_RET_W = 2 * RET_QK + 2 * RET_V
Z_GATES = Z_RET + Z_RET_W
RQ, RK, RV, RG = 0, RET_QK, 2 * RET_QK, 2 * RET_QK + RET_V

MXU_K = 256
S5_KT = S5_WIDTH // MXU_K
S5_NT = S5_N // S5_KT
LRU_KT = LRU_WIDTH // MXU_K

SUBLANES = 8
VMEM_LIMIT = 56 * 1024 * 1024


def _cparams(sem):
    return pltpu.CompilerParams(dimension_semantics=sem, vmem_limit_bytes=VMEM_LIMIT)


def _pick_rows(n_steps, bb, target_rows):
    best = None
    for tt in range(1, n_steps + 1):
        if n_steps % tt:
            continue
        rows = tt * bb
        if rows % 16 == 0 and rows <= target_rows:
            best = rows
    if best is None:
        best = n_steps * bb
    return best


def _rms(x, g):
    return x * lax.rsqrt(jnp.mean(x * x, axis=-1, keepdims=True) + EPS) * g


def _sigmoid(x):
    return 0.5 * jnp.tanh(0.5 * x) + 0.5


def _norm_matmul_kernel(x_ref, g_ref, w_ref, o_ref, xn_ref):
    @pl.when(pl.program_id(1) == 0)
    def _():
        xn_ref[...] = _rms(x_ref[...], g_ref[...]).astype(BF16)

    o_ref[...] = jnp.dot(xn_ref[...], w_ref[...],
                         preferred_element_type=F32).astype(o_ref.dtype)


def _norm_matmul(x, g, w, tm, tn):
    rows, d = x.shape
    n = w.shape[1]
    return pl.pallas_call(
        _norm_matmul_kernel,
        grid=(rows // tm, n // tn),
        in_specs=[pl.BlockSpec((tm, d), lambda i, j: (i, 0)),
                  pl.BlockSpec((1, d), lambda i, j: (0, 0)),
                  pl.BlockSpec((d, tn), lambda i, j: (0, j))],
        out_specs=pl.BlockSpec((tm, tn), lambda i, j: (i, j)),
        out_shape=jax.ShapeDtypeStruct((rows, n), BF16),
        scratch_shapes=[pltpu.VMEM((tm, d), BF16)],
        compiler_params=_cparams(("arbitrary", "arbitrary")),
        name="norm_in_proj",
    )(x, g, w)


def _s5_disc_kernel(lre_ref, lim_ref, lstep_ref, bre_ref, bim_ref, wb_ref, are_ref, aim_ref):
    dt = jnp.exp(lstep_ref[...])
    lr = lre_ref[...]
    li = lim_ref[...]
    mag = jnp.exp(lr * dt)
    a_re = mag * jnp.cos(li * dt)
    a_im = mag * jnp.sin(li * dt)
    den = lr * lr + li * li
    nr = a_re - 1.0
    cr = (nr * lr + a_im * li) / den
    ci = (a_im * lr - nr * li) / den
    are_ref[...] = jnp.broadcast_to(a_re, are_ref.shape)
    aim_ref[...] = jnp.broadcast_to(a_im, aim_ref.shape)
    b_r = bre_ref[0]
    b_i = bim_ref[0]
    wb_ref[0, :, 0:S5_NT] = (cr * b_r - ci * b_i).astype(BF16)
    wb_ref[0, :, S5_NT:2 * S5_NT] = (cr * b_i + ci * b_r).astype(BF16)


def _s5_discretize(p):
    vec = pl.BlockSpec((1, S5_NT), lambda k: (0, k))
    blk = pl.BlockSpec((1, MXU_K, S5_NT), lambda k: (k, 0, 0))
    return pl.pallas_call(
        _s5_disc_kernel,
        grid=(S5_KT,),
        in_specs=[vec, vec, vec, blk, blk],
        out_specs=[pl.BlockSpec((1, MXU_K, 2 * S5_NT), lambda k: (k, 0, 0)),
                   pl.BlockSpec((SUBLANES, S5_NT), lambda k: (0, k)),
                   pl.BlockSpec((SUBLANES, S5_NT), lambda k: (0, k))],
        out_shape=[jax.ShapeDtypeStruct((S5_KT, MXU_K, 2 * S5_NT), BF16),
                   jax.ShapeDtypeStruct((SUBLANES, S5_N), F32),
                   jax.ShapeDtypeStruct((SUBLANES, S5_N), F32)],
        compiler_params=_cparams(("arbitrary",)),
        name="s5_discretize",
    )(p['s5_lre'], p['s5_lim'], p['s5_lstep'], p['s5_bre'], p['s5_bim'])


def _s5_kernel(u_ref, wb_ref, are_ref, aim_ref, cre_ref, cim_ref,
               d_ref, wglu_ref, bglu_ref, h0re_ref, h0im_ref,
               y_ref, hre_out, him_out,
               hre_ref, him_ref, sre_ref, sim_ref, *, bb, tc):
    @pl.when(pl.program_id(0) == 0)
    def _():
        hre_ref[...] = h0re_ref[...]
        him_ref[...] = h0im_ref[...]

    u = u_ref[...]
    for kt in range(S5_KT):
        r = jnp.dot(u[:, kt * MXU_K:(kt + 1) * MXU_K], wb_ref[kt], preferred_element_type=F32)
        sre_ref[:, kt * S5_NT:(kt + 1) * S5_NT] = r[:, 0:S5_NT]
        sim_ref[:, kt * S5_NT:(kt + 1) * S5_NT] = r[:, S5_NT:2 * S5_NT]

    def scan_tile(rt, carry):
        r0 = rt * SUBLANES
        for kt in range(S5_KT):
            sl = slice(kt * S5_NT, (kt + 1) * S5_NT)
            ar = are_ref[:, sl]
            ai = aim_ref[:, sl]
            hr0 = hre_ref[pl.ds(r0, SUBLANES), sl]
            hi0 = him_ref[pl.ds(r0, SUBLANES), sl]

            def step(t, h):
                hr, hi = h
                row = pl.multiple_of(t * bb + r0, SUBLANES)
                nr_ = ar * hr - ai * hi + sre_ref[pl.ds(row, SUBLANES), sl]
                ni_ = ar * hi + ai * hr + sim_ref[pl.ds(row, SUBLANES), sl]
                sre_ref[pl.ds(row, SUBLANES), sl] = nr_
                sim_ref[pl.ds(row, SUBLANES), sl] = ni_
                return nr_, ni_

            hr1, hi1 = lax.fori_loop(0, tc, step, (hr0, hi0), unroll=2 if tc % 2 == 0 else 1)
            hre_ref[pl.ds(r0, SUBLANES), sl] = hr1
            him_ref[pl.ds(r0, SUBLANES), sl] = hi1
        return carry

    if bb == SUBLANES:
        scan_tile(0, 0)
    else:
        lax.fori_loop(0, bb // SUBLANES, scan_tile, 0)

    hre_out[...] = hre_ref[...]
    him_out[...] = him_ref[...]

    ys = []
    for kt in range(S5_KT):
        sl = slice(kt * S5_NT, (kt + 1) * S5_NT)
        ys.append(jnp.dot(sre_ref[:, sl].astype(BF16), cre_ref[kt], preferred_element_type=F32)
                  - jnp.dot(sim_ref[:, sl].astype(BF16), cim_ref[kt], preferred_element_type=F32))
    y = jnp.concatenate(ys, axis=-1) + d_ref[...] * u.astype(F32)
    z = jax.nn.gelu(y)
    gate = jnp.dot(z.astype(BF16), wglu_ref[...], preferred_element_type=F32) + bglu_ref[...]
    y_ref[...] = (z * _sigmoid(gate)).astype(y_ref.dtype)


def _s5_mixer(z, h0_re, h0_im, p, bb, n_steps, target_rows):
    rows_total = z.shape[0]
    rows = _pick_rows(n_steps, bb, target_rows)
    tc = rows // bb
    const2 = lambda i: (0, 0)
    const3 = lambda i: (0, 0, 0)
    kern = functools.partial(_s5_kernel, bb=bb, tc=tc)
    return pl.pallas_call(
        kern,
        grid=(rows_total // rows,),
        in_specs=[pl.BlockSpec((rows, S5_WIDTH), lambda i: (i, Z_UA // S5_WIDTH)),
                  pl.BlockSpec((S5_KT, MXU_K, 2 * S5_NT), const3),
                  pl.BlockSpec((SUBLANES, S5_N), const2),
                  pl.BlockSpec((SUBLANES, S5_N), const2),
                  pl.BlockSpec((S5_KT, S5_NT, MXU_K), const3),
                  pl.BlockSpec((S5_KT, S5_NT, MXU_K), const3),
                  pl.BlockSpec((1, S5_WIDTH), const2),
                  pl.BlockSpec((S5_WIDTH, S5_WIDTH), const2),
                  pl.BlockSpec((1, S5_WIDTH), const2),
                  pl.BlockSpec((bb, S5_N), const2),
                  pl.BlockSpec((bb, S5_N), const2)],
        out_specs=[pl.BlockSpec((rows, S5_WIDTH), lambda i: (i, 0)),
                   pl.BlockSpec((bb, S5_N), const2),
                   pl.BlockSpec((bb, S5_N), const2)],
        out_shape=[jax.ShapeDtypeStruct((rows_total, S5_WIDTH), BF16),
                   jax.ShapeDtypeStruct((bb, S5_N), F32),
                   jax.ShapeDtypeStruct((bb, S5_N), F32)],
        scratch_shapes=[pltpu.VMEM((bb, S5_N), F32),
                        pltpu.VMEM((bb, S5_N), F32),
                        pltpu.VMEM((rows, S5_N), F32),
                        pltpu.VMEM((rows, S5_N), F32)],
        compiler_params=_cparams(("arbitrary",)),
        name="s5_mixer",
    )(z, p['s5_wb'], p['s5_are'], p['s5_aim'], p['s5_cre'], p['s5_cim'],
      p['s5_d'], p['s5_w_glu'], p['s5_b_glu'], h0_re, h0_im)


def _lru_kernel(xb_ref, gb_ref, cache_ref, cw_ref, cb_ref, wg_ref, ba_ref, bx_ref, lam_ref, h0_ref,
                y_ref, h_out, cache_out,
                ext_ref, a_ref, b_ref, h_ref, *, bb, tc):
    rows = tc * bb
    halo = (LRU_CONV - 1) * bb

    @pl.when(pl.program_id(0) == 0)
    def _():
        ext_ref[0:halo, :] = cache_ref[...]
        h_ref[...] = h0_ref[...]

    ext_ref[halo:halo + rows, :] = xb_ref[...].astype(F32)
    xc = cb_ref[...] + cw_ref[0:1, :] * ext_ref[0:rows, :]
    for j in range(1, LRU_CONV):
        xc = xc + cw_ref[j:j + 1, :] * ext_ref[j * bb:j * bb + rows, :]
    new_cache = ext_ref[rows:rows + halo, :]
    cache_out[...] = new_cache
    ext_ref[0:halo, :] = new_cache

    xcb = xc.astype(BF16)
    rs, is_ = [], []
    for kt in range(LRU_KT):
        g = jnp.dot(xcb[:, kt * MXU_K:(kt + 1) * MXU_K], wg_ref[kt], preferred_element_type=F32)
        rs.append(g[:, 0:MXU_K])
        is_.append(g[:, MXU_K:2 * MXU_K])
    r = _sigmoid(jnp.concatenate(rs, axis=-1) + ba_ref[...])
    ig = _sigmoid(jnp.concatenate(is_, axis=-1) + bx_ref[...])
    nl = -lam_ref[...]
    softplus = jnp.maximum(nl, 0.0) + jnp.log1p(jnp.exp(-jnp.abs(nl)))
    log_a = (-LRU_C) * r * softplus
    a = jnp.exp(log_a)
    mult = jnp.sqrt(-jnp.tanh(log_a) * (a * a + 1.0))
    a_ref[...] = a
    b_ref[...] = mult * (ig * xc)

    def scan_tile(rt, carry):
        r0 = rt * SUBLANES
        h0 = h_ref[pl.ds(r0, SUBLANES), :]

        def step(t, h):
            row = pl.multiple_of(t * bb + r0, SUBLANES)
            h = a_ref[pl.ds(row, SUBLANES), :] * h + b_ref[pl.ds(row, SUBLANES), :]
            b_ref[pl.ds(row, SUBLANES), :] = h
            return h

        h1 = lax.fori_loop(0, tc, step, h0, unroll=2 if tc % 2 == 0 else 1)
        h_ref[pl.ds(r0, SUBLANES), :] = h1
        return carry

    if bb == SUBLANES:
        scan_tile(0, 0)
    else:
        lax.fori_loop(0, bb // SUBLANES, scan_tile, 0)

    h_out[...] = h_ref[...]
    y_ref[...] = (b_ref[...] * jax.nn.gelu(gb_ref[...].astype(F32))).astype(y_ref.dtype)


def _lru_mixer(z, h0, cache, p, bb, n_steps, target_rows):
    rows_total = z.shape[0]
    rows = _pick_rows(n_steps, bb, target_rows)
    tc = rows // bb
    halo = (LRU_CONV - 1) * bb
    const2 = lambda i: (0, 0)
    vec = pl.BlockSpec((1, LRU_WIDTH), const2)
    kern = functools.partial(_lru_kernel, bb=bb, tc=tc)
    return pl.pallas_call(
        kern,
        grid=(rows_total // rows,),
        in_specs=[pl.BlockSpec((rows, LRU_WIDTH), lambda i: (i, Z_XB // LRU_WIDTH)),
                  pl.BlockSpec((rows, LRU_WIDTH), lambda i: (i, Z_GB // LRU_WIDTH)),
                  pl.BlockSpec((halo, LRU_WIDTH), const2),
                  pl.BlockSpec((LRU_CONV, LRU_WIDTH), const2),
                  vec,
                  pl.BlockSpec((LRU_KT, MXU_K, 2 * MXU_K), lambda i: (0, 0, 0)),
                  vec, vec, vec,
                  pl.BlockSpec((bb, LRU_WIDTH), const2)],
        out_specs=[pl.BlockSpec((rows, LRU_WIDTH), lambda i: (i, 0)),
                   pl.BlockSpec((bb, LRU_WIDTH), const2),
                   pl.BlockSpec((halo, LRU_WIDTH), const2)],
        out_shape=[jax.ShapeDtypeStruct((rows_total, LRU_WIDTH), BF16),
                   jax.ShapeDtypeStruct((bb, LRU_WIDTH), F32),
                   jax.ShapeDtypeStruct((halo, LRU_WIDTH), F32)],
        scratch_shapes=[pltpu.VMEM((rows + halo, LRU_WIDTH), F32),
                        pltpu.VMEM((rows, LRU_WIDTH), F32),
                        pltpu.VMEM((rows, LRU_WIDTH), F32),
                        pltpu.VMEM((bb, LRU_WIDTH), F32)],
        compiler_params=_cparams(("arbitrary",)),
        name="rglru_mixer",
    )(z, z, cache, p['lru_conv_w'], p['lru_conv_b'], p['lru_wg'], p['lru_b_a'], p['lru_b_x'],
      p['lru_lam'], h0)


def _ret_sequence(q, k, v, state, cos, sin, decay_ref, kdec, cross, sdec, ng):
    half = RET_DK // 2
    lane = lax.broadcasted_iota(jnp.int32, q.shape, 1)
    first = (lane % RET_DK) < half
    mean = lambda x: jnp.mean(x, axis=-1, keepdims=True)

    def rope(x):
        swapped = jnp.where(first, pltpu.roll(x, RET_QK - half, 1), pltpu.roll(x, half, 1))
        return x * cos + swapped * sin

    qr = rope(q).astype(BF16)
    kr = rope(k) * (RET_DK ** -0.5)
    kd = (kr * kdec).astype(BF16)
    kr = kr.astype(BF16)
    outs, new_states = [], []
    for h in range(RET_HEADS):
        qh = qr[:, h * RET_DK:(h + 1) * RET_DK]
        kh = kr[:, h * RET_DK:(h + 1) * RET_DK]
        kdh = kd[:, h * RET_DK:(h + 1) * RET_DK]
        vh = v[:, h * RET_DV:(h + 1) * RET_DV]
        s = state(h)
        sc = lax.dot_general(qh, kh, (((1,), (1,)), ((), ())), preferred_element_type=F32)
        sc = (sc * decay_ref[h]).astype(BF16)
        o = jnp.dot(sc, vh, preferred_element_type=F32)
        o = o + jnp.dot(qh, s.astype(BF16), preferred_element_type=F32) * cross[:, h * RET_DV:(h + 1) * RET_DV]
        new_states.append(sdec[h:h + 1, :] * s + lax.dot_general(kdh, vh, (((0,), (0,)), ((), ())),
                                                                 preferred_element_type=F32))
        oc = o - mean(o)
        outs.append(oc * lax.rsqrt(mean(oc * oc) + EPS) * ng[:, h * RET_DV:(h + 1) * RET_DV])
    return jnp.concatenate(outs, axis=-1), new_states


def _ret_bm_kernel(z_ref, cos_ref, sin_ref, decay_ref, kdec_ref, cross_ref, sdec_ref, ng_ref, s0_ref,
                   y_ref, s_out):
    cos, sin, kdec, cross, sdec, ng = (cos_ref[...], sin_ref[...], kdec_ref[...], cross_ref[...],
                                       sdec_ref[...], ng_ref[...])

    def one(i, carry):
        blk = z_ref[i]
        v = blk[:, RV:RV + RET_V]
        g = blk[:, RG:RG + RET_V].astype(F32)
        q = blk[:, RQ:RQ + RET_QK].astype(F32)
        k = blk[:, RK:RK + RET_QK].astype(F32)
        on, new_states = _ret_sequence(q, k, v, lambda h: s0_ref[i, h], cos, sin, decay_ref, kdec, cross,
                                       sdec, ng)
        for h in range(RET_HEADS):
            s_out[i, h] = new_states[h]
        y_ref[i] = (g * _sigmoid(g) * on).astype(y_ref.dtype)
        return carry

    lax.fori_loop(0, z_ref.shape[0], one, 0, unroll=True)


def _ret_mixer_bm(zr, s0_layers, layer, norm_g, tabs, group):
    nb, c, _ = zr.shape
    const2 = lambda i: (0, 0)
    return pl.pallas_call(
        _ret_bm_kernel,
        grid=(nb // group,),
        in_specs=[pl.BlockSpec((group, c, Z_RET_W), lambda i: (i, 0, 0)),
                  pl.BlockSpec((c, RET_QK), const2),
                  pl.BlockSpec((c, RET_QK), const2),
                  pl.BlockSpec((RET_HEADS, c, c), lambda i: (0, 0, 0)),
                  pl.BlockSpec((c, RET_QK), const2),
                  pl.BlockSpec((c, RET_V), const2),
                  pl.BlockSpec((RET_HEADS, RET_DV), const2),
                  pl.BlockSpec((1, RET_V), const2),
                  pl.BlockSpec((None, group, RET_HEADS, RET_DK, RET_DV), lambda i: (layer, i, 0, 0, 0))],
        out_specs=[pl.BlockSpec((group, c, RET_V), lambda i: (i, 0, 0)),
                   pl.BlockSpec((group, RET_HEADS, RET_DK, RET_DV), lambda i: (i, 0, 0, 0))],
        out_shape=[jax.ShapeDtypeStruct((nb, c, RET_V), BF16),
                   jax.ShapeDtypeStruct((nb, RET_HEADS, RET_DK, RET_DV), F32)],
        compiler_params=_cparams(("arbitrary",)),
        name="retention_step",
    )(zr, tabs['cos'], tabs['sin'], tabs['decay'], tabs['kdec'], tabs['cross'], tabs['sdec'], norm_g,
      s0_layers)


RET_V_SLABS = RET_V // 128
RET_QK_SLABS = RET_QK // 128
RET_IN_SLABS = RET_V_SLABS + 2 * RET_QK_SLABS


def _ret_tm_kernel(*refs, steps):
    z_ref, cos_ref, sin_ref, decay_ref, kdec_ref, cross_ref, sdec_ref, ng_ref, s0_ref = refs[:9]
    y_ref, s_out, in_slab, out_slab = refs[-4:]
    nseq = SUBLANES

    @pl.when(pl.program_id(0) == 0)
    def _():
        s_out[...] = s0_ref[...]

    def fill(lo, n, x):
        for s in range(n):
            in_slab[lo + s] = x[:, s * 128:(s + 1) * 128].astype(F32)

    nv, nq = RET_V_SLABS, RET_QK_SLABS
    fill(0, nv, z_ref[:, RV:RV + RET_V])
    fill(nv, nq, z_ref[:, RQ:RQ + RET_QK])
    fill(nv + nq, nq, z_ref[:, RK:RK + RET_QK])
    cos, sin, kdec, cross, sdec, ng = (cos_ref[...], sin_ref[...], kdec_ref[...], cross_ref[...],
                                       sdec_ref[...], ng_ref[...])

    def one(b, carry):
        rows = pl.ds(b, steps, stride=nseq)
        gather = lambda lo, n: jnp.concatenate([in_slab[lo + s, rows, :] for s in range(n)], axis=-1)
        v = gather(0, nv).astype(BF16)
        q = gather(nv, nq)
        k = gather(nv + nq, nq)
        on, new_states = _ret_sequence(q, k, v, lambda h: s_out[b, h], cos, sin, decay_ref, kdec, cross,
                                       sdec, ng)
        for h in range(RET_HEADS):
            s_out[b, h] = new_states[h]
        for s in range(nv):
            out_slab[s, rows, :] = on[:, s * 128:(s + 1) * 128]
        return carry

    lax.fori_loop(0, nseq, one, 0)
    g = z_ref[:, RG:RG + RET_V].astype(F32)
    on = jnp.concatenate([out_slab[s] for s in range(nv)], axis=-1)
    y_ref[...] = (g * _sigmoid(g) * on).astype(y_ref.dtype)


def _ret_mixer_tm(z, s0, norm_g, tabs, row0, n_chunks):
    c = tabs['chunk']
    rows = c * SUBLANES
    row_at = lambda i: pl.multiple_of(row0 + rows * i, 128)
    const2 = lambda i: (0, 0)
    in_specs = [pl.BlockSpec((pl.Element(rows), pl.Element(Z_RET_W)), lambda i: (row_at(i), Z_RET)),
                pl.BlockSpec((c, RET_QK), lambda i: (i, 0)),
                pl.BlockSpec((c, RET_QK), lambda i: (i, 0)),
                pl.BlockSpec((RET_HEADS, c, c), lambda i: (0, 0, 0)),
                pl.BlockSpec((c, RET_QK), const2),
                pl.BlockSpec((c, RET_V), const2),
                pl.BlockSpec((RET_HEADS, RET_DV), const2),
                pl.BlockSpec((1, RET_V), const2),
                pl.BlockSpec((SUBLANES, RET_HEADS, RET_DK, RET_DV), lambda i: (0, 0, 0, 0))]
    args = [z, tabs['cos'], tabs['sin'], tabs['decay'], tabs['kdec'], tabs['cross'], tabs['sdec'], norm_g, s0]
    return pl.pallas_call(
        functools.partial(_ret_tm_kernel, steps=c),
        grid=(n_chunks,),
        in_specs=in_specs,
        out_specs=[pl.BlockSpec((rows, RET_V), lambda i: (i, 0)),
                   pl.BlockSpec((SUBLANES, RET_HEADS, RET_DK, RET_DV), lambda i: (0, 0, 0, 0))],
        out_shape=[jax.ShapeDtypeStruct((n_chunks * rows, RET_V), BF16),
                   jax.ShapeDtypeStruct((SUBLANES, RET_HEADS, RET_DK, RET_DV), F32)],
        scratch_shapes=[pltpu.VMEM((RET_IN_SLABS, rows, 128), F32),
                        pltpu.VMEM((RET_V_SLABS, rows, 128), F32)],
        compiler_params=_cparams(("arbitrary",)),
        name="retention_chunks",
    )(*args)


def _ret_tables(pos, chunk, n_real):
    half = RET_DK // 2
    inv = jnp.power(ROPE_BASE, -jnp.arange(half, dtype=F32) / half)
    ang = jnp.asarray(pos, F32)[:, None] * inv
    cos_h = jnp.concatenate([jnp.cos(ang), jnp.cos(ang)], axis=-1)
    sin_h = jnp.concatenate([-jnp.sin(ang), jnp.sin(ang)], axis=-1)
    log_gamma = jnp.log1p(-jnp.exp2(-5.0 - jnp.arange(RET_HEADS, dtype=F32)))
    n = jnp.arange(chunk, dtype=F32)
    diff = n[:, None] - n[None, :]
    decay = jnp.where(diff >= 0, jnp.exp(log_gamma[:, None, None] * jnp.maximum(diff, 0.0)), 0.0)
    kdec = jnp.exp(log_gamma[:, None] * (chunk - 1.0 - n))
    m = jnp.maximum(n - (chunk - n_real), 0.0)
    cross = jnp.exp(log_gamma[:, None] * (m + 1.0))
    return {'chunk': chunk,
            'cos': jnp.tile(cos_h, (1, RET_HEADS)), 'sin': jnp.tile(sin_h, (1, RET_HEADS)),
            'decay': decay,
            'kdec': jnp.repeat(kdec.T, RET_DK, axis=1),
            'cross': jnp.repeat(cross.T, RET_DV, axis=1),
            'sdec': jnp.broadcast_to(jnp.exp(log_gamma * n_real)[:, None], (RET_HEADS, RET_DV))}


def _merge_kernel(x_ref, gt_ref, ya_ref, yb_ref, yc_ref, *rest, head):
    if head:
        yh_ref, wa_ref, wb_ref, wc_ref, wo_ref, g_ref, o_ref, ycs_ref = rest
        tm = yc_ref.shape[0]

        @pl.when(pl.program_id(0) == 0)
        def _():
            ycs_ref[0:head, :] = yh_ref[...]
            ycs_ref[head:tm, :] = yc_ref[0:tm - head, :]

        @pl.when(pl.program_id(0) > 0)
        def _():
            ycs_ref[...] = yc_ref[...]

        yc = ycs_ref[...]
    else:
        wa_ref, wb_ref, wc_ref, wo_ref, g_ref, o_ref = rest
        yc = yc_ref[...]
    gate = lambda j: _sigmoid(gt_ref[:, j * D_MODEL:(j + 1) * D_MODEL].astype(F32))
    m = gate(0) * jnp.dot(ya_ref[...], wa_ref[...], preferred_element_type=F32)
    m = m + gate(1) * jnp.dot(yb_ref[...], wb_ref[...], preferred_element_type=F32)
    m = m + gate(2) * jnp.dot(yc, wc_ref[...], preferred_element_type=F32)
    mo = jnp.dot(m.astype(BF16), wo_ref[...], preferred_element_type=F32)
    o_ref[...] = x_ref[...] + _rms(mo, g_ref[...])


def _merge(x, z, ya, yb, yc, p, tm):
    rows = x.shape[0]
    row = lambda i: (i, 0)
    const = lambda i: (0, 0)
    if isinstance(yc, tuple):
        y_head, yc = yc
        head = y_head.shape[0]
        assert head % 16 == 0 and head < tm and head + yc.shape[0] == rows
        yc_specs = [pl.BlockSpec((pl.Element(tm), pl.Element(RET_V)),
                                 lambda i: (pl.multiple_of(jnp.maximum(i * tm - head, 0), 16), 0)),
                    pl.BlockSpec((head, RET_V), const)]
        yc_args = [yc, y_head]
        scratch = [pltpu.VMEM((tm, RET_V), BF16)]
    else:
        head = 0
        yc_specs = [pl.BlockSpec((tm, RET_V), row)]
        yc_args = [yc]
        scratch = []
    return pl.pallas_call(
        functools.partial(_merge_kernel, head=head),
        grid=(rows // tm,),
        in_specs=[pl.BlockSpec((tm, D_MODEL), row),
                  pl.BlockSpec((pl.Element(tm), pl.Element(N_BRANCH * D_MODEL)),
                               lambda i: (pl.multiple_of(i * tm, 16), Z_GATES)),
                  pl.BlockSpec((tm, S5_WIDTH), row),
                  pl.BlockSpec((tm, LRU_WIDTH), row)] + yc_specs + [
                  pl.BlockSpec((S5_WIDTH, D_MODEL), const),
                  pl.BlockSpec((LRU_WIDTH, D_MODEL), const),
                  pl.BlockSpec((RET_V, D_MODEL), const),
                  pl.BlockSpec((D_MODEL, D_MODEL), const),
                  pl.BlockSpec((1, D_MODEL), const)],
        out_specs=pl.BlockSpec((tm, D_MODEL), row),
        out_shape=jax.ShapeDtypeStruct((rows, D_MODEL), F32),
        scratch_shapes=scratch,
        compiler_params=_cparams(("arbitrary",)),
        name="branch_merge",
    )(x, z, ya, yb, *yc_args, p['w_branch_a'], p['w_branch_b'], p['w_branch_c'], p['w_out'], p['g_post_mix'])


def _ffn_kernel(x_ref, gpre_ref, wg_ref, wu_ref, cw_ref, cb_ref, wd_ref, gpost_ref, cache_ref,
                o_ref, cache_out, xn_ref, acc_ref, ext_ref, *, bb, n_chunks):
    i = pl.program_id(0)
    c = pl.program_id(1)
    rows = x_ref.shape[0]
    halo = (FFN_CONV - 1) * bb

    @pl.when(c == 0)
    def _():
        xn_ref[...] = _rms(x_ref[...], gpre_ref[...]).astype(BF16)

    @pl.when(i == 0)
    def _():
        ext_ref[c, 0:halo, :] = cache_ref[...]

    xn = xn_ref[...]
    gf = jnp.dot(xn, wg_ref[...], preferred_element_type=F32)
    uf = jnp.dot(xn, wu_ref[...], preferred_element_type=F32)
    ext_ref[c, halo:halo + rows, :] = gf
    gc = cb_ref[...] + cw_ref[0:1, :] * ext_ref[c, 0:rows, :]
    for j in range(1, FFN_CONV):
        gc = gc + cw_ref[j:j + 1, :] * ext_ref[c, j * bb:j * bb + rows, :]
    new_cache = ext_ref[c, rows:rows + halo, :]
    cache_out[c] = new_cache
    ext_ref[c, 0:halo, :] = new_cache
    act = (jax.nn.gelu(gc) * uf).astype(BF16)
    part = jnp.dot(act, wd_ref[...], preferred_element_type=F32)

    @pl.when(c == 0)
    def _():
        acc_ref[...] = part

    @pl.when(c > 0)
    def _():
        acc_ref[...] += part

    @pl.when(c == n_chunks - 1)
    def _():
        o_ref[...] = x_ref[...] + _rms(acc_ref[...], gpost_ref[...])


def _ffn(x, cache, p, bb, tm, tf):
    rows = x.shape[0]
    halo = (FFN_CONV - 1) * bb
    n_chunks = D_FF // tf
    kern = functools.partial(_ffn_kernel, bb=bb, n_chunks=n_chunks)
    return pl.pallas_call(
        kern,
        grid=(rows // tm, n_chunks),
        in_specs=[pl.BlockSpec((tm, D_MODEL), lambda i, c: (i, 0)),
                  pl.BlockSpec((1, D_MODEL), lambda i, c: (0, 0)),
                  pl.BlockSpec((D_MODEL, tf), lambda i, c: (0, c)),
                  pl.BlockSpec((D_MODEL, tf), lambda i, c: (0, c + n_chunks)),
                  pl.BlockSpec((FFN_CONV, tf), lambda i, c: (0, c)),
                  pl.BlockSpec((1, tf), lambda i, c: (0, c)),
                  pl.BlockSpec((tf, D_MODEL), lambda i, c: (c, 0)),
                  pl.BlockSpec((1, D_MODEL), lambda i, c: (0, 0)),
                  pl.BlockSpec((halo, tf), lambda i, c: (0, c))],
        out_specs=[pl.BlockSpec((tm, D_MODEL), lambda i, c: (i, 0)),
                   pl.BlockSpec((n_chunks, halo, tf), lambda i, c: (0, 0, 0))],
        out_shape=[jax.ShapeDtypeStruct((rows, D_MODEL), F32),
                   jax.ShapeDtypeStruct((n_chunks, halo, tf), F32)],
        scratch_shapes=[pltpu.VMEM((tm, D_MODEL), BF16),
                        pltpu.VMEM((tm, D_MODEL), F32),
                        pltpu.VMEM((n_chunks, tm + halo, tf), F32)],
        compiler_params=_cparams(("arbitrary", "arbitrary")),
        name="conv_ffn",
    )(x, p['g_pre_ffn'], p['ffn_w_up'], p['ffn_w_up'], p['ffn_conv_w'], p['ffn_conv_b'],
      p['ffn_w_down'], p['g_post_ffn'], cache)


def _block_diag(w, tiles):
    nb, r, c = w.shape
    per = nb // tiles
    w4 = w.reshape(tiles, per, r, c)
    eye = jnp.eye(per, dtype=w.dtype)
    return jnp.einsum('kgrc,gh->kgrhc', w4, eye).reshape(tiles, per * r, per * c)


def _prep_layer(P, l):
    row = lambda a: a[l].reshape(1, -1).astype(F32)
    p = {
        'g_pre_mix': row(P['g_pre_mix']), 'g_post_mix': row(P['g_post_mix']),
        'g_pre_ffn': row(P['g_pre_ffn']), 'g_post_ffn': row(P['g_post_ffn']),
        'w_in': P['w_in'][l].astype(BF16),
        's5_lre': row(P['s5_a_re']), 's5_lim': row(P['s5_a_im']),
        's5_lstep': jnp.repeat(P['s5_log_step'][l], S5_STATE).reshape(1, S5_N),
        's5_bre': _block_diag(jnp.swapaxes(P['s5_b_re'][l], 1, 2), S5_KT),
        's5_bim': _block_diag(jnp.swapaxes(P['s5_b_im'][l], 1, 2), S5_KT),
        's5_cre': _block_diag(jnp.swapaxes(P['s5_c_re'][l], 1, 2), S5_KT).astype(BF16),
        's5_cim': _block_diag(jnp.swapaxes(P['s5_c_im'][l], 1, 2), S5_KT).astype(BF16),
        's5_d': row(P['s5_d']), 's5_w_glu': P['s5_w_glu'][l].astype(BF16), 's5_b_glu': row(P['s5_b_glu']),
        'lru_conv_w': P['lru_conv_w'][l], 'lru_conv_b': row(P['lru_conv_b']),
        'lru_wg': jnp.concatenate([_block_diag(P['lru_w_a'][l], LRU_KT),
                                   _block_diag(P['lru_w_x'][l], LRU_KT)], axis=-1).astype(BF16),
        'lru_b_a': row(P['lru_b_a']), 'lru_b_x': row(P['lru_b_x']), 'lru_lam': row(P['lru_lam']),
        'ret_norm_g': row(P['ret_norm_g']),
        'w_branch_a': P['w_branch_a'][l].astype(BF16), 'w_branch_b': P['w_branch_b'][l].astype(BF16),
        'w_branch_c': P['w_branch_c'][l].astype(BF16), 'w_out': P['w_out'][l].astype(BF16),
        'ffn_w_up': P['ffn_w_up'][l].astype(BF16), 'ffn_conv_w': P['ffn_conv_w'][l],
        'ffn_conv_b': row(P['ffn_conv_b']), 'ffn_w_down': P['ffn_w_down'][l].astype(BF16),
    }
    p['s5_wb'], p['s5_are'], p['s5_aim'] = _s5_discretize(p)
    return p


def _run_group(x, states, params, bb, n_steps, retention, tiles):
    s5_re, s5_im, lru_h, lru_cache, ret_s, ffn_cache = states
    new = [[] for _ in range(6)]
    for l in range(DEPTH):
        p = params[l]
        z = _norm_matmul(x, p['g_pre_mix'], p['w_in'], tiles['tm_in'], tiles['tn_in'])
        ya, hre, him = _s5_mixer(z, s5_re[l], s5_im[l], p, bb, n_steps, tiles['rows_scan'])
        yb, hl, lcache = _lru_mixer(z, lru_h[l], lru_cache[l], p, bb, n_steps, tiles['rows_scan'])
        yc, rs = retention(z, ret_s[l], p['ret_norm_g'])
        x = _merge(x, z, ya, yb, yc, p, tiles['tm'])
        x, fcache = _ffn(x, ffn_cache[l], p, bb, tiles['tm'], tiles['tf'])
        fcache = jnp.swapaxes(fcache, 0, 1).reshape(-1, D_FF)
        for lst, val in zip(new, (hre, him, hl, lcache, rs, fcache)):
            lst.append(val)
    return x, new


def kernel(x_prompt, x_sample, state_s5_re, state_s5_im, state_lru, cache_lru_conv, state_ret, cache_ffn_conv, meta_tokens, g_pre_mix, g_post_mix, g_pre_ffn, g_post_ffn, w_in, s5_a_re, s5_a_im, s5_log_step, s5_b_re, s5_b_im, s5_c_re, s5_c_im, s5_d, s5_w_glu, s5_b_glu, lru_conv_w, lru_conv_b, lru_w_a, lru_b_a, lru_w_x, lru_b_x, lru_lam, ret_norm_g, w_branch_a, w_branch_b, w_branch_c, w_out, ffn_w_up, ffn_conv_w, ffn_conv_b, ffn_w_down):
    P = {'g_pre_mix': g_pre_mix, 'g_post_mix': g_post_mix, 'g_pre_ffn': g_pre_ffn, 'g_post_ffn': g_post_ffn,
         'w_in': w_in, 's5_a_re': s5_a_re, 's5_a_im': s5_a_im, 's5_log_step': s5_log_step,
         's5_b_re': s5_b_re, 's5_b_im': s5_b_im, 's5_c_re': s5_c_re, 's5_c_im': s5_c_im,
         's5_d': s5_d, 's5_w_glu': s5_w_glu, 's5_b_glu': s5_b_glu,
         'lru_conv_w': lru_conv_w, 'lru_conv_b': lru_conv_b, 'lru_w_a': lru_w_a, 'lru_b_a': lru_b_a,
         'lru_w_x': lru_w_x, 'lru_b_x': lru_b_x, 'lru_lam': lru_lam, 'ret_norm_g': ret_norm_g,
         'w_branch_a': w_branch_a, 'w_branch_b': w_branch_b, 'w_branch_c': w_branch_c, 'w_out': w_out,
         'ffn_w_up': ffn_w_up, 'ffn_conv_w': ffn_conv_w, 'ffn_conv_b': ffn_conv_b, 'ffn_w_down': ffn_w_down}
    params = [_prep_layer(P, l) for l in range(DEPTH)]

    bp, sp, _ = x_prompt.shape
    lp = N_META + sp
    meta = jnp.broadcast_to(meta_tokens.astype(F32)[:, None, :], (N_META, bp, D_MODEL))
    xp = jnp.concatenate([meta, jnp.swapaxes(x_prompt, 0, 1)], axis=0).reshape(lp * bp, D_MODEL)
    zeros = lambda *s: [jnp.zeros(s, F32)] * DEPTH
    st_p = (zeros(bp, S5_N), zeros(bp, S5_N), zeros(bp, LRU_WIDTH),
            zeros((LRU_CONV - 1) * bp, LRU_WIDTH), zeros(bp, RET_HEADS, RET_DK, RET_DV),
            zeros((FFN_CONV - 1) * bp, D_FF))
    assert bp == SUBLANES and sp % RET_CHUNK == 0 and (N_META * bp) % 128 == 0
    tabs_meta = _ret_tables(np.arange(N_META), N_META, N_META)
    tabs_main = _ret_tables(N_META + np.arange(sp), RET_CHUNK, RET_CHUNK)

    def retention_p(z, s0, ng):
        y_meta, s1 = _ret_mixer_tm(z, s0, ng, tabs_meta, 0, 1)
        y_main, s2 = _ret_mixer_tm(z, s1, ng, tabs_main, N_META * bp, sp // RET_CHUNK)
        return (y_meta, y_main), s2

    tiles_p = {'tm_in': _pick_rows(lp, bp, 1400), 'tn_in': 1408, 'rows_scan': 700,
               'tm': _pick_rows(lp, bp, 700), 'tf': 1408}
    yp, new_p = _run_group(xp, st_p, params, bp, lp, retention_p, tiles_p)
    y_prompt = jnp.swapaxes(yp.reshape(lp, bp, D_MODEL), 0, 1)[:, N_META:]

    bs, ds, _ = x_sample.shape
    xs = jnp.swapaxes(x_sample, 0, 1).reshape(ds * bs, D_MODEL)
    tm_major = lambda c: jnp.swapaxes(c, 0, 1).reshape(-1, c.shape[-1])
    st_s = ([state_s5_re[l].reshape(bs, S5_N) for l in range(DEPTH)],
            [state_s5_im[l].reshape(bs, S5_N) for l in range(DEPTH)],
            [state_lru[l] for l in range(DEPTH)],
            [tm_major(cache_lru_conv[l]) for l in range(DEPTH)],
            list(range(DEPTH)),
            [tm_major(cache_ffn_conv[l]) for l in range(DEPTH)])
    chunk_s = 16
    pad_s = chunk_s - ds
    tabs_s = _ret_tables(np.concatenate([np.zeros(pad_s), PAST_LEN + np.arange(ds)]), chunk_s, ds)

    def retention_s(z, layer, ng):
        zr = z[:, Z_RET:Z_RET + Z_RET_W].reshape(ds, bs, Z_RET_W).transpose(1, 0, 2)
        zr = jnp.pad(zr, ((0, 0), (pad_s, 0), (0, 0)))
        yc, s1 = _ret_mixer_bm(zr, state_ret, layer, ng, tabs_s, SUBLANES)
        return yc[:, pad_s:].transpose(1, 0, 2).reshape(ds * bs, RET_V), s1

    tiles_s = {'tm_in': ds * bs, 'tn_in': 1408, 'rows_scan': ds * bs, 'tm': ds * bs, 'tf': 1408}
    ys, new_s = _run_group(xs, st_s, params, bs, ds, retention_s, tiles_s)
    y_sample = jnp.swapaxes(ys.reshape(ds, bs, D_MODEL), 0, 1)

    def pack(new, b):
        s5r, s5i, lh, lc, rs, fc = [jnp.stack(v) for v in new]
        return (s5r.reshape(DEPTH, b, S5_GROUPS, S5_STATE), s5i.reshape(DEPTH, b, S5_GROUPS, S5_STATE), lh,
                jnp.swapaxes(lc.reshape(DEPTH, LRU_CONV - 1, b, LRU_WIDTH), 1, 2), rs,
                jnp.swapaxes(fc.reshape(DEPTH, FFN_CONV - 1, b, D_FF), 1, 2))

    return (y_prompt, y_sample) + pack(new_p, bp) + pack(new_s, bs)
```

```python
import functools
import math

import jax
import jax.numpy as jnp
import numpy as np
from jax import lax
from jax.experimental import pallas as pl
from jax.experimental.pallas import tpu as pltpu

F32 = jnp.float32
BF16 = jnp.bfloat16

D_MODEL = 1024
DEPTH = 2
PAST_LEN = 16384
N_META = 16
S5_WIDTH = 768
S5_GROUP = 16
S5_GROUPS = S5_WIDTH // S5_GROUP
S5_STATE = 64
S5_N = S5_GROUPS * S5_STATE
LRU_WIDTH = 768
LRU_BLOCK = 64
LRU_BLOCKS = LRU_WIDTH // LRU_BLOCK
LRU_CONV = 4
LRU_C = 8.0
RET_HEADS = 8
RET_DK = 64
RET_DV = 128
RET_QK = RET_HEADS * RET_DK
RET_V = RET_HEADS * RET_DV
RET_CHUNK = 128
ROPE_BASE = 10000.0
N_BRANCH = 3
D_FF = 2816
FFN_CONV = 3
EPS = 1e-6
D_IN = S5_WIDTH + 2 * LRU_WIDTH + 2 * RET_QK + 2 * RET_V + N_BRANCH * D_MODEL

Z_UA = 0
Z_XB = Z_UA + S5_WIDTH
Z_GB = Z_XB + LRU_WIDTH
Z_RET = Z_GB + LRU_WIDTH
Z_RET_W = 2 * RET_QK + 2 * RET_V
Z_GATES = Z_RET + Z_RET_W
RQ, RK, RV, RG = 0, RET_QK, 2 * RET_QK, 2 * RET_QK + RET_V

MXU_K = 256
S5_KT = S5_WIDTH // MXU_K
S5_NT = S5_N // S5_KT
LRU_KT = LRU_WIDTH // MXU_K

SUBLANES = 8
VMEM_LIMIT = 56 * 1024 * 1024


def _cparams(sem):
    return pltpu.CompilerParams(dimension_semantics=sem, vmem_limit_bytes=VMEM_LIMIT)


def _pick_rows(n_steps, bb, target_rows):
    best = None
    for tt in range(1, n_steps + 1):
        if n_steps % tt:
            continue
        rows = tt * bb
        if rows % 16 == 0 and rows <= target_rows:
            best = rows
    if best is None:
        best = n_steps * bb
    return best


def _rms(x, g):
    return x * lax.rsqrt(jnp.mean(x * x, axis=-1, keepdims=True) + EPS) * g


def _sigmoid(x):
    return 0.5 * jnp.tanh(0.5 * x) + 0.5


def _norm_matmul_kernel(x_ref, g_ref, w_ref, o_ref, xn_ref):
    @pl.when(pl.program_id(1) == 0)
    def _():
        xn_ref[...] = _rms(x_ref[...], g_ref[...]).astype(BF16)

    o_ref[...] = jnp.dot(xn_ref[...], w_ref[...],
                         preferred_element_type=F32).astype(o_ref.dtype)


def _norm_matmul(x, g, w, tm, tn):
    rows, d = x.shape
    n = w.shape[1]
    return pl.pallas_call(
        _norm_matmul_kernel,
        grid=(rows // tm, n // tn),
        in_specs=[pl.BlockSpec((tm, d), lambda i, j: (i, 0)),
                  pl.BlockSpec((1, d), lambda i, j: (0, 0)),
                  pl.BlockSpec((d, tn), lambda i, j: (0, j))],
        out_specs=pl.BlockSpec((tm, tn), lambda i, j: (i, j)),
        out_shape=jax.ShapeDtypeStruct((rows, n), BF16),
        scratch_shapes=[pltpu.VMEM((tm, d), BF16)],
        compiler_params=_cparams(("arbitrary", "arbitrary")),
        name="norm_in_proj",
    )(x, g, w)


def _s5_disc_kernel(lre_ref, lim_ref, lstep_ref, bre_ref, bim_ref, wb_ref, are_ref, aim_ref):
    dt = jnp.exp(lstep_ref[...])
    lr = lre_ref[...]
    li = lim_ref[...]
    mag = jnp.exp(lr * dt)
    a_re = mag * jnp.cos(li * dt)
    a_im = mag * jnp.sin(li * dt)
    den = lr * lr + li * li
    nr = a_re - 1.0
    cr = (nr * lr + a_im * li) / den
    ci = (a_im * lr - nr * li) / den
    are_ref[...] = jnp.broadcast_to(a_re, are_ref.shape)
    aim_ref[...] = jnp.broadcast_to(a_im, aim_ref.shape)
    b_r = bre_ref[0]
    b_i = bim_ref[0]
    wb_ref[0, :, 0:S5_NT] = (cr * b_r - ci * b_i).astype(BF16)
    wb_ref[0, :, S5_NT:2 * S5_NT] = (cr * b_i + ci * b_r).astype(BF16)


def _s5_discretize(p):
    vec = pl.BlockSpec((1, S5_NT), lambda k: (0, k))
    blk = pl.BlockSpec((1, MXU_K, S5_NT), lambda k: (k, 0, 0))
    return pl.pallas_call(
        _s5_disc_kernel,
        grid=(S5_KT,),
        in_specs=[vec, vec, vec, blk, blk],
        out_specs=[pl.BlockSpec((1, MXU_K, 2 * S5_NT), lambda k: (k, 0, 0)),
                   pl.BlockSpec((SUBLANES, S5_NT), lambda k: (0, k)),
                   pl.BlockSpec((SUBLANES, S5_NT), lambda k: (0, k))],
        out_shape=[jax.ShapeDtypeStruct((S5_KT, MXU_K, 2 * S5_NT), BF16),
                   jax.ShapeDtypeStruct((SUBLANES, S5_N), F32),
                   jax.ShapeDtypeStruct((SUBLANES, S5_N), F32)],
        compiler_params=_cparams(("arbitrary",)),
        name="s5_discretize",
    )(p['s5_lre'], p['s5_lim'], p['s5_lstep'], p['s5_bre'], p['s5_bim'])


def _s5_kernel(u_ref, wb_ref, are_ref, aim_ref, cre_ref, cim_ref,
               d_ref, wglu_ref, bglu_ref, h0re_ref, h0im_ref,
               y_ref, hre_out, him_out,
               hre_ref, him_ref, sre_ref, sim_ref, *, bb, tc):
    @pl.when(pl.program_id(0) == 0)
    def _():
        hre_ref[...] = h0re_ref[...]
        him_ref[...] = h0im_ref[...]

    u = u_ref[...]
    for kt in range(S5_KT):
        r = jnp.dot(u[:, kt * MXU_K:(kt + 1) * MXU_K], wb_ref[kt], preferred_element_type=F32)
        sre_ref[:, kt * S5_NT:(kt + 1) * S5_NT] = r[:, 0:S5_NT]
        sim_ref[:, kt * S5_NT:(kt + 1) * S5_NT] = r[:, S5_NT:2 * S5_NT]

    def scan_tile(rt, carry):
        r0 = rt * SUBLANES
        for kt in range(S5_KT):
            sl = slice(kt * S5_NT, (kt + 1) * S5_NT)
            ar = are_ref[:, sl]
            ai = aim_ref[:, sl]
            hr0 = hre_ref[pl.ds(r0, SUBLANES), sl]
            hi0 = him_ref[pl.ds(r0, SUBLANES), sl]

            def step(t, h):
                hr, hi = h
                row = pl.multiple_of(t * bb + r0, SUBLANES)
                nr_ = ar * hr - ai * hi + sre_ref[pl.ds(row, SUBLANES), sl]
                ni_ = ar * hi + ai * hr + sim_ref[pl.ds(row, SUBLANES), sl]
                sre_ref[pl.ds(row, SUBLANES), sl] = nr_
                sim_ref[pl.ds(row, SUBLANES), sl] = ni_
                return nr_, ni_

            hr1, hi1 = lax.fori_loop(0, tc, step, (hr0, hi0), unroll=2 if tc % 2 == 0 else 1)
            hre_ref[pl.ds(r0, SUBLANES), sl] = hr1
            him_ref[pl.ds(r0, SUBLANES), sl] = hi1
        return carry

    if bb == SUBLANES:
        scan_tile(0, 0)
    else:
        lax.fori_loop(0, bb // SUBLANES, scan_tile, 0)

    hre_out[...] = hre_ref[...]
    him_out[...] = him_ref[...]

    ys = []
    for kt in range(S5_KT):
        sl = slice(kt * S5_NT, (kt + 1) * S5_NT)
        ys.append(jnp.dot(sre_ref[:, sl].astype(BF16), cre_ref[kt], preferred_element_type=F32)
                  - jnp.dot(sim_ref[:, sl].astype(BF16), cim_ref[kt], preferred_element_type=F32))
    y = jnp.concatenate(ys, axis=-1) + d_ref[...] * u.astype(F32)
    z = jax.nn.gelu(y)
    gate = jnp.dot(z.astype(BF16), wglu_ref[...], preferred_element_type=F32) + bglu_ref[...]
    y_ref[...] = (z * _sigmoid(gate)).astype(y_ref.dtype)


def _s5_mixer(z, h0_re, h0_im, p, bb, n_steps, target_rows):
    rows_total = z.shape[0]
    rows = _pick_rows(n_steps, bb, target_rows)
    tc = rows // bb
    const2 = lambda i: (0, 0)
    const3 = lambda i: (0, 0, 0)
    kern = functools.partial(_s5_kernel, bb=bb, tc=tc)
    return pl.pallas_call(
        kern,
        grid=(rows_total // rows,),
        in_specs=[pl.BlockSpec((rows, S5_WIDTH), lambda i: (i, Z_UA // S5_WIDTH)),
                  pl.BlockSpec((S5_KT, MXU_K, 2 * S5_NT), const3),
                  pl.BlockSpec((SUBLANES, S5_N), const2),
                  pl.BlockSpec((SUBLANES, S5_N), const2),
                  pl.BlockSpec((S5_KT, S5_NT, MXU_K), const3),
                  pl.BlockSpec((S5_KT, S5_NT, MXU_K), const3),
                  pl.BlockSpec((1, S5_WIDTH), const2),
                  pl.BlockSpec((S5_WIDTH, S5_WIDTH), const2),
                  pl.BlockSpec((1, S5_WIDTH), const2),
                  pl.BlockSpec((bb, S5_N), const2),
                  pl.BlockSpec((bb, S5_N), const2)],
        out_specs=[pl.BlockSpec((rows, S5_WIDTH), lambda i: (i, 0)),
                   pl.BlockSpec((bb, S5_N), const2),
                   pl.BlockSpec((bb, S5_N), const2)],
        out_shape=[jax.ShapeDtypeStruct((rows_total, S5_WIDTH), BF16),
                   jax.ShapeDtypeStruct((bb, S5_N), F32),
                   jax.ShapeDtypeStruct((bb, S5_N), F32)],
        scratch_shapes=[pltpu.VMEM((bb, S5_N), F32),
                        pltpu.VMEM((bb, S5_N), F32),
                        pltpu.VMEM((rows, S5_N), F32),
                        pltpu.VMEM((rows, S5_N), F32)],
        compiler_params=_cparams(("arbitrary",)),
        name="s5_mixer",
    )(z, p['s5_wb'], p['s5_are'], p['s5_aim'], p['s5_cre'], p['s5_cim'],
      p['s5_d'], p['s5_w_glu'], p['s5_b_glu'], h0_re, h0_im)


def _lru_kernel(xb_ref, gb_ref, cache_ref, cw_ref, cb_ref, wg_ref, ba_ref, bx_ref, lam_ref, h0_ref,
                y_ref, h_out, cache_out,
                ext_ref, a_ref, b_ref, h_ref, *, bb, tc):
    rows = tc * bb
    halo = (LRU_CONV - 1) * bb

    @pl.when(pl.program_id(0) == 0)
    def _():
        ext_ref[0:halo, :] = cache_ref[...]
        h_ref[...] = h0_ref[...]

    ext_ref[halo:halo + rows, :] = xb_ref[...].astype(F32)
    xc = cb_ref[...] + cw_ref[0:1, :] * ext_ref[0:rows, :]
    for j in range(1, LRU_CONV):
        xc = xc + cw_ref[j:j + 1, :] * ext_ref[j * bb:j * bb + rows, :]
    new_cache = ext_ref[rows:rows + halo, :]
    cache_out[...] = new_cache
    ext_ref[0:halo, :] = new_cache

    xcb = xc.astype(BF16)
    rs, is_ = [], []
    for kt in range(LRU_KT):
        g = jnp.dot(xcb[:, kt * MXU_K:(kt + 1) * MXU_K], wg_ref[kt], preferred_element_type=F32)
        rs.append(g[:, 0:MXU_K])
        is_.append(g[:, MXU_K:2 * MXU_K])
    r = _sigmoid(jnp.concatenate(rs, axis=-1) + ba_ref[...])
    ig = _sigmoid(jnp.concatenate(is_, axis=-1) + bx_ref[...])
    nl = -lam_ref[...]
    softplus = jnp.maximum(nl, 0.0) + jnp.log1p(jnp.exp(-jnp.abs(nl)))
    log_a = (-LRU_C) * r * softplus
    a = jnp.exp(log_a)
    mult = jnp.sqrt(-jnp.tanh(log_a) * (a * a + 1.0))
    a_ref[...] = a
    b_ref[...] = mult * (ig * xc)

    def scan_tile(rt, carry):
        r0 = rt * SUBLANES
        h0 = h_ref[pl.ds(r0, SUBLANES), :]

        def step(t, h):
            row = pl.multiple_of(t * bb + r0, SUBLANES)
            h = a_ref[pl.ds(row, SUBLANES), :] * h + b_ref[pl.ds(row, SUBLANES), :]
            b_ref[pl.ds(row, SUBLANES), :] = h
            return h

        h1 = lax.fori_loop(0, tc, step, h0, unroll=2 if tc % 2 == 0 else 1)
        h_ref[pl.ds(r0, SUBLANES), :] = h1
        return carry

    if bb == SUBLANES:
        scan_tile(0, 0)
    else:
        lax.fori_loop(0, bb // SUBLANES, scan_tile, 0)

    h_out[...] = h_ref[...]
    y_ref[...] = (b_ref[...] * jax.nn.gelu(gb_ref[...].astype(F32))).astype(y_ref.dtype)


def _lru_mixer(z, h0, cache, p, bb, n_steps, target_rows):
    rows_total = z.shape[0]
    rows = _pick_rows(n_steps, bb, target_rows)
    tc = rows // bb
    halo = (LRU_CONV - 1) * bb
    const2 = lambda i: (0, 0)
    vec = pl.BlockSpec((1, LRU_WIDTH), const2)
    kern = functools.partial(_lru_kernel, bb=bb, tc=tc)
    return pl.pallas_call(
        kern,
        grid=(rows_total // rows,),
        in_specs=[pl.BlockSpec((rows, LRU_WIDTH), lambda i: (i, Z_XB // LRU_WIDTH)),
                  pl.BlockSpec((rows, LRU_WIDTH), lambda i: (i, Z_GB // LRU_WIDTH)),
                  pl.BlockSpec((halo, LRU_WIDTH), const2),
                  pl.BlockSpec((LRU_CONV, LRU_WIDTH), const2),
                  vec,
                  pl.BlockSpec((LRU_KT, MXU_K, 2 * MXU_K), lambda i: (0, 0, 0)),
                  vec, vec, vec,
                  pl.BlockSpec((bb, LRU_WIDTH), const2)],
        out_specs=[pl.BlockSpec((rows, LRU_WIDTH), lambda i: (i, 0)),
                   pl.BlockSpec((bb, LRU_WIDTH), const2),
                   pl.BlockSpec((halo, LRU_WIDTH), const2)],
        out_shape=[jax.ShapeDtypeStruct((rows_total, LRU_WIDTH), BF16),
                   jax.ShapeDtypeStruct((bb, LRU_WIDTH), F32),
                   jax.ShapeDtypeStruct((halo, LRU_WIDTH), F32)],
        scratch_shapes=[pltpu.VMEM((rows + halo, LRU_WIDTH), F32),
                        pltpu.VMEM((rows, LRU_WIDTH), F32),
                        pltpu.VMEM((rows, LRU_WIDTH), F32),
                        pltpu.VMEM((bb, LRU_WIDTH), F32)],
        compiler_params=_cparams(("arbitrary",)),
        name="rglru_mixer",
    )(z, z, cache, p['lru_conv_w'], p['lru_conv_b'], p['lru_wg'], p['lru_b_a'], p['lru_b_x'],
      p['lru_lam'], h0)


def _ret_sequence(q, k, v, state, cos, sin, decay_ref, kdec, cross, sdec, ng):
    half = RET_DK // 2
    lane = lax.broadcasted_iota(jnp.int32, q.shape, 1)
    first = (lane % RET_DK) < half
    mean = lambda x: jnp.mean(x, axis=-1, keepdims=True)

    def rope(x):
        swapped = jnp.where(first, pltpu.roll(x, RET_QK - half, 1), pltpu.roll(x, half, 1))
        return x * cos + swapped * sin

    qr = rope(q).astype(BF16)
    kr = rope(k) * (RET_DK ** -0.5)
    kd = (kr * kdec).astype(BF16)
    kr = kr.astype(BF16)
    outs, new_states = [], []
    for h in range(RET_HEADS):
        qh = qr[:, h * RET_DK:(h + 1) * RET_DK]
        kh = kr[:, h * RET_DK:(h + 1) * RET_DK]
        kdh = kd[:, h * RET_DK:(h + 1) * RET_DK]
        vh = v[:, h * RET_DV:(h + 1) * RET_DV]
        s = state(h)
        sc = lax.dot_general(qh, kh, (((1,), (1,)), ((), ())), preferred_element_type=F32)
        sc = (sc * decay_ref[h]).astype(BF16)
        o = jnp.dot(sc, vh, preferred_element_type=F32)
        o = o + jnp.dot(qh, s.astype(BF16), preferred_element_type=F32) * cross[:, h * RET_DV:(h + 1) * RET_DV]
        new_states.append(sdec[h:h + 1, :] * s + lax.dot_general(kdh, vh, (((0,), (0,)), ((), ())),
                                                                 preferred_element_type=F32))
        oc = o - mean(o)
        outs.append(oc * lax.rsqrt(mean(oc * oc) + EPS) * ng[:, h * RET_DV:(h + 1) * RET_DV])
    return jnp.concatenate(outs, axis=-1), new_states


def _ret_bm_kernel(z_ref, cos_ref, sin_ref, decay_ref, kdec_ref, cross_ref, sdec_ref, ng_ref, s0_ref,
                   y_ref, s_out):
    cos, sin, kdec, cross, sdec, ng = (cos_ref[...], sin_ref[...], kdec_ref[...], cross_ref[...],
                                       sdec_ref[...], ng_ref[...])

    def one(i, carry):
        blk = z_ref[i]
        v = blk[:, RV:RV + RET_V]
        g = blk[:, RG:RG + RET_V].astype(F32)
        q = blk[:, RQ:RQ + RET_QK].astype(F32)
        k = blk[:, RK:RK + RET_QK].astype(F32)
        on, new_states = _ret_sequence(q, k, v, lambda h: s0_ref[i, h], cos, sin, decay_ref, kdec, cross,
                                       sdec, ng)
        for h in range(RET_HEADS):
            s_out[i, h] = new_states[h]
        y_ref[i] = (g * _sigmoid(g) * on).astype(y_ref.dtype)
        return carry

    lax.fori_loop(0, z_ref.shape[0], one, 0, unroll=True)


def _ret_mixer_bm(zr, s0_layers, layer, norm_g, tabs, group):
    nb, c, _ = zr.shape
    const2 = lambda i: (0, 0)
    return pl.pallas_call(
        _ret_bm_kernel,
        grid=(nb // group,),
        in_specs=[pl.BlockSpec((group, c, Z_RET_W), lambda i: (i, 0, 0)),
                  pl.BlockSpec((c, RET_QK), const2),
                  pl.BlockSpec((c, RET_QK), const2),
                  pl.BlockSpec((RET_HEADS, c, c), lambda i: (0, 0, 0)),
                  pl.BlockSpec((c, RET_QK), const2),
                  pl.BlockSpec((c, RET_V), const2),
                  pl.BlockSpec((RET_HEADS, RET_DV), const2),
                  pl.BlockSpec((1, RET_V), const2),
                  pl.BlockSpec((None, group, RET_HEADS, RET_DK, RET_DV), lambda i: (layer, i, 0, 0, 0))],
        out_specs=[pl.BlockSpec((group, c, RET_V), lambda i: (i, 0, 0)),
                   pl.BlockSpec((group, RET_HEADS, RET_DK, RET_DV), lambda i: (i, 0, 0, 0))],
        out_shape=[jax.ShapeDtypeStruct((nb, c, RET_V), BF16),
                   jax.ShapeDtypeStruct((nb, RET_HEADS, RET_DK, RET_DV), F32)],
        compiler_params=_cparams(("arbitrary",)),
        name="retention_step",
    )(zr, tabs['cos'], tabs['sin'], tabs['decay'], tabs['kdec'], tabs['cross'], tabs['sdec'], norm_g,
      s0_layers)


RET_V_SLABS = RET_V // 128
RET_QK_SLABS = RET_QK // 128
RET_IN_SLABS = RET_V_SLABS + 2 * RET_QK_SLABS


def _ret_tm_kernel(*refs, steps):
    z_ref, cos_ref, sin_ref, decay_ref, kdec_ref, cross_ref, sdec_ref, ng_ref, s0_ref = refs[:9]
    y_ref, s_out, in_slab, out_slab = refs[-4:]
    nseq = SUBLANES

    @pl.when(pl.program_id(0) == 0)
    def _():
        s_out[...] = s0_ref[...]

    def fill(lo, n, x):
        for s in range(n):
            in_slab[lo + s] = x[:, s * 128:(s + 1) * 128].astype(F32)

    nv, nq = RET_V_SLABS, RET_QK_SLABS
    fill(0, nv, z_ref[:, RV:RV + RET_V])
    fill(nv, nq, z_ref[:, RQ:RQ + RET_QK])
    fill(nv + nq, nq, z_ref[:, RK:RK + RET_QK])
    cos, sin, kdec, cross, sdec, ng = (cos_ref[...], sin_ref[...], kdec_ref[...], cross_ref[...],
                                       sdec_ref[...], ng_ref[...])

    def one(b, carry):
        rows = pl.ds(b, steps, stride=nseq)
        gather = lambda lo, n: jnp.concatenate([in_slab[lo + s, rows, :] for s in range(n)], axis=-1)
        v = gather(0, nv).astype(BF16)
        q = gather(nv, nq)
        k = gather(nv + nq, nq)
        on, new_states = _ret_sequence(q, k, v, lambda h: s_out[b, h], cos, sin, decay_ref, kdec, cross,
                                       sdec, ng)
        for h in range(RET_HEADS):
            s_out[b, h] = new_states[h]
        for s in range(nv):
            out_slab[s, rows, :] = on[:, s * 128:(s + 1) * 128]
        return carry

    lax.fori_loop(0, nseq, one, 0)
    g = z_ref[:, RG:RG + RET_V].astype(F32)
    on = jnp.concatenate([out_slab[s] for s in range(nv)], axis=-1)
    y_ref[...] = (g * _sigmoid(g) * on).astype(y_ref.dtype)


def _ret_mixer_tm(z, s0, norm_g, tabs, row0, n_chunks):
    c = tabs['chunk']
    rows = c * SUBLANES
    row_at = lambda i: pl.multiple_of(row0 + rows * i, 128)
    const2 = lambda i: (0, 0)
    in_specs = [pl.BlockSpec((pl.Element(rows), pl.Element(Z_RET_W)), lambda i: (row_at(i), Z_RET)),
                pl.BlockSpec((c, RET_QK), lambda i: (i, 0)),
                pl.BlockSpec((c, RET_QK), lambda i: (i, 0)),
                pl.BlockSpec((RET_HEADS, c, c), lambda i: (0, 0, 0)),
                pl.BlockSpec((c, RET_QK), const2),
                pl.BlockSpec((c, RET_V), const2),
                pl.BlockSpec((RET_HEADS, RET_DV), const2),
                pl.BlockSpec((1, RET_V), const2),
                pl.BlockSpec((SUBLANES, RET_HEADS, RET_DK, RET_DV), lambda i: (0, 0, 0, 0))]
    args = [z, tabs['cos'], tabs['sin'], tabs['decay'], tabs['kdec'], tabs['cross'], tabs['sdec'], norm_g, s0]
    return pl.pallas_call(
        functools.partial(_ret_tm_kernel, steps=c),
        grid=(n_chunks,),
        in_specs=in_specs,
        out_specs=[pl.BlockSpec((rows, RET_V), lambda i: (i, 0)),
                   pl.BlockSpec((SUBLANES, RET_HEADS, RET_DK, RET_DV), lambda i: (0, 0, 0, 0))],
        out_shape=[jax.ShapeDtypeStruct((n_chunks * rows, RET_V), BF16),
                   jax.ShapeDtypeStruct((SUBLANES, RET_HEADS, RET_DK, RET_DV), F32)],
        scratch_shapes=[pltpu.VMEM((RET_IN_SLABS, rows, 128), F32),
                        pltpu.VMEM((RET_V_SLABS, rows, 128), F32)],
        compiler_params=_cparams(("arbitrary",)),
        name="retention_chunks",
    )(*args)


def _ret_tables(pos, chunk, n_real):
    half = RET_DK // 2
    inv = jnp.power(ROPE_BASE, -jnp.arange(half, dtype=F32) / half)
    ang = jnp.asarray(pos, F32)[:, None] * inv
    cos_h = jnp.concatenate([jnp.cos(ang), jnp.cos(ang)], axis=-1)
    sin_h = jnp.concatenate([-jnp.sin(ang), jnp.sin(ang)], axis=-1)
    log_gamma = jnp.log1p(-jnp.exp2(-5.0 - jnp.arange(RET_HEADS, dtype=F32)))
    n = jnp.arange(chunk, dtype=F32)
    diff = n[:, None] - n[None, :]
    decay = jnp.where(diff >= 0, jnp.exp(log_gamma[:, None, None] * jnp.maximum(diff, 0.0)), 0.0)
    kdec = jnp.exp(log_gamma[:, None] * (chunk - 1.0 - n))
    m = jnp.maximum(n - (chunk - n_real), 0.0)
    cross = jnp.exp(log_gamma[:, None] * (m + 1.0))
    return {'chunk': chunk,
            'cos': jnp.tile(cos_h, (1, RET_HEADS)), 'sin': jnp.tile(sin_h, (1, RET_HEADS)),
            'decay': decay,
            'kdec': jnp.repeat(kdec.T, RET_DK, axis=1),
            'cross': jnp.repeat(cross.T, RET_DV, axis=1),
            'sdec': jnp.broadcast_to(jnp.exp(log_gamma * n_real)[:, None], (RET_HEADS, RET_DV))}


def _merge_kernel(x_ref, gt_ref, ya_ref, yb_ref, yc_ref, *rest, head):
    if head:
        yh_ref, wa_ref, wb_ref, wc_ref, wo_ref, g_ref, o_ref, ycs_ref = rest
        tm = yc_ref.shape[0]

        @pl.when(pl.program_id(0) == 0)
        def _():
            ycs_ref[0:head, :] = yh_ref[...]
            ycs_ref[head:tm, :] = yc_ref[0:tm - head, :]

        @pl.when(pl.program_id(0) > 0)
        def _():
            ycs_ref[...] = yc_ref[...]

        yc = ycs_ref[...]
    else:
        wa_ref, wb_ref, wc_ref, wo_ref, g_ref, o_ref = rest
        yc = yc_ref[...]
    gate = lambda j: _sigmoid(gt_ref[:, j * D_MODEL:(j + 1) * D_MODEL].astype(F32))
    m = gate(0) * jnp.dot(ya_ref[...], wa_ref[...], preferred_element_type=F32)
    m = m + gate(1) * jnp.dot(yb_ref[...], wb_ref[...], preferred_element_type=F32)
    m = m + gate(2) * jnp.dot(yc, wc_ref[...], preferred_element_type=F32)
    mo = jnp.dot(m.astype(BF16), wo_ref[...], preferred_element_type=F32)
    o_ref[...] = x_ref[...] + _rms(mo, g_ref[...])


def _merge(x, z, ya, yb, yc, p, tm):
    rows = x.shape[0]
    row = lambda i: (i, 0)
    const = lambda i: (0, 0)
    if isinstance(yc, tuple):
        y_head, yc = yc
        head = y_head.shape[0]
        assert head % 16 == 0 and head < tm and head + yc.shape[0] == rows
        yc_specs = [pl.BlockSpec((pl.Element(tm), pl.Element(RET_V)),
                                 lambda i: (pl.multiple_of(jnp.maximum(i * tm - head, 0), 16), 0)),
                    pl.BlockSpec((head, RET_V), const)]
        yc_args = [yc, y_head]
        scratch = [pltpu.VMEM((tm, RET_V), BF16)]
    else:
        head = 0
        yc_specs = [pl.BlockSpec((tm, RET_V), row)]
        yc_args = [yc]
        scratch = []
    return pl.pallas_call(
        functools.partial(_merge_kernel, head=head),
        grid=(rows // tm,),
        in_specs=[pl.BlockSpec((tm, D_MODEL), row),
                  pl.BlockSpec((pl.Element(tm), pl.Element(N_BRANCH * D_MODEL)),
                               lambda i: (pl.multiple_of(i * tm, 16), Z_GATES)),
                  pl.BlockSpec((tm, S5_WIDTH), row),
                  pl.BlockSpec((tm, LRU_WIDTH), row)] + yc_specs + [
                  pl.BlockSpec((S5_WIDTH, D_MODEL), const),
                  pl.BlockSpec((LRU_WIDTH, D_MODEL), const),
                  pl.BlockSpec((RET_V, D_MODEL), const),
                  pl.BlockSpec((D_MODEL, D_MODEL), const),
                  pl.BlockSpec((1, D_MODEL), const)],
        out_specs=pl.BlockSpec((tm, D_MODEL), row),
        out_shape=jax.ShapeDtypeStruct((rows, D_MODEL), F32),
        scratch_shapes=scratch,
        compiler_params=_cparams(("arbitrary",)),
        name="branch_merge",
    )(x, z, ya, yb, *yc_args, p['w_branch_a'], p['w_branch_b'], p['w_branch_c'], p['w_out'], p['g_post_mix'])


def _ffn_kernel(x_ref, gpre_ref, wg_ref, wu_ref, cw_ref, cb_ref, wd_ref, gpost_ref, cache_ref,
                o_ref, cache_out, xn_ref, acc_ref, ext_ref, *, bb, n_chunks):
    i = pl.program_id(0)
    c = pl.program_id(1)
    rows = x_ref.shape[0]
    halo = (FFN_CONV - 1) * bb

    @pl.when(c == 0)
    def _():
        xn_ref[...] = _rms(x_ref[...], gpre_ref[...]).astype(BF16)

    @pl.when(i == 0)
    def _():
        ext_ref[c, 0:halo, :] = cache_ref[...]

    xn = xn_ref[...]
    gf = jnp.dot(xn, wg_ref[...], preferred_element_type=F32)
    uf = jnp.dot(xn, wu_ref[...], preferred_element_type=F32)
    ext_ref[c, halo:halo + rows, :] = gf
    gc = cb_ref[...] + cw_ref[0:1, :] * ext_ref[c, 0:rows, :]
    for j in range(1, FFN_CONV):
        gc = gc + cw_ref[j:j + 1, :] * ext_ref[c, j * bb:j * bb + rows, :]
    new_cache = ext_ref[c, rows:rows + halo, :]
    cache_out[c] = new_cache
    ext_ref[c, 0:halo, :] = new_cache
    act = (jax.nn.gelu(gc) * uf).astype(BF16)
    part = jnp.dot(act, wd_ref[...], preferred_element_type=F32)

    @pl.when(c == 0)
    def _():
        acc_ref[...] = part

    @pl.when(c > 0)
    def _():
        acc_ref[...] += part

    @pl.when(c == n_chunks - 1)
    def _():
        o_ref[...] = x_ref[...] + _rms(acc_ref[...], gpost_ref[...])


def _ffn(x, cache, p, bb, tm, tf):
    rows = x.shape[0]
    halo = (FFN_CONV - 1) * bb
    n_chunks = D_FF // tf
    kern = functools.partial(_ffn_kernel, bb=bb, n_chunks=n_chunks)
    return pl.pallas_call(
        kern,
        grid=(rows // tm, n_chunks),
        in_specs=[pl.BlockSpec((tm, D_MODEL), lambda i, c: (i, 0)),
                  pl.BlockSpec((1, D_MODEL), lambda i, c: (0, 0)),
                  pl.BlockSpec((D_MODEL, tf), lambda i, c: (0, c)),
                  pl.BlockSpec((D_MODEL, tf), lambda i, c: (0, c + n_chunks)),
                  pl.BlockSpec((FFN_CONV, tf), lambda i, c: (0, c)),
                  pl.BlockSpec((1, tf), lambda i, c: (0, c)),
                  pl.BlockSpec((tf, D_MODEL), lambda i, c: (c, 0)),
                  pl.BlockSpec((1, D_MODEL), lambda i, c: (0, 0)),
                  pl.BlockSpec((halo, tf), lambda i, c: (0, c))],
        out_specs=[pl.BlockSpec((tm, D_MODEL), lambda i, c: (i, 0)),
                   pl.BlockSpec((n_chunks, halo, tf), lambda i, c: (0, 0, 0))],
        out_shape=[jax.ShapeDtypeStruct((rows, D_MODEL), F32),
                   jax.ShapeDtypeStruct((n_chunks, halo, tf), F32)],
        scratch_shapes=[pltpu.VMEM((tm, D_MODEL), BF16),
                        pltpu.VMEM((tm, D_MODEL), F32),
                        pltpu.VMEM((n_chunks, tm + halo, tf), F32)],
        compiler_params=_cparams(("arbitrary", "arbitrary")),
        name="conv_ffn",
    )(x, p['g_pre_ffn'], p['ffn_w_up'], p['ffn_w_up'], p['ffn_conv_w'], p['ffn_conv_b'],
      p['ffn_w_down'], p['g_post_ffn'], cache)


def _to_tm_kernel(x_ref, meta_ref, o_ref):
    tt = o_ref.shape[0]

    @pl.when(pl.program_id(1) == 0)
    def _():
        o_ref[0:N_META, :] = meta_ref[...]
        o_ref[N_META:tt, :] = x_ref[0, 0:tt - N_META, :]

    @pl.when(pl.program_id(1) > 0)
    def _():
        o_ref[...] = x_ref[0]


def _relayout_steps(n, limit):
    best = None
    for tt in range(16, min(n, limit) + 1, 16):
        if n % tt == 0:
            best = tt
    return best


def _to_time_major(x_prompt, meta_tokens):
    bp, sp, d = x_prompt.shape
    lp = N_META + sp
    tt = _relayout_steps(lp, min(sp, 700))
    if tt is None or tt <= N_META:
        meta = jnp.broadcast_to(meta_tokens[:, None, :], (N_META, bp, d))
        return jnp.concatenate([meta, jnp.swapaxes(x_prompt, 0, 1)], axis=0).reshape(lp * bp, d)
    out = pl.pallas_call(
        _to_tm_kernel,
        grid=(bp, lp // tt),
        in_specs=[pl.BlockSpec((pl.Element(1), pl.Element(tt), pl.Element(d)),
                               lambda b, i: (b, pl.multiple_of(jnp.maximum(i * tt - N_META, 0), 16), 0)),
                  pl.BlockSpec((N_META, d), lambda b, i: (0, 0))],
        out_specs=pl.BlockSpec((tt, d), lambda b, i: (i, b)),
        out_shape=jax.ShapeDtypeStruct((lp, bp * d), x_prompt.dtype),
        compiler_params=_cparams(("arbitrary", "arbitrary")),
        name="to_time_major",
    )(x_prompt, meta_tokens)
    return out.reshape(lp * bp, d)


def _from_tm_kernel(x_ref, o_ref):
    o_ref[...] = x_ref[...]


def _from_time_major(y, bp, sp):
    d = y.shape[1]
    lp = N_META + sp
    tt = _relayout_steps(sp, 700)
    if tt is None:
        return jnp.swapaxes(y.reshape(lp, bp, d), 0, 1)[:, N_META:]
    return pl.pallas_call(
        _from_tm_kernel,
        grid=(bp, sp // tt),
        in_specs=[pl.BlockSpec((pl.Element(tt), pl.Element(d)),
                               lambda b, j: (pl.multiple_of(N_META + j * tt, 16), pl.multiple_of(b * d, 128)))],
        out_specs=pl.BlockSpec((None, tt, d), lambda b, j: (b, j, 0)),
        out_shape=jax.ShapeDtypeStruct((bp, sp, d), y.dtype),
        compiler_params=_cparams(("arbitrary", "arbitrary")),
        name="from_time_major",
    )(y.reshape(lp, bp * d))


def _block_diag(w, tiles):
    nb, r, c = w.shape
    per = nb // tiles
    w4 = w.reshape(tiles, per, r, c)
    eye = jnp.eye(per, dtype=w.dtype)
    return jnp.einsum('kgrc,gh->kgrhc', w4, eye).reshape(tiles, per * r, per * c)


def _prep_layer(P, l):
    row = lambda a: a[l].reshape(1, -1).astype(F32)
    p = {
        'g_pre_mix': row(P['g_pre_mix']), 'g_post_mix': row(P['g_post_mix']),
        'g_pre_ffn': row(P['g_pre_ffn']), 'g_post_ffn': row(P['g_post_ffn']),
        'w_in': P['w_in'][l].astype(BF16),
        's5_lre': row(P['s5_a_re']), 's5_lim': row(P['s5_a_im']),
        's5_lstep': jnp.repeat(P['s5_log_step'][l], S5_STATE).reshape(1, S5_N),
        's5_bre': _block_diag(jnp.swapaxes(P['s5_b_re'][l], 1, 2), S5_KT),
        's5_bim': _block_diag(jnp.swapaxes(P['s5_b_im'][l], 1, 2), S5_KT),
        's5_cre': _block_diag(jnp.swapaxes(P['s5_c_re'][l], 1, 2), S5_KT).astype(BF16),
        's5_cim': _block_diag(jnp.swapaxes(P['s5_c_im'][l], 1, 2), S5_KT).astype(BF16),
        's5_d': row(P['s5_d']), 's5_w_glu': P['s5_w_glu'][l].astype(BF16), 's5_b_glu': row(P['s5_b_glu']),
        'lru_conv_w': P['lru_conv_w'][l], 'lru_conv_b': row(P['lru_conv_b']),
        'lru_wg': jnp.concatenate([_block_diag(P['lru_w_a'][l], LRU_KT),
                                   _block_diag(P['lru_w_x'][l], LRU_KT)], axis=-1).astype(BF16),
        'lru_b_a': row(P['lru_b_a']), 'lru_b_x': row(P['lru_b_x']), 'lru_lam': row(P['lru_lam']),
        'ret_norm_g': row(P['ret_norm_g']),
        'w_branch_a': P['w_branch_a'][l].astype(BF16), 'w_branch_b': P['w_branch_b'][l].astype(BF16),
        'w_branch_c': P['w_branch_c'][l].astype(BF16), 'w_out': P['w_out'][l].astype(BF16),
        'ffn_w_up': P['ffn_w_up'][l].astype(BF16), 'ffn_conv_w': P['ffn_conv_w'][l],
        'ffn_conv_b': row(P['ffn_conv_b']), 'ffn_w_down': P['ffn_w_down'][l].astype(BF16),
    }
    p['s5_wb'], p['s5_are'], p['s5_aim'] = _s5_discretize(p)
    return p


def _run_group(x, states, params, bb, n_steps, retention, tiles):
    s5_re, s5_im, lru_h, lru_cache, ret_s, ffn_cache = states
    new = [[] for _ in range(6)]
    for l in range(DEPTH):
        p = params[l]
        z = _norm_matmul(x, p['g_pre_mix'], p['w_in'], tiles['tm_in'], tiles['tn_in'])
        ya, hre, him = _s5_mixer(z, s5_re[l], s5_im[l], p, bb, n_steps, tiles['rows_scan'])
        yb, hl, lcache = _lru_mixer(z, lru_h[l], lru_cache[l], p, bb, n_steps, tiles['rows_scan'])
        yc, rs = retention(z, ret_s[l], p['ret_norm_g'])
        x = _merge(x, z, ya, yb, yc, p, tiles['tm'])
        x, fcache = _ffn(x, ffn_cache[l], p, bb, tiles['tm'], tiles['tf'])
        fcache = jnp.swapaxes(fcache, 0, 1).reshape(-1, D_FF)
        for lst, val in zip(new, (hre, him, hl, lcache, rs, fcache)):
            lst.append(val)
    return x, new


def kernel(x_prompt, x_sample, state_s5_re, state_s5_im, state_lru, cache_lru_conv, state_ret, cache_ffn_conv, meta_tokens, g_pre_mix, g_post_mix, g_pre_ffn, g_post_ffn, w_in, s5_a_re, s5_a_im, s5_log_step, s5_b_re, s5_b_im, s5_c_re, s5_c_im, s5_d, s5_w_glu, s5_b_glu, lru_conv_w, lru_conv_b, lru_w_a, lru_b_a, lru_w_x, lru_b_x, lru_lam, ret_norm_g, w_branch_a, w_branch_b, w_branch_c, w_out, ffn_w_up, ffn_conv_w, ffn_conv_b, ffn_w_down):
    P = {'g_pre_mix': g_pre_mix, 'g_post_mix': g_post_mix, 'g_pre_ffn': g_pre_ffn, 'g_post_ffn': g_post_ffn,
         'w_in': w_in, 's5_a_re': s5_a_re, 's5_a_im': s5_a_im, 's5_log_step': s5_log_step,
         's5_b_re': s5_b_re, 's5_b_im': s5_b_im, 's5_c_re': s5_c_re, 's5_c_im': s5_c_im,
         's5_d': s5_d, 's5_w_glu': s5_w_glu, 's5_b_glu': s5_b_glu,
         'lru_conv_w': lru_conv_w, 'lru_conv_b': lru_conv_b, 'lru_w_a': lru_w_a, 'lru_b_a': lru_b_a,
         'lru_w_x': lru_w_x, 'lru_b_x': lru_b_x, 'lru_lam': lru_lam, 'ret_norm_g': ret_norm_g,
         'w_branch_a': w_branch_a, 'w_branch_b': w_branch_b, 'w_branch_c': w_branch_c, 'w_out': w_out,
         'ffn_w_up': ffn_w_up, 'ffn_conv_w': ffn_conv_w, 'ffn_conv_b': ffn_conv_b, 'ffn_w_down': ffn_w_down}
    params = [_prep_layer(P, l) for l in range(DEPTH)]

    bp, sp, _ = x_prompt.shape
    lp = N_META + sp
    xp = _to_time_major(x_prompt, meta_tokens.astype(x_prompt.dtype))
    zeros = lambda *s: [jnp.zeros(s, F32)] * DEPTH
    st_p = (zeros(bp, S5_N), zeros(bp, S5_N), zeros(bp, LRU_WIDTH),
            zeros((LRU_CONV - 1) * bp, LRU_WIDTH), zeros(bp, RET_HEADS, RET_DK, RET_DV),
            zeros((FFN_CONV - 1) * bp, D_FF))
    assert bp == SUBLANES and sp % RET_CHUNK == 0 and (N_META * bp) % 128 == 0
    tabs_meta = _ret_tables(np.arange(N_META), N_META, N_META)
    tabs_main = _ret_tables(N_META + np.arange(sp), RET_CHUNK, RET_CHUNK)

    def retention_p(z, s0, ng):
        y_meta, s1 = _ret_mixer_tm(z, s0, ng, tabs_meta, 0, 1)
        y_main, s2 = _ret_mixer_tm(z, s1, ng, tabs_main, N_META * bp, sp // RET_CHUNK)
        return (y_meta, y_main), s2

    tiles_p = {'tm_in': _pick_rows(lp, bp, 1400), 'tn_in': 1408, 'rows_scan': 700,
               'tm': _pick_rows(lp, bp, 700), 'tf': 1408}
    yp, new_p = _run_group(xp, st_p, params, bp, lp, retention_p, tiles_p)
    y_prompt = _from_time_major(yp, bp, sp)

    bs, ds, _ = x_sample.shape
    xs = jnp.swapaxes(x_sample, 0, 1).reshape(ds * bs, D_MODEL)
    tm_major = lambda c: jnp.swapaxes(c, 0, 1).reshape(-1, c.shape[-1])
    st_s = ([state_s5_re[l].reshape(bs, S5_N) for l in range(DEPTH)],
            [state_s5_im[l].reshape(bs, S5_N) for l in range(DEPTH)],
            [state_lru[l] for l in range(DEPTH)],
            [tm_major(cache_lru_conv[l]) for l in range(DEPTH)],
            list(range(DEPTH)),
            [tm_major(cache_ffn_conv[l]) for l in range(DEPTH)])
    chunk_s = 16
    pad_s = chunk_s - ds
    tabs_s = _ret_tables(np.concatenate([np.zeros(pad_s), PAST_LEN + np.arange(ds)]), chunk_s, ds)

    def retention_s(z, layer, ng):
        zr = z[:, Z_RET:Z_RET + Z_RET_W].reshape(ds, bs, Z_RET_W).transpose(1, 0, 2)
        zr = jnp.pad(zr, ((0, 0), (pad_s, 0), (0, 0)))
        yc, s1 = _ret_mixer_bm(zr, state_ret, layer, ng, tabs_s, SUBLANES)
        return yc[:, pad_s:].transpose(1, 0, 2).reshape(ds * bs, RET_V), s1

    tiles_s = {'tm_in': ds * bs, 'tn_in': 1408, 'rows_scan': ds * bs, 'tm': ds * bs, 'tf': 1408}
    ys, new_s = _run_group(xs, st_s, params, bs, ds, retention_s, tiles_s)
    y_sample = jnp.swapaxes(ys.reshape(ds, bs, D_MODEL), 0, 1)

    def pack(new, b):
        s5r, s5i, lh, lc, rs, fc = [jnp.stack(v) for v in new]
        return (s5r.reshape(DEPTH, b, S5_GROUPS, S5_STATE), s5i.reshape(DEPTH, b, S5_GROUPS, S5_STATE), lh,
                jnp.swapaxes(lc.reshape(DEPTH, LRU_CONV - 1, b, LRU_WIDTH), 1, 2), rs,
                jnp.swapaxes(fc.reshape(DEPTH, FFN_CONV - 1, b, D_FF), 1, 2))

    return (y_prompt, y_sample) + pack(new_p, bp) + pack(new_s, bs)
```

```python
import functools
import math

import jax
import jax.numpy as jnp
import numpy as np
from jax import lax
from jax.experimental import pallas as pl
from jax.experimental.pallas import tpu as pltpu

F32 = jnp.float32
BF16 = jnp.bfloat16

D_MODEL = 1024
DEPTH = 2
PAST_LEN = 16384
N_META = 16
S5_WIDTH = 768
S5_GROUP = 16
S5_GROUPS = S5_WIDTH // S5_GROUP
S5_STATE = 64
S5_N = S5_GROUPS * S5_STATE
LRU_WIDTH = 768
LRU_BLOCK = 64
LRU_BLOCKS = LRU_WIDTH // LRU_BLOCK
LRU_CONV = 4
LRU_C = 8.0
RET_HEADS = 8
RET_DK = 64
RET_DV = 128
RET_QK = RET_HEADS * RET_DK
RET_V = RET_HEADS * RET_DV
RET_CHUNK = 128
ROPE_BASE = 10000.0
N_BRANCH = 3
D_FF = 2816
FFN_CONV = 3
EPS = 1e-6
D_IN = S5_WIDTH + 2 * LRU_WIDTH + 2 * RET_QK + 2 * RET_V + N_BRANCH * D_MODEL

Z_UA = 0
Z_XB = Z_UA + S5_WIDTH
Z_GB = Z_XB + LRU_WIDTH
Z_RET = Z_GB + LRU_WIDTH
Z_RET_W = 2 * RET_QK + 2 * RET_V
Z_GATES = Z_RET + Z_RET_W
RQ, RK, RV, RG = 0, RET_QK, 2 * RET_QK, 2 * RET_QK + RET_V

MXU_K = 256
S5_KT = S5_WIDTH // MXU_K
S5_NT = S5_N // S5_KT
LRU_KT = LRU_WIDTH // MXU_K

SUBLANES = 8
VMEM_LIMIT = 56 * 1024 * 1024


def _cparams(sem):
    return pltpu.CompilerParams(dimension_semantics=sem, vmem_limit_bytes=VMEM_LIMIT)


def _pick_rows(n_steps, bb, target_rows):
    best = None
    for tt in range(1, n_steps + 1):
        if n_steps % tt:
            continue
        rows = tt * bb
        if rows % 16 == 0 and rows <= target_rows:
            best = rows
    if best is None:
        best = n_steps * bb
    return best


def _rms(x, g):
    return x * lax.rsqrt(jnp.mean(x * x, axis=-1, keepdims=True) + EPS) * g


def _sigmoid(x):
    return 0.5 * jnp.tanh(0.5 * x) + 0.5


def _norm_matmul_kernel(x_ref, g_ref, w_ref, o_ref, xn_ref):
    @pl.when(pl.program_id(1) == 0)
    def _():
        xn_ref[...] = _rms(x_ref[...], g_ref[...]).astype(BF16)

    o_ref[...] = jnp.dot(xn_ref[...], w_ref[...],
                         preferred_element_type=F32).astype(o_ref.dtype)


def _norm_matmul(x, g, w, tm, tn):
    rows, d = x.shape
    n = w.shape[1]
    return pl.pallas_call(
        _norm_matmul_kernel,
        grid=(rows // tm, n // tn),
        in_specs=[pl.BlockSpec((tm, d), lambda i, j: (i, 0)),
                  pl.BlockSpec((1, d), lambda i, j: (0, 0)),
                  pl.BlockSpec((d, tn), lambda i, j: (0, j))],
        out_specs=pl.BlockSpec((tm, tn), lambda i, j: (i, j)),
        out_shape=jax.ShapeDtypeStruct((rows, n), BF16),
        scratch_shapes=[pltpu.VMEM((tm, d), BF16)],
        compiler_params=_cparams(("arbitrary", "arbitrary")),
        name="norm_in_proj",
    )(x, g, w)


def _s5_disc_kernel(lre_ref, lim_ref, lstep_ref, bre_ref, bim_ref, wb_ref, are_ref, aim_ref):
    dt = jnp.exp(lstep_ref[...])
    lr = lre_ref[...]
    li = lim_ref[...]
    mag = jnp.exp(lr * dt)
    a_re = mag * jnp.cos(li * dt)
    a_im = mag * jnp.sin(li * dt)
    den = lr * lr + li * li
    nr = a_re - 1.0
    cr = (nr * lr + a_im * li) / den
    ci = (a_im * lr - nr * li) / den
    are_ref[...] = jnp.broadcast_to(a_re, are_ref.shape)
    aim_ref[...] = jnp.broadcast_to(a_im, aim_ref.shape)
    b_r = bre_ref[0]
    b_i = bim_ref[0]
    wb_ref[0, :, 0:S5_NT] = (cr * b_r - ci * b_i).astype(BF16)
    wb_ref[0, :, S5_NT:2 * S5_NT] = (cr * b_i + ci * b_r).astype(BF16)


def _s5_discretize(p):
    vec = pl.BlockSpec((1, S5_NT), lambda k: (0, k))
    blk = pl.BlockSpec((1, MXU_K, S5_NT), lambda k: (k, 0, 0))
    return pl.pallas_call(
        _s5_disc_kernel,
        grid=(S5_KT,),
        in_specs=[vec, vec, vec, blk, blk],
        out_specs=[pl.BlockSpec((1, MXU_K, 2 * S5_NT), lambda k: (k, 0, 0)),
                   pl.BlockSpec((SUBLANES, S5_NT), lambda k: (0, k)),
                   pl.BlockSpec((SUBLANES, S5_NT), lambda k: (0, k))],
        out_shape=[jax.ShapeDtypeStruct((S5_KT, MXU_K, 2 * S5_NT), BF16),
                   jax.ShapeDtypeStruct((SUBLANES, S5_N), F32),
                   jax.ShapeDtypeStruct((SUBLANES, S5_N), F32)],
        compiler_params=_cparams(("arbitrary",)),
        name="s5_discretize",
    )(p['s5_lre'], p['s5_lim'], p['s5_lstep'], p['s5_bre'], p['s5_bim'])


def _s5_kernel(u_ref, wb_ref, are_ref, aim_ref, cre_ref, cim_ref,
               d_ref, wglu_ref, bglu_ref, h0re_ref, h0im_ref,
               y_ref, hre_out, him_out,
               hre_ref, him_ref, sre_ref, sim_ref, *, bb, tc):
    @pl.when(pl.program_id(0) == 0)
    def _():
        hre_ref[...] = h0re_ref[...]
        him_ref[...] = h0im_ref[...]

    u = u_ref[...]
    for kt in range(S5_KT):
        r = jnp.dot(u[:, kt * MXU_K:(kt + 1) * MXU_K], wb_ref[kt], preferred_element_type=F32)
        sre_ref[:, kt * S5_NT:(kt + 1) * S5_NT] = r[:, 0:S5_NT]
        sim_ref[:, kt * S5_NT:(kt + 1) * S5_NT] = r[:, S5_NT:2 * S5_NT]

    def scan_tile(rt, carry):
        r0 = rt * SUBLANES
        for kt in range(S5_KT):
            sl = slice(kt * S5_NT, (kt + 1) * S5_NT)
            ar = are_ref[:, sl]
            ai = aim_ref[:, sl]
            hr0 = hre_ref[pl.ds(r0, SUBLANES), sl]
            hi0 = him_ref[pl.ds(r0, SUBLANES), sl]

            def step(t, h):
                hr, hi = h
                row = pl.multiple_of(t * bb + r0, SUBLANES)
                nr_ = ar * hr - ai * hi + sre_ref[pl.ds(row, SUBLANES), sl]
                ni_ = ar * hi + ai * hr + sim_ref[pl.ds(row, SUBLANES), sl]
                sre_ref[pl.ds(row, SUBLANES), sl] = nr_
                sim_ref[pl.ds(row, SUBLANES), sl] = ni_
                return nr_, ni_

            hr1, hi1 = lax.fori_loop(0, tc, step, (hr0, hi0), unroll=2 if tc % 2 == 0 else 1)
            hre_ref[pl.ds(r0, SUBLANES), sl] = hr1
            him_ref[pl.ds(r0, SUBLANES), sl] = hi1
        return carry

    if bb == SUBLANES:
        scan_tile(0, 0)
    else:
        lax.fori_loop(0, bb // SUBLANES, scan_tile, 0)

    hre_out[...] = hre_ref[...]
    him_out[...] = him_ref[...]

    ys = []
    for kt in range(S5_KT):
        sl = slice(kt * S5_NT, (kt + 1) * S5_NT)
        ys.append(jnp.dot(sre_ref[:, sl].astype(BF16), cre_ref[kt], preferred_element_type=F32)
                  - jnp.dot(sim_ref[:, sl].astype(BF16), cim_ref[kt], preferred_element_type=F32))
    y = jnp.concatenate(ys, axis=-1) + d_ref[...] * u.astype(F32)
    z = jax.nn.gelu(y)
    gate = jnp.dot(z.astype(BF16), wglu_ref[...], preferred_element_type=F32) + bglu_ref[...]
    y_ref[...] = (z * _sigmoid(gate)).astype(y_ref.dtype)


def _s5_mixer(z, h0_re, h0_im, p, bb, n_steps, target_rows):
    rows_total = z.shape[0]
    rows = _pick_rows(n_steps, bb, target_rows)
    tc = rows // bb
    const2 = lambda i: (0, 0)
    const3 = lambda i: (0, 0, 0)
    kern = functools.partial(_s5_kernel, bb=bb, tc=tc)
    return pl.pallas_call(
        kern,
        grid=(rows_total // rows,),
        in_specs=[pl.BlockSpec((rows, S5_WIDTH), lambda i: (i, Z_UA // S5_WIDTH)),
                  pl.BlockSpec((S5_KT, MXU_K, 2 * S5_NT), const3),
                  pl.BlockSpec((SUBLANES, S5_N), const2),
                  pl.BlockSpec((SUBLANES, S5_N), const2),
                  pl.BlockSpec((S5_KT, S5_NT, MXU_K), const3),
                  pl.BlockSpec((S5_KT, S5_NT, MXU_K), const3),
                  pl.BlockSpec((1, S5_WIDTH), const2),
                  pl.BlockSpec((S5_WIDTH, S5_WIDTH), const2),
                  pl.BlockSpec((1, S5_WIDTH), const2),
                  pl.BlockSpec((bb, S5_N), const2),
                  pl.BlockSpec((bb, S5_N), const2)],
        out_specs=[pl.BlockSpec((rows, S5_WIDTH), lambda i: (i, 0)),
                   pl.BlockSpec((bb, S5_N), const2),
                   pl.BlockSpec((bb, S5_N), const2)],
        out_shape=[jax.ShapeDtypeStruct((rows_total, S5_WIDTH), BF16),
                   jax.ShapeDtypeStruct((bb, S5_N), F32),
                   jax.ShapeDtypeStruct((bb, S5_N), F32)],
        scratch_shapes=[pltpu.VMEM((bb, S5_N), F32),
                        pltpu.VMEM((bb, S5_N), F32),
                        pltpu.VMEM((rows, S5_N), F32),
                        pltpu.VMEM((rows, S5_N), F32)],
        compiler_params=_cparams(("arbitrary",)),
        name="s5_mixer",
    )(z, p['s5_wb'], p['s5_are'], p['s5_aim'], p['s5_cre'], p['s5_cim'],
      p['s5_d'], p['s5_w_glu'], p['s5_b_glu'], h0_re, h0_im)


def _lru_kernel(xb_ref, gb_ref, cache_ref, cw_ref, cb_ref, wg_ref, ba_ref, bx_ref, lam_ref, h0_ref,
                y_ref, h_out, cache_out,
                ext_ref, a_ref, b_ref, h_ref, *, bb, tc):
    rows = tc * bb
    halo = (LRU_CONV - 1) * bb

    @pl.when(pl.program_id(0) == 0)
    def _():
        ext_ref[0:halo, :] = cache_ref[...]
        h_ref[...] = h0_ref[...]

    ext_ref[halo:halo + rows, :] = xb_ref[...].astype(F32)
    xc = cb_ref[...] + cw_ref[0:1, :] * ext_ref[0:rows, :]
    for j in range(1, LRU_CONV):
        xc = xc + cw_ref[j:j + 1, :] * ext_ref[j * bb:j * bb + rows, :]
    new_cache = ext_ref[rows:rows + halo, :]
    cache_out[...] = new_cache
    ext_ref[0:halo, :] = new_cache

    xcb = xc.astype(BF16)
    rs, is_ = [], []
    for kt in range(LRU_KT):
        g = jnp.dot(xcb[:, kt * MXU_K:(kt + 1) * MXU_K], wg_ref[kt], preferred_element_type=F32)
        rs.append(g[:, 0:MXU_K])
        is_.append(g[:, MXU_K:2 * MXU_K])
    r = _sigmoid(jnp.concatenate(rs, axis=-1) + ba_ref[...])
    ig = _sigmoid(jnp.concatenate(is_, axis=-1) + bx_ref[...])
    nl = -lam_ref[...]
    softplus = jnp.maximum(nl, 0.0) + jnp.log1p(jnp.exp(-jnp.abs(nl)))
    log_a = (-LRU_C) * r * softplus
    a = jnp.exp(log_a)
    mult = jnp.sqrt(-jnp.tanh(log_a) * (a * a + 1.0))
    a_ref[...] = a
    b_ref[...] = mult * (ig * xc)

    def scan_tile(rt, carry):
        r0 = rt * SUBLANES
        h0 = h_ref[pl.ds(r0, SUBLANES), :]

        def step(t, h):
            row = pl.multiple_of(t * bb + r0, SUBLANES)
            h = a_ref[pl.ds(row, SUBLANES), :] * h + b_ref[pl.ds(row, SUBLANES), :]
            b_ref[pl.ds(row, SUBLANES), :] = h
            return h

        h1 = lax.fori_loop(0, tc, step, h0, unroll=2 if tc % 2 == 0 else 1)
        h_ref[pl.ds(r0, SUBLANES), :] = h1
        return carry

    if bb == SUBLANES:
        scan_tile(0, 0)
    else:
        lax.fori_loop(0, bb // SUBLANES, scan_tile, 0)

    h_out[...] = h_ref[...]
    y_ref[...] = (b_ref[...] * jax.nn.gelu(gb_ref[...].astype(F32))).astype(y_ref.dtype)


def _lru_mixer(z, h0, cache, p, bb, n_steps, target_rows):
    rows_total = z.shape[0]
    rows = _pick_rows(n_steps, bb, target_rows)
    tc = rows // bb
    halo = (LRU_CONV - 1) * bb
    const2 = lambda i: (0, 0)
    vec = pl.BlockSpec((1, LRU_WIDTH), const2)
    kern = functools.partial(_lru_kernel, bb=bb, tc=tc)
    return pl.pallas_call(
        kern,
        grid=(rows_total // rows,),
        in_specs=[pl.BlockSpec((rows, LRU_WIDTH), lambda i: (i, Z_XB // LRU_WIDTH)),
                  pl.BlockSpec((rows, LRU_WIDTH), lambda i: (i, Z_GB // LRU_WIDTH)),
                  pl.BlockSpec((halo, LRU_WIDTH), const2),
                  pl.BlockSpec((LRU_CONV, LRU_WIDTH), const2),
                  vec,
                  pl.BlockSpec((LRU_KT, MXU_K, 2 * MXU_K), lambda i: (0, 0, 0)),
                  vec, vec, vec,
                  pl.BlockSpec((bb, LRU_WIDTH), const2)],
        out_specs=[pl.BlockSpec((rows, LRU_WIDTH), lambda i: (i, 0)),
                   pl.BlockSpec((bb, LRU_WIDTH), const2),
                   pl.BlockSpec((halo, LRU_WIDTH), const2)],
        out_shape=[jax.ShapeDtypeStruct((rows_total, LRU_WIDTH), BF16),
                   jax.ShapeDtypeStruct((bb, LRU_WIDTH), F32),
                   jax.ShapeDtypeStruct((halo, LRU_WIDTH), F32)],
        scratch_shapes=[pltpu.VMEM((rows + halo, LRU_WIDTH), F32),
                        pltpu.VMEM((rows, LRU_WIDTH), F32),
                        pltpu.VMEM((rows, LRU_WIDTH), F32),
                        pltpu.VMEM((bb, LRU_WIDTH), F32)],
        compiler_params=_cparams(("arbitrary",)),
        name="rglru_mixer",
    )(z, z, cache, p['lru_conv_w'], p['lru_conv_b'], p['lru_wg'], p['lru_b_a'], p['lru_b_x'],
      p['lru_lam'], h0)


def _ret_sequence(q, k, v, state, cos, sin, decay_ref, kdec, cross, sdec, ng):
    half = RET_DK // 2
    lane = lax.broadcasted_iota(jnp.int32, q.shape, 1)
    first = (lane % RET_DK) < half
    mean = lambda x: jnp.mean(x, axis=-1, keepdims=True)

    def rope(x):
        swapped = jnp.where(first, pltpu.roll(x, RET_QK - half, 1), pltpu.roll(x, half, 1))
        return x * cos + swapped * sin

    qr = rope(q).astype(BF16)
    kr = rope(k) * (RET_DK ** -0.5)
    kd = (kr * kdec).astype(BF16)
    kr = kr.astype(BF16)
    outs, new_states = [], []
    for h in range(RET_HEADS):
        qh = qr[:, h * RET_DK:(h + 1) * RET_DK]
        kh = kr[:, h * RET_DK:(h + 1) * RET_DK]
        kdh = kd[:, h * RET_DK:(h + 1) * RET_DK]
        vh = v[:, h * RET_DV:(h + 1) * RET_DV]
        s = state(h)
        sc = lax.dot_general(qh, kh, (((1,), (1,)), ((), ())), preferred_element_type=F32)
        sc = (sc * decay_ref[h]).astype(BF16)
        o = jnp.dot(sc, vh, preferred_element_type=F32)
        o = o + jnp.dot(qh, s.astype(BF16), preferred_element_type=F32) * cross[:, h * RET_DV:(h + 1) * RET_DV]
        new_states.append(sdec[h:h + 1, :] * s + lax.dot_general(kdh, vh, (((0,), (0,)), ((), ())),
                                                                 preferred_element_type=F32))
        oc = o - mean(o)
        outs.append(oc * lax.rsqrt(mean(oc * oc) + EPS) * ng[:, h * RET_DV:(h + 1) * RET_DV])
    return jnp.concatenate(outs, axis=-1), new_states


def _ret_bm_kernel(z_ref, cos_ref, sin_ref, decay_ref, kdec_ref, cross_ref, sdec_ref, ng_ref, s0_ref,
                   y_ref, s_out):
    cos, sin, kdec, cross, sdec, ng = (cos_ref[...], sin_ref[...], kdec_ref[...], cross_ref[...],
                                       sdec_ref[...], ng_ref[...])

    def one(i, carry):
        blk = z_ref[i]
        v = blk[:, RV:RV + RET_V]
        g = blk[:, RG:RG + RET_V].astype(F32)
        q = blk[:, RQ:RQ + RET_QK].astype(F32)
        k = blk[:, RK:RK + RET_QK].astype(F32)
        on, new_states = _ret_sequence(q, k, v, lambda h: s0_ref[i, h], cos, sin, decay_ref, kdec, cross,
                                       sdec, ng)
        for h in range(RET_HEADS):
            s_out[i, h] = new_states[h]
        y_ref[i] = (g * _sigmoid(g) * on).astype(y_ref.dtype)
        return carry

    lax.fori_loop(0, z_ref.shape[0], one, 0, unroll=True)


def _ret_mixer_bm(zr, s0_layers, layer, norm_g, tabs, group):
    nb, c, _ = zr.shape
    const2 = lambda i: (0, 0)
    return pl.pallas_call(
        _ret_bm_kernel,
        grid=(nb // group,),
        in_specs=[pl.BlockSpec((group, c, Z_RET_W), lambda i: (i, 0, 0)),
                  pl.BlockSpec((c, RET_QK), const2),
                  pl.BlockSpec((c, RET_QK), const2),
                  pl.BlockSpec((RET_HEADS, c, c), lambda i: (0, 0, 0)),
                  pl.BlockSpec((c, RET_QK), const2),
                  pl.BlockSpec((c, RET_V), const2),
                  pl.BlockSpec((RET_HEADS, RET_DV), const2),
                  pl.BlockSpec((1, RET_V), const2),
                  pl.BlockSpec((None, group, RET_HEADS, RET_DK, RET_DV), lambda i: (layer, i, 0, 0, 0))],
        out_specs=[pl.BlockSpec((group, c, RET_V), lambda i: (i, 0, 0)),
                   pl.BlockSpec((group, RET_HEADS, RET_DK, RET_DV), lambda i: (i, 0, 0, 0))],
        out_shape=[jax.ShapeDtypeStruct((nb, c, RET_V), BF16),
                   jax.ShapeDtypeStruct((nb, RET_HEADS, RET_DK, RET_DV), F32)],
        compiler_params=_cparams(("arbitrary",)),
        name="retention_step",
    )(zr, tabs['cos'], tabs['sin'], tabs['decay'], tabs['kdec'], tabs['cross'], tabs['sdec'], norm_g,
      s0_layers)


RET_V_SLABS = RET_V // 128
RET_QK_SLABS = RET_QK // 128
RET_IN_SLABS = RET_V_SLABS + 2 * RET_QK_SLABS


def _ret_tm_kernel(*refs, steps):
    z_ref, cos_ref, sin_ref, decay_ref, kdec_ref, cross_ref, sdec_ref, ng_ref, s0_ref = refs[:9]
    y_ref, s_out, in_slab, out_slab = refs[-4:]
    nseq = SUBLANES

    @pl.when(pl.program_id(0) == 0)
    def _():
        s_out[...] = s0_ref[...]

    def fill(lo, n, x):
        for s in range(n):
            in_slab[lo + s] = x[:, s * 128:(s + 1) * 128].astype(F32)

    nv, nq = RET_V_SLABS, RET_QK_SLABS
    fill(0, nv, z_ref[:, RV:RV + RET_V])
    fill(nv, nq, z_ref[:, RQ:RQ + RET_QK])
    fill(nv + nq, nq, z_ref[:, RK:RK + RET_QK])
    cos, sin, kdec, cross, sdec, ng = (cos_ref[...], sin_ref[...], kdec_ref[...], cross_ref[...],
                                       sdec_ref[...], ng_ref[...])

    def one(b, carry):
        rows = pl.ds(b, steps, stride=nseq)
        gather = lambda lo, n: jnp.concatenate([in_slab[lo + s, rows, :] for s in range(n)], axis=-1)
        v = gather(0, nv).astype(BF16)
        q = gather(nv, nq)
        k = gather(nv + nq, nq)
        on, new_states = _ret_sequence(q, k, v, lambda h: s_out[b, h], cos, sin, decay_ref, kdec, cross,
                                       sdec, ng)
        for h in range(RET_HEADS):
            s_out[b, h] = new_states[h]
        for s in range(nv):
            out_slab[s, rows, :] = on[:, s * 128:(s + 1) * 128]
        return carry

    lax.fori_loop(0, nseq, one, 0)
    g = z_ref[:, RG:RG + RET_V].astype(F32)
    on = jnp.concatenate([out_slab[s] for s in range(nv)], axis=-1)
    y_ref[...] = (g * _sigmoid(g) * on).astype(y_ref.dtype)


def _ret_mixer_tm(z, s0, norm_g, tabs, row0, n_chunks):
    c = tabs['chunk']
    rows = c * SUBLANES
    row_at = lambda i: pl.multiple_of(row0 + rows * i, 128)
    const2 = lambda i: (0, 0)
    in_specs = [pl.BlockSpec((pl.Element(rows), pl.Element(Z_RET_W)), lambda i: (row_at(i), Z_RET)),
                pl.BlockSpec((c, RET_QK), lambda i: (i, 0)),
                pl.BlockSpec((c, RET_QK), lambda i: (i, 0)),
                pl.BlockSpec((RET_HEADS, c, c), lambda i: (0, 0, 0)),
                pl.BlockSpec((c, RET_QK), const2),
                pl.BlockSpec((c, RET_V), const2),
                pl.BlockSpec((RET_HEADS, RET_DV), const2),
                pl.BlockSpec((1, RET_V), const2),
                pl.BlockSpec((SUBLANES, RET_HEADS, RET_DK, RET_DV), lambda i: (0, 0, 0, 0))]
    args = [z, tabs['cos'], tabs['sin'], tabs['decay'], tabs['kdec'], tabs['cross'], tabs['sdec'], norm_g, s0]
    return pl.pallas_call(
        functools.partial(_ret_tm_kernel, steps=c),
        grid=(n_chunks,),
        in_specs=in_specs,
        out_specs=[pl.BlockSpec((rows, RET_V), lambda i: (i, 0)),
                   pl.BlockSpec((SUBLANES, RET_HEADS, RET_DK, RET_DV), lambda i: (0, 0, 0, 0))],
        out_shape=[jax.ShapeDtypeStruct((n_chunks * rows, RET_V), BF16),
                   jax.ShapeDtypeStruct((SUBLANES, RET_HEADS, RET_DK, RET_DV), F32)],
        scratch_shapes=[pltpu.VMEM((RET_IN_SLABS, rows, 128), F32),
                        pltpu.VMEM((RET_V_SLABS, rows, 128), F32)],
        compiler_params=_cparams(("arbitrary",)),
        name="retention_chunks",
    )(*args)


def _ret_tables(pos, chunk, n_real):
    half = RET_DK // 2
    inv = np.power(ROPE_BASE, -np.arange(half, dtype=np.float64) / half)
    ang = np.asarray(pos, np.float64)[:, None] * inv
    cos_h = np.concatenate([np.cos(ang), np.cos(ang)], axis=-1)
    sin_h = np.concatenate([-np.sin(ang), np.sin(ang)], axis=-1)
    log_gamma = np.log1p(-np.exp2(-5.0 - np.arange(RET_HEADS, dtype=np.float64)))
    n = np.arange(chunk, dtype=np.float64)
    diff = n[:, None] - n[None, :]
    decay = np.where(diff >= 0, np.exp(log_gamma[:, None, None] * np.maximum(diff, 0.0)), 0.0)
    kdec = np.exp(log_gamma[:, None] * (chunk - 1.0 - n))
    m = np.maximum(n - (chunk - n_real), 0.0)
    cross = np.exp(log_gamma[:, None] * (m + 1.0))
    const = lambda a: jnp.asarray(np.ascontiguousarray(a), F32)
    return {'chunk': chunk,
            'cos': const(np.tile(cos_h, (1, RET_HEADS))), 'sin': const(np.tile(sin_h, (1, RET_HEADS))),
            'decay': const(decay),
            'kdec': const(np.repeat(kdec.T, RET_DK, axis=1)),
            'cross': const(np.repeat(cross.T, RET_DV, axis=1)),
            'sdec': const(np.broadcast_to(np.exp(log_gamma * n_real)[:, None], (RET_HEADS, RET_DV)))}


def _merge_kernel(x_ref, gt_ref, ya_ref, yb_ref, yc_ref, *rest, head):
    if head:
        yh_ref, wa_ref, wb_ref, wc_ref, wo_ref, g_ref, o_ref, ycs_ref = rest
        tm = yc_ref.shape[0]

        @pl.when(pl.program_id(0) == 0)
        def _():
            ycs_ref[0:head, :] = yh_ref[...]
            ycs_ref[head:tm, :] = yc_ref[0:tm - head, :]

        @pl.when(pl.program_id(0) > 0)
        def _():
            ycs_ref[...] = yc_ref[...]

        yc = ycs_ref[...]
    else:
        wa_ref, wb_ref, wc_ref, wo_ref, g_ref, o_ref = rest
        yc = yc_ref[...]
    gate = lambda j: _sigmoid(gt_ref[:, j * D_MODEL:(j + 1) * D_MODEL].astype(F32))
    m = gate(0) * jnp.dot(ya_ref[...], wa_ref[...], preferred_element_type=F32)
    m = m + gate(1) * jnp.dot(yb_ref[...], wb_ref[...], preferred_element_type=F32)
    m = m + gate(2) * jnp.dot(yc, wc_ref[...], preferred_element_type=F32)
    mo = jnp.dot(m.astype(BF16), wo_ref[...], preferred_element_type=F32)
    o_ref[...] = x_ref[...] + _rms(mo, g_ref[...])


def _merge(x, z, ya, yb, yc, p, tm):
    rows = x.shape[0]
    row = lambda i: (i, 0)
    const = lambda i: (0, 0)
    if isinstance(yc, tuple):
        y_head, yc = yc
        head = y_head.shape[0]
        assert head % 16 == 0 and head < tm and head + yc.shape[0] == rows
        yc_specs = [pl.BlockSpec((pl.Element(tm), pl.Element(RET_V)),
                                 lambda i: (pl.multiple_of(jnp.maximum(i * tm - head, 0), 16), 0)),
                    pl.BlockSpec((head, RET_V), const)]
        yc_args = [yc, y_head]
        scratch = [pltpu.VMEM((tm, RET_V), BF16)]
    else:
        head = 0
        yc_specs = [pl.BlockSpec((tm, RET_V), row)]
        yc_args = [yc]
        scratch = []
    return pl.pallas_call(
        functools.partial(_merge_kernel, head=head),
        grid=(rows // tm,),
        in_specs=[pl.BlockSpec((tm, D_MODEL), row),
                  pl.BlockSpec((pl.Element(tm), pl.Element(N_BRANCH * D_MODEL)),
                               lambda i: (pl.multiple_of(i * tm, 16), Z_GATES)),
                  pl.BlockSpec((tm, S5_WIDTH), row),
                  pl.BlockSpec((tm, LRU_WIDTH), row)] + yc_specs + [
                  pl.BlockSpec((S5_WIDTH, D_MODEL), const),
                  pl.BlockSpec((LRU_WIDTH, D_MODEL), const),
                  pl.BlockSpec((RET_V, D_MODEL), const),
                  pl.BlockSpec((D_MODEL, D_MODEL), const),
                  pl.BlockSpec((1, D_MODEL), const)],
        out_specs=pl.BlockSpec((tm, D_MODEL), row),
        out_shape=jax.ShapeDtypeStruct((rows, D_MODEL), F32),
        scratch_shapes=scratch,
        compiler_params=_cparams(("arbitrary",)),
        name="branch_merge",
    )(x, z, ya, yb, *yc_args, p['w_branch_a'], p['w_branch_b'], p['w_branch_c'], p['w_out'], p['g_post_mix'])


def _ffn_kernel(x_ref, gpre_ref, wg_ref, wu_ref, cw_ref, cb_ref, wd_ref, gpost_ref, cache_ref,
                o_ref, cache_out, xn_ref, acc_ref, ext_ref, *, bb, n_chunks):
    i = pl.program_id(0)
    c = pl.program_id(1)
    rows = x_ref.shape[0]
    halo = (FFN_CONV - 1) * bb

    @pl.when(c == 0)
    def _():
        xn_ref[...] = _rms(x_ref[...], gpre_ref[...]).astype(BF16)

    @pl.when(i == 0)
    def _():
        ext_ref[c, 0:halo, :] = cache_ref[...]

    xn = xn_ref[...]
    gf = jnp.dot(xn, wg_ref[...], preferred_element_type=F32)
    uf = jnp.dot(xn, wu_ref[...], preferred_element_type=F32)
    ext_ref[c, halo:halo + rows, :] = gf
    gc = cb_ref[...] + cw_ref[0:1, :] * ext_ref[c, 0:rows, :]
    for j in range(1, FFN_CONV):
        gc = gc + cw_ref[j:j + 1, :] * ext_ref[c, j * bb:j * bb + rows, :]
    new_cache = ext_ref[c, rows:rows + halo, :]
    cache_out[c] = new_cache
    ext_ref[c, 0:halo, :] = new_cache
    act = (jax.nn.gelu(gc) * uf).astype(BF16)
    part = jnp.dot(act, wd_ref[...], preferred_element_type=F32)

    @pl.when(c == 0)
    def _():
        acc_ref[...] = part

    @pl.when(c > 0)
    def _():
        acc_ref[...] += part

    @pl.when(c == n_chunks - 1)
    def _():
        o_ref[...] = x_ref[...] + _rms(acc_ref[...], gpost_ref[...])


def _ffn(x, cache, p, bb, tm, tf):
    rows = x.shape[0]
    halo = (FFN_CONV - 1) * bb
    n_chunks = D_FF // tf
    kern = functools.partial(_ffn_kernel, bb=bb, n_chunks=n_chunks)
    return pl.pallas_call(
        kern,
        grid=(rows // tm, n_chunks),
        in_specs=[pl.BlockSpec((tm, D_MODEL), lambda i, c: (i, 0)),
                  pl.BlockSpec((1, D_MODEL), lambda i, c: (0, 0)),
                  pl.BlockSpec((D_MODEL, tf), lambda i, c: (0, c)),
                  pl.BlockSpec((D_MODEL, tf), lambda i, c: (0, c + n_chunks)),
                  pl.BlockSpec((FFN_CONV, tf), lambda i, c: (0, c)),
                  pl.BlockSpec((1, tf), lambda i, c: (0, c)),
                  pl.BlockSpec((tf, D_MODEL), lambda i, c: (c, 0)),
                  pl.BlockSpec((1, D_MODEL), lambda i, c: (0, 0)),
                  pl.BlockSpec((halo, tf), lambda i, c: (0, c))],
        out_specs=[pl.BlockSpec((tm, D_MODEL), lambda i, c: (i, 0)),
                   pl.BlockSpec((n_chunks, halo, tf), lambda i, c: (0, 0, 0))],
        out_shape=[jax.ShapeDtypeStruct((rows, D_MODEL), F32),
                   jax.ShapeDtypeStruct((n_chunks, halo, tf), F32)],
        scratch_shapes=[pltpu.VMEM((tm, D_MODEL), BF16),
                        pltpu.VMEM((tm, D_MODEL), F32),
                        pltpu.VMEM((n_chunks, tm + halo, tf), F32)],
        compiler_params=_cparams(("arbitrary", "arbitrary")),
        name="conv_ffn",
    )(x, p['g_pre_ffn'], p['ffn_w_up'], p['ffn_w_up'], p['ffn_conv_w'], p['ffn_conv_b'],
      p['ffn_w_down'], p['g_post_ffn'], cache)


def _block_diag(w, tiles):
    nb, r, c = w.shape
    per = nb // tiles
    w4 = w.reshape(tiles, per, r, c)
    eye = jnp.eye(per, dtype=w.dtype)
    return jnp.einsum('kgrc,gh->kgrhc', w4, eye).reshape(tiles, per * r, per * c)


def _prep_layer(P, l):
    row = lambda a: a[l].reshape(1, -1).astype(F32)
    p = {
        'g_pre_mix': row(P['g_pre_mix']), 'g_post_mix': row(P['g_post_mix']),
        'g_pre_ffn': row(P['g_pre_ffn']), 'g_post_ffn': row(P['g_post_ffn']),
        'w_in': P['w_in'][l].astype(BF16),
        's5_lre': row(P['s5_a_re']), 's5_lim': row(P['s5_a_im']),
        's5_lstep': jnp.repeat(P['s5_log_step'][l], S5_STATE).reshape(1, S5_N),
        's5_bre': _block_diag(jnp.swapaxes(P['s5_b_re'][l], 1, 2), S5_KT),
        's5_bim': _block_diag(jnp.swapaxes(P['s5_b_im'][l], 1, 2), S5_KT),
        's5_cre': _block_diag(jnp.swapaxes(P['s5_c_re'][l], 1, 2), S5_KT).astype(BF16),
        's5_cim': _block_diag(jnp.swapaxes(P['s5_c_im'][l], 1, 2), S5_KT).astype(BF16),
        's5_d': row(P['s5_d']), 's5_w_glu': P['s5_w_glu'][l].astype(BF16), 's5_b_glu': row(P['s5_b_glu']),
        'lru_conv_w': P['lru_conv_w'][l], 'lru_conv_b': row(P['lru_conv_b']),
        'lru_wg': jnp.concatenate([_block_diag(P['lru_w_a'][l], LRU_KT),
                                   _block_diag(P['lru_w_x'][l], LRU_KT)], axis=-1).astype(BF16),
        'lru_b_a': row(P['lru_b_a']), 'lru_b_x': row(P['lru_b_x']), 'lru_lam': row(P['lru_lam']),
        'ret_norm_g': row(P['ret_norm_g']),
        'w_branch_a': P['w_branch_a'][l].astype(BF16), 'w_branch_b': P['w_branch_b'][l].astype(BF16),
        'w_branch_c': P['w_branch_c'][l].astype(BF16), 'w_out': P['w_out'][l].astype(BF16),
        'ffn_w_up': P['ffn_w_up'][l].astype(BF16), 'ffn_conv_w': P['ffn_conv_w'][l],
        'ffn_conv_b': row(P['ffn_conv_b']), 'ffn_w_down': P['ffn_w_down'][l].astype(BF16),
    }
    p['s5_wb'], p['s5_are'], p['s5_aim'] = _s5_discretize(p)
    return p


def _run_group(x, states, params, bb, n_steps, retention, tiles):
    s5_re, s5_im, lru_h, lru_cache, ret_s, ffn_cache = states
    new = [[] for _ in range(6)]
    for l in range(DEPTH):
        p = params[l]
        z = _norm_matmul(x, p['g_pre_mix'], p['w_in'], tiles['tm_in'], tiles['tn_in'])
        ya, hre, him = _s5_mixer(z, s5_re[l], s5_im[l], p, bb, n_steps, tiles['rows_scan'])
        yb, hl, lcache = _lru_mixer(z, lru_h[l], lru_cache[l], p, bb, n_steps, tiles['rows_scan'])
        yc, rs = retention(z, ret_s[l], p['ret_norm_g'])
        x = _merge(x, z, ya, yb, yc, p, tiles['tm'])
        x, fcache = _ffn(x, ffn_cache[l], p, bb, tiles['tm'], tiles['tf'])
        fcache = jnp.swapaxes(fcache, 0, 1).reshape(-1, D_FF)
        for lst, val in zip(new, (hre, him, hl, lcache, rs, fcache)):
            lst.append(val)
    return x, new


def kernel(x_prompt, x_sample, state_s5_re, state_s5_im, state_lru, cache_lru_conv, state_ret, cache_ffn_conv, meta_tokens, g_pre_mix, g_post_mix, g_pre_ffn, g_post_ffn, w_in, s5_a_re, s5_a_im, s5_log_step, s5_b_re, s5_b_im, s5_c_re, s5_c_im, s5_d, s5_w_glu, s5_b_glu, lru_conv_w, lru_conv_b, lru_w_a, lru_b_a, lru_w_x, lru_b_x, lru_lam, ret_norm_g, w_branch_a, w_branch_b, w_branch_c, w_out, ffn_w_up, ffn_conv_w, ffn_conv_b, ffn_w_down):
    P = {'g_pre_mix': g_pre_mix, 'g_post_mix': g_post_mix, 'g_pre_ffn': g_pre_ffn, 'g_post_ffn': g_post_ffn,
         'w_in': w_in, 's5_a_re': s5_a_re, 's5_a_im': s5_a_im, 's5_log_step': s5_log_step,
         's5_b_re': s5_b_re, 's5_b_im': s5_b_im, 's5_c_re': s5_c_re, 's5_c_im': s5_c_im,
         's5_d': s5_d, 's5_w_glu': s5_w_glu, 's5_b_glu': s5_b_glu,
         'lru_conv_w': lru_conv_w, 'lru_conv_b': lru_conv_b, 'lru_w_a': lru_w_a, 'lru_b_a': lru_b_a,
         'lru_w_x': lru_w_x, 'lru_b_x': lru_b_x, 'lru_lam': lru_lam, 'ret_norm_g': ret_norm_g,
         'w_branch_a': w_branch_a, 'w_branch_b': w_branch_b, 'w_branch_c': w_branch_c, 'w_out': w_out,
         'ffn_w_up': ffn_w_up, 'ffn_conv_w': ffn_conv_w, 'ffn_conv_b': ffn_conv_b, 'ffn_w_down': ffn_w_down}
    params = [_prep_layer(P, l) for l in range(DEPTH)]

    bp, sp, _ = x_prompt.shape
    lp = N_META + sp
    meta = jnp.broadcast_to(meta_tokens.astype(F32)[:, None, :], (N_META, bp, D_MODEL))
    xp = jnp.concatenate([meta, jnp.swapaxes(x_prompt, 0, 1)], axis=0).reshape(lp * bp, D_MODEL)
    zeros = lambda *s: [jnp.zeros(s, F32)] * DEPTH
    st_p = (zeros(bp, S5_N), zeros(bp, S5_N), zeros(bp, LRU_WIDTH),
            zeros((LRU_CONV - 1) * bp, LRU_WIDTH), zeros(bp, RET_HEADS, RET_DK, RET_DV),
            zeros((FFN_CONV - 1) * bp, D_FF))
    assert bp == SUBLANES and sp % RET_CHUNK == 0 and (N_META * bp) % 128 == 0
    tabs_meta = _ret_tables(np.arange(N_META), N_META, N_META)
    tabs_main = _ret_tables(N_META + np.arange(sp), RET_CHUNK, RET_CHUNK)

    def retention_p(z, s0, ng):
        y_meta, s1 = _ret_mixer_tm(z, s0, ng, tabs_meta, 0, 1)
        y_main, s2 = _ret_mixer_tm(z, s1, ng, tabs_main, N_META * bp, sp // RET_CHUNK)
        return (y_meta, y_main), s2

    tiles_p = {'tm_in': _pick_rows(lp, bp, 1400), 'tn_in': 1408, 'rows_scan': 700,
               'tm': _pick_rows(lp, bp, 700), 'tf': 1408}
    yp, new_p = _run_group(xp, st_p, params, bp, lp, retention_p, tiles_p)
    y_prompt = jnp.swapaxes(yp.reshape(lp, bp, D_MODEL), 0, 1)[:, N_META:]

    bs, ds, _ = x_sample.shape
    xs = jnp.swapaxes(x_sample, 0, 1).reshape(ds * bs, D_MODEL)
    tm_major = lambda c: jnp.swapaxes(c, 0, 1).reshape(-1, c.shape[-1])
    st_s = ([state_s5_re[l].reshape(bs, S5_N) for l in range(DEPTH)],
            [state_s5_im[l].reshape(bs, S5_N) for l in range(DEPTH)],
            [state_lru[l] for l in range(DEPTH)],
            [tm_major(cache_lru_conv[l]) for l in range(DEPTH)],
            list(range(DEPTH)),
            [tm_major(cache_ffn_conv[l]) for l in range(DEPTH)])
    chunk_s = 16
    pad_s = chunk_s - ds
    tabs_s = _ret_tables(np.concatenate([np.zeros(pad_s), PAST_LEN + np.arange(ds)]), chunk_s, ds)

    def retention_s(z, layer, ng):
        zr = z[:, Z_RET:Z_RET + Z_RET_W].reshape(ds, bs, Z_RET_W).transpose(1, 0, 2)
        zr = jnp.pad(zr, ((0, 0), (pad_s, 0), (0, 0)))
        yc, s1 = _ret_mixer_bm(zr, state_ret, layer, ng, tabs_s, SUBLANES)
        return yc[:, pad_s:].transpose(1, 0, 2).reshape(ds * bs, RET_V), s1

    tiles_s = {'tm_in': ds * bs, 'tn_in': 1408, 'rows_scan': ds * bs, 'tm': ds * bs, 'tf': 1408}
    ys, new_s = _run_group(xs, st_s, params, bs, ds, retention_s, tiles_s)
    y_sample = jnp.swapaxes(ys.reshape(ds, bs, D_MODEL), 0, 1)

    def pack(new, b):
        s5r, s5i, lh, lc, rs, fc = [jnp.stack(v) for v in new]
        return (s5r.reshape(DEPTH, b, S5_GROUPS, S5_STATE), s5i.reshape(DEPTH, b, S5_GROUPS, S5_STATE), lh,
                jnp.swapaxes(lc.reshape(DEPTH, LRU_CONV - 1, b, LRU_WIDTH), 1, 2), rs,
                jnp.swapaxes(fc.reshape(DEPTH, FFN_CONV - 1, b, D_FF), 1, 2))

    return (y_prompt, y_sample) + pack(new_p, bp) + pack(new_s, bs)
```

```python
import functools
import math

import jax
import jax.numpy as jnp
import numpy as np
from jax import lax
from jax.experimental import pallas as pl
from jax.experimental.pallas import tpu as pltpu

F32 = jnp.float32
BF16 = jnp.bfloat16

D_MODEL = 1024
DEPTH = 2
PAST_LEN = 16384
N_META = 16
S5_WIDTH = 768
S5_GROUP = 16
S5_GROUPS = S5_WIDTH // S5_GROUP
S5_STATE = 64
S5_N = S5_GROUPS * S5_STATE
LRU_WIDTH = 768
LRU_BLOCK = 64
LRU_BLOCKS = LRU_WIDTH // LRU_BLOCK
LRU_CONV = 4
LRU_C = 8.0
RET_HEADS = 8
RET_DK = 64
RET_DV = 128
RET_QK = RET_HEADS * RET_DK
RET_V = RET_HEADS * RET_DV
RET_CHUNK = 128
ROPE_BASE = 10000.0
N_BRANCH = 3
D_FF = 2816
FFN_CONV = 3
EPS = 1e-6
D_IN = S5_WIDTH + 2 * LRU_WIDTH + 2 * RET_QK + 2 * RET_V + N_BRANCH * D_MODEL

Z_UA = 0
Z_XB = Z_UA + S5_WIDTH
Z_GB = Z_XB + LRU_WIDTH
Z_RET = Z_GB + LRU_WIDTH
Z_RET_W = 2 * RET_QK + 2 * RET_V
Z_GATES = Z_RET + Z_RET_W
RQ, RK, RV, RG = 0, RET_QK, 2 * RET_QK, 2 * RET_QK + RET_V

MXU_K = 256
S5_KT = S5_WIDTH // MXU_K
S5_NT = S5_N // S5_KT
LRU_KT = LRU_WIDTH // MXU_K

SUBLANES = 8
VMEM_LIMIT = 56 * 1024 * 1024


def _cparams(sem):
    return pltpu.CompilerParams(dimension_semantics=sem, vmem_limit_bytes=VMEM_LIMIT)


def _pick_rows(n_steps, bb, target_rows):
    best = None
    for tt in range(1, n_steps + 1):
        if n_steps % tt:
            continue
        rows = tt * bb
        if rows % 16 == 0 and rows <= target_rows:
            best = rows
    if best is None:
        best = n_steps * bb
    return best


def _rms(x, g):
    return x * lax.rsqrt(jnp.mean(x * x, axis=-1, keepdims=True) + EPS) * g


def _sigmoid(x):
    return 0.5 * jnp.tanh(0.5 * x) + 0.5


def _norm_matmul_kernel(x_ref, g_ref, w_ref, o_ref, xn_ref):
    @pl.when(pl.program_id(1) == 0)
    def _():
        xn_ref[...] = _rms(x_ref[...], g_ref[...]).astype(BF16)

    o_ref[...] = jnp.dot(xn_ref[...], w_ref[...],
                         preferred_element_type=F32).astype(o_ref.dtype)


def _norm_matmul(x, g, w, tm, tn):
    rows, d = x.shape
    n = w.shape[1]
    return pl.pallas_call(
        _norm_matmul_kernel,
        grid=(rows // tm, n // tn),
        in_specs=[pl.BlockSpec((tm, d), lambda i, j: (i, 0)),
                  pl.BlockSpec((1, d), lambda i, j: (0, 0)),
                  pl.BlockSpec((d, tn), lambda i, j: (0, j))],
        out_specs=pl.BlockSpec((tm, tn), lambda i, j: (i, j)),
        out_shape=jax.ShapeDtypeStruct((rows, n), BF16),
        scratch_shapes=[pltpu.VMEM((tm, d), BF16)],
        compiler_params=_cparams(("arbitrary", "arbitrary")),
        name="norm_in_proj",
    )(x, g, w)


def _s5_disc_kernel(lre_ref, lim_ref, lstep_ref, bre_ref, bim_ref, wb_ref, are_ref, aim_ref):
    dt = jnp.exp(lstep_ref[...])
    lr = lre_ref[...]
    li = lim_ref[...]
    mag = jnp.exp(lr * dt)
    a_re = mag * jnp.cos(li * dt)
    a_im = mag * jnp.sin(li * dt)
    den = lr * lr + li * li
    nr = a_re - 1.0
    cr = (nr * lr + a_im * li) / den
    ci = (a_im * lr - nr * li) / den
    are_ref[...] = jnp.broadcast_to(a_re, are_ref.shape)
    aim_ref[...] = jnp.broadcast_to(a_im, aim_ref.shape)
    b_r = bre_ref[0]
    b_i = bim_ref[0]
    wb_ref[0, :, 0:S5_NT] = (cr * b_r - ci * b_i).astype(BF16)
    wb_ref[0, :, S5_NT:2 * S5_NT] = (cr * b_i + ci * b_r).astype(BF16)


def _s5_discretize(p):
    vec = pl.BlockSpec((1, S5_NT), lambda k: (0, k))
    blk = pl.BlockSpec((1, MXU_K, S5_NT), lambda k: (k, 0, 0))
    return pl.pallas_call(
        _s5_disc_kernel,
        grid=(S5_KT,),
        in_specs=[vec, vec, vec, blk, blk],
        out_specs=[pl.BlockSpec((1, MXU_K, 2 * S5_NT), lambda k: (k, 0, 0)),
                   pl.BlockSpec((SUBLANES, S5_NT), lambda k: (0, k)),
                   pl.BlockSpec((SUBLANES, S5_NT), lambda k: (0, k))],
        out_shape=[jax.ShapeDtypeStruct((S5_KT, MXU_K, 2 * S5_NT), BF16),
                   jax.ShapeDtypeStruct((SUBLANES, S5_N), F32),
                   jax.ShapeDtypeStruct((SUBLANES, S5_N), F32)],
        compiler_params=_cparams(("arbitrary",)),
        name="s5_discretize",
    )(p['s5_lre'], p['s5_lim'], p['s5_lstep'], p['s5_bre'], p['s5_bim'])


def _s5_kernel(u_ref, wb_ref, are_ref, aim_ref, cre_ref, cim_ref,
               d_ref, wglu_ref, bglu_ref, h0re_ref, h0im_ref,
               y_ref, hre_out, him_out,
               hre_ref, him_ref, sre_ref, sim_ref, *, bb, tc):
    @pl.when(pl.program_id(0) == 0)
    def _():
        hre_ref[...] = h0re_ref[...]
        him_ref[...] = h0im_ref[...]

    u = u_ref[...]
    for kt in range(S5_KT):
        r = jnp.dot(u[:, kt * MXU_K:(kt + 1) * MXU_K], wb_ref[kt], preferred_element_type=F32)
        sre_ref[:, kt * S5_NT:(kt + 1) * S5_NT] = r[:, 0:S5_NT]
        sim_ref[:, kt * S5_NT:(kt + 1) * S5_NT] = r[:, S5_NT:2 * S5_NT]

    def scan_tile(rt, carry):
        r0 = rt * SUBLANES
        for kt in range(S5_KT):
            sl = slice(kt * S5_NT, (kt + 1) * S5_NT)
            ar = are_ref[:, sl]
            ai = aim_ref[:, sl]
            hr0 = hre_ref[pl.ds(r0, SUBLANES), sl]
            hi0 = him_ref[pl.ds(r0, SUBLANES), sl]

            def step(t, h):
                hr, hi = h
                row = pl.multiple_of(t * bb + r0, SUBLANES)
                nr_ = ar * hr - ai * hi + sre_ref[pl.ds(row, SUBLANES), sl]
                ni_ = ar * hi + ai * hr + sim_ref[pl.ds(row, SUBLANES), sl]
                sre_ref[pl.ds(row, SUBLANES), sl] = nr_
                sim_ref[pl.ds(row, SUBLANES), sl] = ni_
                return nr_, ni_

            hr1, hi1 = lax.fori_loop(0, tc, step, (hr0, hi0), unroll=2 if tc % 2 == 0 else 1)
            hre_ref[pl.ds(r0, SUBLANES), sl] = hr1
            him_ref[pl.ds(r0, SUBLANES), sl] = hi1
        return carry

    if bb == SUBLANES:
        scan_tile(0, 0)
    else:
        lax.fori_loop(0, bb // SUBLANES, scan_tile, 0)

    hre_out[...] = hre_ref[...]
    him_out[...] = him_ref[...]

    ys = []
    for kt in range(S5_KT):
        sl = slice(kt * S5_NT, (kt + 1) * S5_NT)
        ys.append(jnp.dot(sre_ref[:, sl].astype(BF16), cre_ref[kt], preferred_element_type=F32)
                  - jnp.dot(sim_ref[:, sl].astype(BF16), cim_ref[kt], preferred_element_type=F32))
    y = jnp.concatenate(ys, axis=-1) + d_ref[...] * u.astype(F32)
    z = jax.nn.gelu(y)
    gate = jnp.dot(z.astype(BF16), wglu_ref[...], preferred_element_type=F32) + bglu_ref[...]
    y_ref[...] = (z * _sigmoid(gate)).astype(y_ref.dtype)


def _s5_mixer(z, h0_re, h0_im, p, bb, n_steps, target_rows):
    rows_total = z.shape[0]
    rows = _pick_rows(n_steps, bb, target_rows)
    tc = rows // bb
    const2 = lambda i: (0, 0)
    const3 = lambda i: (0, 0, 0)
    kern = functools.partial(_s5_kernel, bb=bb, tc=tc)
    return pl.pallas_call(
        kern,
        grid=(rows_total // rows,),
        in_specs=[pl.BlockSpec((rows, S5_WIDTH), lambda i: (i, Z_UA // S5_WIDTH)),
                  pl.BlockSpec((S5_KT, MXU_K, 2 * S5_NT), const3),
                  pl.BlockSpec((SUBLANES, S5_N), const2),
                  pl.BlockSpec((SUBLANES, S5_N), const2),
                  pl.BlockSpec((S5_KT, S5_NT, MXU_K), const3),
                  pl.BlockSpec((S5_KT, S5_NT, MXU_K), const3),
                  pl.BlockSpec((1, S5_WIDTH), const2),
                  pl.BlockSpec((S5_WIDTH, S5_WIDTH), const2),
                  pl.BlockSpec((1, S5_WIDTH), const2),
                  pl.BlockSpec((bb, S5_N), const2),
                  pl.BlockSpec((bb, S5_N), const2)],
        out_specs=[pl.BlockSpec((rows, S5_WIDTH), lambda i: (i, 0)),
                   pl.BlockSpec((bb, S5_N), const2),
                   pl.BlockSpec((bb, S5_N), const2)],
        out_shape=[jax.ShapeDtypeStruct((rows_total, S5_WIDTH), BF16),
                   jax.ShapeDtypeStruct((bb, S5_N), F32),
                   jax.ShapeDtypeStruct((bb, S5_N), F32)],
        scratch_shapes=[pltpu.VMEM((bb, S5_N), F32),
                        pltpu.VMEM((bb, S5_N), F32),
                        pltpu.VMEM((rows, S5_N), F32),
                        pltpu.VMEM((rows, S5_N), F32)],
        compiler_params=_cparams(("arbitrary",)),
        name="s5_mixer",
    )(z, p['s5_wb'], p['s5_are'], p['s5_aim'], p['s5_cre'], p['s5_cim'],
      p['s5_d'], p['s5_w_glu'], p['s5_b_glu'], h0_re, h0_im)


def _lru_kernel(xb_ref, gb_ref, cache_ref, cw_ref, cb_ref, wg_ref, ba_ref, bx_ref, lam_ref, h0_ref,
                y_ref, h_out, cache_out,
                ext_ref, a_ref, b_ref, h_ref, *, bb, tc):
    rows = tc * bb
    halo = (LRU_CONV - 1) * bb

    @pl.when(pl.program_id(0) == 0)
    def _():
        ext_ref[0:halo, :] = cache_ref[...]
        h_ref[...] = h0_ref[...]

    ext_ref[halo:halo + rows, :] = xb_ref[...].astype(F32)
    xc = cb_ref[...] + cw_ref[0:1, :] * ext_ref[0:rows, :]
    for j in range(1, LRU_CONV):
        xc = xc + cw_ref[j:j + 1, :] * ext_ref[j * bb:j * bb + rows, :]
    new_cache = ext_ref[rows:rows + halo, :]
    cache_out[...] = new_cache
    ext_ref[0:halo, :] = new_cache

    xcb = xc.astype(BF16)
    rs, is_ = [], []
    for kt in range(LRU_KT):
        g = jnp.dot(xcb[:, kt * MXU_K:(kt + 1) * MXU_K], wg_ref[kt], preferred_element_type=F32)
        rs.append(g[:, 0:MXU_K])
        is_.append(g[:, MXU_K:2 * MXU_K])
    r = _sigmoid(jnp.concatenate(rs, axis=-1) + ba_ref[...])
    ig = _sigmoid(jnp.concatenate(is_, axis=-1) + bx_ref[...])
    nl = -lam_ref[...]
    softplus = jnp.maximum(nl, 0.0) + jnp.log1p(jnp.exp(-jnp.abs(nl)))
    log_a = (-LRU_C) * r * softplus
    a = jnp.exp(log_a)
    mult = jnp.sqrt(-jnp.tanh(log_a) * (a * a + 1.0))
    a_ref[...] = a
    b_ref[...] = mult * (ig * xc)

    def scan_tile(rt, carry):
        r0 = rt * SUBLANES
        h0 = h_ref[pl.ds(r0, SUBLANES), :]

        def step(t, h):
            row = pl.multiple_of(t * bb + r0, SUBLANES)
            h = a_ref[pl.ds(row, SUBLANES), :] * h + b_ref[pl.ds(row, SUBLANES), :]
            b_ref[pl.ds(row, SUBLANES), :] = h
            return h

        h1 = lax.fori_loop(0, tc, step, h0, unroll=2 if tc % 2 == 0 else 1)
        h_ref[pl.ds(r0, SUBLANES), :] = h1
        return carry

    if bb == SUBLANES:
        scan_tile(0, 0)
    else:
        lax.fori_loop(0, bb // SUBLANES, scan_tile, 0)

    h_out[...] = h_ref[...]
    y_ref[...] = (b_ref[...] * jax.nn.gelu(gb_ref[...].astype(F32))).astype(y_ref.dtype)


def _lru_mixer(z, h0, cache, p, bb, n_steps, target_rows):
    rows_total = z.shape[0]
    rows = _pick_rows(n_steps, bb, target_rows)
    tc = rows // bb
    halo = (LRU_CONV - 1) * bb
    const2 = lambda i: (0, 0)
    vec = pl.BlockSpec((1, LRU_WIDTH), const2)
    kern = functools.partial(_lru_kernel, bb=bb, tc=tc)
    return pl.pallas_call(
        kern,
        grid=(rows_total // rows,),
        in_specs=[pl.BlockSpec((rows, LRU_WIDTH), lambda i: (i, Z_XB // LRU_WIDTH)),
                  pl.BlockSpec((rows, LRU_WIDTH), lambda i: (i, Z_GB // LRU_WIDTH)),
                  pl.BlockSpec((halo, LRU_WIDTH), const2),
                  pl.BlockSpec((LRU_CONV, LRU_WIDTH), const2),
                  vec,
                  pl.BlockSpec((LRU_KT, MXU_K, 2 * MXU_K), lambda i: (0, 0, 0)),
                  vec, vec, vec,
                  pl.BlockSpec((bb, LRU_WIDTH), const2)],
        out_specs=[pl.BlockSpec((rows, LRU_WIDTH), lambda i: (i, 0)),
                   pl.BlockSpec((bb, LRU_WIDTH), const2),
                   pl.BlockSpec((halo, LRU_WIDTH), const2)],
        out_shape=[jax.ShapeDtypeStruct((rows_total, LRU_WIDTH), BF16),
                   jax.ShapeDtypeStruct((bb, LRU_WIDTH), F32),
                   jax.ShapeDtypeStruct((halo, LRU_WIDTH), F32)],
        scratch_shapes=[pltpu.VMEM((rows + halo, LRU_WIDTH), F32),
                        pltpu.VMEM((rows, LRU_WIDTH), F32),
                        pltpu.VMEM((rows, LRU_WIDTH), F32),
                        pltpu.VMEM((bb, LRU_WIDTH), F32)],
        compiler_params=_cparams(("arbitrary",)),
        name="rglru_mixer",
    )(z, z, cache, p['lru_conv_w'], p['lru_conv_b'], p['lru_wg'], p['lru_b_a'], p['lru_b_x'],
      p['lru_lam'], h0)


def _ret_sequence(q, k, v, state, cos, sin, decay_ref, kdec, cross, sdec, ng):
    half = RET_DK // 2
    lane = lax.broadcasted_iota(jnp.int32, q.shape, 1)
    first = (lane % RET_DK) < half
    mean = lambda x: jnp.mean(x, axis=-1, keepdims=True)

    def rope(x):
        swapped = jnp.where(first, pltpu.roll(x, RET_QK - half, 1), pltpu.roll(x, half, 1))
        return x * cos + swapped * sin

    qr = rope(q).astype(BF16)
    kr = rope(k) * (RET_DK ** -0.5)
    kd = (kr * kdec).astype(BF16)
    kr = kr.astype(BF16)
    outs, new_states = [], []
    for h in range(RET_HEADS):
        qh = qr[:, h * RET_DK:(h + 1) * RET_DK]
        kh = kr[:, h * RET_DK:(h + 1) * RET_DK]
        kdh = kd[:, h * RET_DK:(h + 1) * RET_DK]
        vh = v[:, h * RET_DV:(h + 1) * RET_DV]
        s = state(h)
        sc = lax.dot_general(qh, kh, (((1,), (1,)), ((), ())), preferred_element_type=F32)
        sc = (sc * decay_ref[h]).astype(BF16)
        o = jnp.dot(sc, vh, preferred_element_type=F32)
        o = o + jnp.dot(qh, s.astype(BF16), preferred_element_type=F32) * cross[:, h * RET_DV:(h + 1) * RET_DV]
        new_states.append(sdec[h:h + 1, :] * s + lax.dot_general(kdh, vh, (((0,), (0,)), ((), ())),
                                                                 preferred_element_type=F32))
        oc = o - mean(o)
        outs.append(oc * lax.rsqrt(mean(oc * oc) + EPS) * ng[:, h * RET_DV:(h + 1) * RET_DV])
    return jnp.concatenate(outs, axis=-1), new_states


def _ret_bm_kernel(z_ref, cos_ref, sin_ref, decay_ref, kdec_ref, cross_ref, sdec_ref, ng_ref, s0_ref,
                   y_ref, s_out):
    cos, sin, kdec, cross, sdec, ng = (cos_ref[...], sin_ref[...], kdec_ref[...], cross_ref[...],
                                       sdec_ref[...], ng_ref[...])

    def one(i, carry):
        blk = z_ref[i]
        v = blk[:, RV:RV + RET_V]
        g = blk[:, RG:RG + RET_V].astype(F32)
        q = blk[:, RQ:RQ + RET_QK].astype(F32)
        k = blk[:, RK:RK + RET_QK].astype(F32)
        on, new_states = _ret_sequence(q, k, v, lambda h: s0_ref[i, h], cos, sin, decay_ref, kdec, cross,
                                       sdec, ng)
        for h in range(RET_HEADS):
            s_out[i, h] = new_states[h]
        y_ref[i] = (g * _sigmoid(g) * on).astype(y_ref.dtype)
        return carry

    lax.fori_loop(0, z_ref.shape[0], one, 0, unroll=True)


def _ret_mixer_bm(zr, s0_layers, layer, norm_g, tabs, group):
    nb, c, _ = zr.shape
    const2 = lambda i: (0, 0)
    return pl.pallas_call(
        _ret_bm_kernel,
        grid=(nb // group,),
        in_specs=[pl.BlockSpec((group, c, Z_RET_W), lambda i: (i, 0, 0)),
                  pl.BlockSpec((c, RET_QK), const2),
                  pl.BlockSpec((c, RET_QK), const2),
                  pl.BlockSpec((RET_HEADS, c, c), lambda i: (0, 0, 0)),
                  pl.BlockSpec((c, RET_QK), const2),
                  pl.BlockSpec((c, RET_V), const2),
                  pl.BlockSpec((RET_HEADS, RET_DV), const2),
                  pl.BlockSpec((1, RET_V), const2),
                  pl.BlockSpec((None, group, RET_HEADS, RET_DK, RET_DV), lambda i: (layer, i, 0, 0, 0))],
        out_specs=[pl.BlockSpec((group, c, RET_V), lambda i: (i, 0, 0)),
                   pl.BlockSpec((group, RET_HEADS, RET_DK, RET_DV), lambda i: (i, 0, 0, 0))],
        out_shape=[jax.ShapeDtypeStruct((nb, c, RET_V), BF16),
                   jax.ShapeDtypeStruct((nb, RET_HEADS, RET_DK, RET_DV), F32)],
        compiler_params=_cparams(("arbitrary",)),
        name="retention_step",
    )(zr, tabs['cos'], tabs['sin'], tabs['decay'], tabs['kdec'], tabs['cross'], tabs['sdec'], norm_g,
      s0_layers)


RET_V_SLABS = RET_V // 128
RET_QK_SLABS = RET_QK // 128
RET_IN_SLABS = RET_V_SLABS + 2 * RET_QK_SLABS


def _ret_tm_kernel(*refs, steps):
    z_ref, cos_ref, sin_ref, decay_ref, kdec_ref, cross_ref, sdec_ref, ng_ref, s0_ref = refs[:9]
    y_ref, s_out, in_slab, out_slab = refs[-4:]
    nseq = SUBLANES

    @pl.when(pl.program_id(0) == 0)
    def _():
        s_out[...] = s0_ref[...]

    def fill(lo, n, x):
        for s in range(n):
            in_slab[lo + s] = x[:, s * 128:(s + 1) * 128].astype(F32)

    nv, nq = RET_V_SLABS, RET_QK_SLABS
    fill(0, nv, z_ref[:, RV:RV + RET_V])
    fill(nv, nq, z_ref[:, RQ:RQ + RET_QK])
    fill(nv + nq, nq, z_ref[:, RK:RK + RET_QK])
    cos, sin, kdec, cross, sdec, ng = (cos_ref[...], sin_ref[...], kdec_ref[...], cross_ref[...],
                                       sdec_ref[...], ng_ref[...])

    def one(b, carry):
        rows = pl.ds(b, steps, stride=nseq)
        gather = lambda lo, n: jnp.concatenate([in_slab[lo + s, rows, :] for s in range(n)], axis=-1)
        v = gather(0, nv).astype(BF16)
        q = gather(nv, nq)
        k = gather(nv + nq, nq)
        on, new_states = _ret_sequence(q, k, v, lambda h: s_out[b, h], cos, sin, decay_ref, kdec, cross,
                                       sdec, ng)
        for h in range(RET_HEADS):
            s_out[b, h] = new_states[h]
        for s in range(nv):
            out_slab[s, rows, :] = on[:, s * 128:(s + 1) * 128]
        return carry

    lax.fori_loop(0, nseq, one, 0)
    g = z_ref[:, RG:RG + RET_V].astype(F32)
    on = jnp.concatenate([out_slab[s] for s in range(nv)], axis=-1)
    y_ref[...] = (g * _sigmoid(g) * on).astype(y_ref.dtype)


def _ret_mixer_tm(z, s0, norm_g, tabs, row0, n_chunks):
    c = tabs['chunk']
    rows = c * SUBLANES
    row_at = lambda i: pl.multiple_of(row0 + rows * i, 128)
    const2 = lambda i: (0, 0)
    in_specs = [pl.BlockSpec((pl.Element(rows), pl.Element(Z_RET_W)), lambda i: (row_at(i), Z_RET)),
                pl.BlockSpec((c, RET_QK), lambda i: (i, 0)),
                pl.BlockSpec((c, RET_QK), lambda i: (i, 0)),
                pl.BlockSpec((RET_HEADS, c, c), lambda i: (0, 0, 0)),
                pl.BlockSpec((c, RET_QK), const2),
                pl.BlockSpec((c, RET_V), const2),
                pl.BlockSpec((RET_HEADS, RET_DV), const2),
                pl.BlockSpec((1, RET_V), const2),
                pl.BlockSpec((SUBLANES, RET_HEADS, RET_DK, RET_DV), lambda i: (0, 0, 0, 0))]
    args = [z, tabs['cos'], tabs['sin'], tabs['decay'], tabs['kdec'], tabs['cross'], tabs['sdec'], norm_g, s0]
    return pl.pallas_call(
        functools.partial(_ret_tm_kernel, steps=c),
        grid=(n_chunks,),
        in_specs=in_specs,
        out_specs=[pl.BlockSpec((rows, RET_V), lambda i: (i, 0)),
                   pl.BlockSpec((SUBLANES, RET_HEADS, RET_DK, RET_DV), lambda i: (0, 0, 0, 0))],
        out_shape=[jax.ShapeDtypeStruct((n_chunks * rows, RET_V), BF16),
                   jax.ShapeDtypeStruct((SUBLANES, RET_HEADS, RET_DK, RET_DV), F32)],
        scratch_shapes=[pltpu.VMEM((RET_IN_SLABS, rows, 128), F32),
                        pltpu.VMEM((RET_V_SLABS, rows, 128), F32)],
        compiler_params=_cparams(("arbitrary",)),
        name="retention_chunks",
    )(*args)


def _ret_tables(pos, chunk, n_real):
    half = RET_DK // 2
    inv = np.power(ROPE_BASE, -np.arange(half, dtype=np.float64) / half)
    ang = np.asarray(pos, np.float64)[:, None] * inv
    cos_h = np.concatenate([np.cos(ang), np.cos(ang)], axis=-1)
    sin_h = np.concatenate([-np.sin(ang), np.sin(ang)], axis=-1)
    log_gamma = np.log1p(-np.exp2(-5.0 - np.arange(RET_HEADS, dtype=np.float64)))
    n = np.arange(chunk, dtype=np.float64)
    diff = n[:, None] - n[None, :]
    decay = np.where(diff >= 0, np.exp(log_gamma[:, None, None] * np.maximum(diff, 0.0)), 0.0)
    kdec = np.exp(log_gamma[:, None] * (chunk - 1.0 - n))
    m = np.maximum(n - (chunk - n_real), 0.0)
    cross = np.exp(log_gamma[:, None] * (m + 1.0))
    const = lambda a: jnp.asarray(np.ascontiguousarray(a), F32)
    return {'chunk': chunk,
            'cos': const(np.tile(cos_h, (1, RET_HEADS))), 'sin': const(np.tile(sin_h, (1, RET_HEADS))),
            'decay': const(decay),
            'kdec': const(np.repeat(kdec.T, RET_DK, axis=1)),
            'cross': const(np.repeat(cross.T, RET_DV, axis=1)),
            'sdec': const(np.broadcast_to(np.exp(log_gamma * n_real)[:, None], (RET_HEADS, RET_DV)))}


def _merge_kernel(x_ref, gt_ref, ya_ref, yb_ref, yc_ref, *rest, head):
    if head:
        yh_ref, wa_ref, wb_ref, wc_ref, wo_ref, g_ref, o_ref, ycs_ref = rest
        tm = yc_ref.shape[0]

        @pl.when(pl.program_id(0) == 0)
        def _():
            ycs_ref[0:head, :] = yh_ref[...]
            ycs_ref[head:tm, :] = yc_ref[0:tm - head, :]

        @pl.when(pl.program_id(0) > 0)
        def _():
            ycs_ref[...] = yc_ref[...]

        yc = ycs_ref[...]
    else:
        wa_ref, wb_ref, wc_ref, wo_ref, g_ref, o_ref = rest
        yc = yc_ref[...]
    gate = lambda j: _sigmoid(gt_ref[:, j * D_MODEL:(j + 1) * D_MODEL].astype(F32))
    m = gate(0) * jnp.dot(ya_ref[...], wa_ref[...], preferred_element_type=F32)
    m = m + gate(1) * jnp.dot(yb_ref[...], wb_ref[...], preferred_element_type=F32)
    m = m + gate(2) * jnp.dot(yc, wc_ref[...], preferred_element_type=F32)
    mo = jnp.dot(m.astype(BF16), wo_ref[...], preferred_element_type=F32)
    o_ref[...] = x_ref[...] + _rms(mo, g_ref[...])


def _merge(x, z, ya, yb, yc, p, tm):
    rows = x.shape[0]
    row = lambda i: (i, 0)
    const = lambda i: (0, 0)
    if isinstance(yc, tuple):
        y_head, yc = yc
        head = y_head.shape[0]
        assert head % 16 == 0 and head < tm and head + yc.shape[0] == rows
        yc_specs = [pl.BlockSpec((pl.Element(tm), pl.Element(RET_V)),
                                 lambda i: (pl.multiple_of(jnp.maximum(i * tm - head, 0), 16), 0)),
                    pl.BlockSpec((head, RET_V), const)]
        yc_args = [yc, y_head]
        scratch = [pltpu.VMEM((tm, RET_V), BF16)]
    else:
        head = 0
        yc_specs = [pl.BlockSpec((tm, RET_V), row)]
        yc_args = [yc]
        scratch = []
    return pl.pallas_call(
        functools.partial(_merge_kernel, head=head),
        grid=(rows // tm,),
        in_specs=[pl.BlockSpec((tm, D_MODEL), row),
                  pl.BlockSpec((pl.Element(tm), pl.Element(N_BRANCH * D_MODEL)),
                               lambda i: (pl.multiple_of(i * tm, 16), Z_GATES)),
                  pl.BlockSpec((tm, S5_WIDTH), row),
                  pl.BlockSpec((tm, LRU_WIDTH), row)] + yc_specs + [
                  pl.BlockSpec((S5_WIDTH, D_MODEL), const),
                  pl.BlockSpec((LRU_WIDTH, D_MODEL), const),
                  pl.BlockSpec((RET_V, D_MODEL), const),
                  pl.BlockSpec((D_MODEL, D_MODEL), const),
                  pl.BlockSpec((1, D_MODEL), const)],
        out_specs=pl.BlockSpec((tm, D_MODEL), row),
        out_shape=jax.ShapeDtypeStruct((rows, D_MODEL), F32),
        scratch_shapes=scratch,
        compiler_params=_cparams(("arbitrary",)),
        name="branch_merge",
    )(x, z, ya, yb, *yc_args, p['w_branch_a'], p['w_branch_b'], p['w_branch_c'], p['w_out'], p['g_post_mix'])


def _ffn_kernel(x_ref, gpre_ref, wg_ref, wu_ref, cw_ref, cb_ref, wd_ref, gpost_ref, cache_ref,
                o_ref, cache_out, xn_ref, acc_ref, ext_ref, *, bb, n_chunks):
    i = pl.program_id(0)
    c = pl.program_id(1)
    rows = x_ref.shape[0]
    halo = (FFN_CONV - 1) * bb

    @pl.when(c == 0)
    def _():
        xn_ref[...] = _rms(x_ref[...], gpre_ref[...]).astype(BF16)

    @pl.when(i == 0)
    def _():
        ext_ref[c, 0:halo, :] = cache_ref[...]

    xn = xn_ref[...]
    gf = jnp.dot(xn, wg_ref[...], preferred_element_type=F32)
    uf = jnp.dot(xn, wu_ref[...], preferred_element_type=F32)
    ext_ref[c, halo:halo + rows, :] = gf
    gc = cb_ref[...] + cw_ref[0:1, :] * ext_ref[c, 0:rows, :]
    for j in range(1, FFN_CONV):
        gc = gc + cw_ref[j:j + 1, :] * ext_ref[c, j * bb:j * bb + rows, :]
    new_cache = ext_ref[c, rows:rows + halo, :]
    cache_out[c] = new_cache
    ext_ref[c, 0:halo, :] = new_cache
    act = (jax.nn.gelu(gc) * uf).astype(BF16)
    part = jnp.dot(act, wd_ref[...], preferred_element_type=F32)

    @pl.when(c == 0)
    def _():
        acc_ref[...] = part

    @pl.when(c > 0)
    def _():
        acc_ref[...] += part

    @pl.when(c == n_chunks - 1)
    def _():
        o_ref[...] = x_ref[...] + _rms(acc_ref[...], gpost_ref[...])


def _ffn(x, cache, p, bb, tm, tf):
    rows = x.shape[0]
    halo = (FFN_CONV - 1) * bb
    n_chunks = D_FF // tf
    kern = functools.partial(_ffn_kernel, bb=bb, n_chunks=n_chunks)
    return pl.pallas_call(
        kern,
        grid=(rows // tm, n_chunks),
        in_specs=[pl.BlockSpec((tm, D_MODEL), lambda i, c: (i, 0)),
                  pl.BlockSpec((1, D_MODEL), lambda i, c: (0, 0)),
                  pl.BlockSpec((D_MODEL, tf), lambda i, c: (0, c)),
                  pl.BlockSpec((D_MODEL, tf), lambda i, c: (0, c + n_chunks)),
                  pl.BlockSpec((FFN_CONV, tf), lambda i, c: (0, c)),
                  pl.BlockSpec((1, tf), lambda i, c: (0, c)),
                  pl.BlockSpec((tf, D_MODEL), lambda i, c: (c, 0)),
                  pl.BlockSpec((1, D_MODEL), lambda i, c: (0, 0)),
                  pl.BlockSpec((halo, tf), lambda i, c: (0, c))],
        out_specs=[pl.BlockSpec((tm, D_MODEL), lambda i, c: (i, 0)),
                   pl.BlockSpec((n_chunks, halo, tf), lambda i, c: (0, 0, 0))],
        out_shape=[jax.ShapeDtypeStruct((rows, D_MODEL), F32),
                   jax.ShapeDtypeStruct((n_chunks, halo, tf), F32)],
        scratch_shapes=[pltpu.VMEM((tm, D_MODEL), BF16),
                        pltpu.VMEM((tm, D_MODEL), F32),
                        pltpu.VMEM((n_chunks, tm + halo, tf), F32)],
        compiler_params=_cparams(("arbitrary", "arbitrary")),
        name="conv_ffn",
    )(x, p['g_pre_ffn'], p['ffn_w_up'], p['ffn_w_up'], p['ffn_conv_w'], p['ffn_conv_b'],
      p['ffn_w_down'], p['g_post_ffn'], cache)


def _block_diag(w, tiles):
    nb, r, c = w.shape
    per = nb // tiles
    rows, cols = per * r, per * c
    on_diag = (np.arange(rows)[:, None] // r) == (np.arange(cols)[None, :] // c)
    return jnp.where(on_diag, jnp.tile(w.reshape(tiles, rows, c), (1, 1, per)), 0)


def _prep_layer(P, l):
    row = lambda a: a[l].reshape(1, -1).astype(F32)
    p = {
        'g_pre_mix': row(P['g_pre_mix']), 'g_post_mix': row(P['g_post_mix']),
        'g_pre_ffn': row(P['g_pre_ffn']), 'g_post_ffn': row(P['g_post_ffn']),
        'w_in': P['w_in'][l].astype(BF16),
        's5_lre': row(P['s5_a_re']), 's5_lim': row(P['s5_a_im']),
        's5_lstep': jnp.repeat(P['s5_log_step'][l], S5_STATE).reshape(1, S5_N),
        's5_bre': _block_diag(jnp.swapaxes(P['s5_b_re'][l], 1, 2), S5_KT),
        's5_bim': _block_diag(jnp.swapaxes(P['s5_b_im'][l], 1, 2), S5_KT),
        's5_cre': _block_diag(jnp.swapaxes(P['s5_c_re'][l], 1, 2), S5_KT).astype(BF16),
        's5_cim': _block_diag(jnp.swapaxes(P['s5_c_im'][l], 1, 2), S5_KT).astype(BF16),
        's5_d': row(P['s5_d']), 's5_w_glu': P['s5_w_glu'][l].astype(BF16), 's5_b_glu': row(P['s5_b_glu']),
        'lru_conv_w': P['lru_conv_w'][l], 'lru_conv_b': row(P['lru_conv_b']),
        'lru_wg': jnp.concatenate([_block_diag(P['lru_w_a'][l], LRU_KT),
                                   _block_diag(P['lru_w_x'][l], LRU_KT)], axis=-1).astype(BF16),
        'lru_b_a': row(P['lru_b_a']), 'lru_b_x': row(P['lru_b_x']), 'lru_lam': row(P['lru_lam']),
        'ret_norm_g': row(P['ret_norm_g']),
        'w_branch_a': P['w_branch_a'][l].astype(BF16), 'w_branch_b': P['w_branch_b'][l].astype(BF16),
        'w_branch_c': P['w_branch_c'][l].astype(BF16), 'w_out': P['w_out'][l].astype(BF16),
        'ffn_w_up': P['ffn_w_up'][l].astype(BF16), 'ffn_conv_w': P['ffn_conv_w'][l],
        'ffn_conv_b': row(P['ffn_conv_b']), 'ffn_w_down': P['ffn_w_down'][l].astype(BF16),
    }
    p['s5_wb'], p['s5_are'], p['s5_aim'] = _s5_discretize(p)
    return p


def _run_group(x, states, params, bb, n_steps, retention, tiles):
    s5_re, s5_im, lru_h, lru_cache, ret_s, ffn_cache = states
    new = [[] for _ in range(6)]
    for l in range(DEPTH):
        p = params[l]
        z = _norm_matmul(x, p['g_pre_mix'], p['w_in'], tiles['tm_in'], tiles['tn_in'])
        ya, hre, him = _s5_mixer(z, s5_re[l], s5_im[l], p, bb, n_steps, tiles['rows_scan'])
        yb, hl, lcache = _lru_mixer(z, lru_h[l], lru_cache[l], p, bb, n_steps, tiles['rows_scan'])
        yc, rs = retention(z, ret_s[l], p['ret_norm_g'])
        x = _merge(x, z, ya, yb, yc, p, tiles['tm'])
        x, fcache = _ffn(x, ffn_cache[l], p, bb, tiles['tm'], tiles['tf'])
        fcache = jnp.swapaxes(fcache, 0, 1).reshape(-1, D_FF)
        for lst, val in zip(new, (hre, him, hl, lcache, rs, fcache)):
            lst.append(val)
    return x, new


def kernel(x_prompt, x_sample, state_s5_re, state_s5_im, state_lru, cache_lru_conv, state_ret, cache_ffn_conv, meta_tokens, g_pre_mix, g_post_mix, g_pre_ffn, g_post_ffn, w_in, s5_a_re, s5_a_im, s5_log_step, s5_b_re, s5_b_im, s5_c_re, s5_c_im, s5_d, s5_w_glu, s5_b_glu, lru_conv_w, lru_conv_b, lru_w_a, lru_b_a, lru_w_x, lru_b_x, lru_lam, ret_norm_g, w_branch_a, w_branch_b, w_branch_c, w_out, ffn_w_up, ffn_conv_w, ffn_conv_b, ffn_w_down):
    P = {'g_pre_mix': g_pre_mix, 'g_post_mix': g_post_mix, 'g_pre_ffn': g_pre_ffn, 'g_post_ffn': g_post_ffn,
         'w_in': w_in, 's5_a_re': s5_a_re, 's5_a_im': s5_a_im, 's5_log_step': s5_log_step,
         's5_b_re': s5_b_re, 's5_b_im': s5_b_im, 's5_c_re': s5_c_re, 's5_c_im': s5_c_im,
         's5_d': s5_d, 's5_w_glu': s5_w_glu, 's5_b_glu': s5_b_glu,
         'lru_conv_w': lru_conv_w, 'lru_conv_b': lru_conv_b, 'lru_w_a': lru_w_a, 'lru_b_a': lru_b_a,
         'lru_w_x': lru_w_x, 'lru_b_x': lru_b_x, 'lru_lam': lru_lam, 'ret_norm_g': ret_norm_g,
         'w_branch_a': w_branch_a, 'w_branch_b': w_branch_b, 'w_branch_c': w_branch_c, 'w_out': w_out,
         'ffn_w_up': ffn_w_up, 'ffn_conv_w': ffn_conv_w, 'ffn_conv_b': ffn_conv_b, 'ffn_w_down': ffn_w_down}
    params = [_prep_layer(P, l) for l in range(DEPTH)]

    bp, sp, _ = x_prompt.shape
    lp = N_META + sp
    meta = jnp.broadcast_to(meta_tokens.astype(F32)[:, None, :], (N_META, bp, D_MODEL))
    xp = jnp.concatenate([meta, jnp.swapaxes(x_prompt, 0, 1)], axis=0).reshape(lp * bp, D_MODEL)
    zeros = lambda *s: [jnp.zeros(s, F32)] * DEPTH
    st_p = (zeros(bp, S5_N), zeros(bp, S5_N), zeros(bp, LRU_WIDTH),
            zeros((LRU_CONV - 1) * bp, LRU_WIDTH), zeros(bp, RET_HEADS, RET_DK, RET_DV),
            zeros((FFN_CONV - 1) * bp, D_FF))
    assert bp == SUBLANES and sp % RET_CHUNK == 0 and (N_META * bp) % 128 == 0
    tabs_meta = _ret_tables(np.arange(N_META), N_META, N_META)
    tabs_main = _ret_tables(N_META + np.arange(sp), RET_CHUNK, RET_CHUNK)

    def retention_p(z, s0, ng):
        y_meta, s1 = _ret_mixer_tm(z, s0, ng, tabs_meta, 0, 1)
        y_main, s2 = _ret_mixer_tm(z, s1, ng, tabs_main, N_META * bp, sp // RET_CHUNK)
        return (y_meta, y_main), s2

    tiles_p = {'tm_in': _pick_rows(lp, bp, 1400), 'tn_in': 1408, 'rows_scan': 700,
               'tm': _pick_rows(lp, bp, 700), 'tf': 1408}
    yp, new_p = _run_group(xp, st_p, params, bp, lp, retention_p, tiles_p)
    y_prompt = jnp.swapaxes(yp.reshape(lp, bp, D_MODEL), 0, 1)[:, N_META:]

    bs, ds, _ = x_sample.shape
    xs = jnp.swapaxes(x_sample, 0, 1).reshape(ds * bs, D_MODEL)
    tm_major = lambda c: jnp.swapaxes(c, 0, 1).reshape(-1, c.shape[-1])
    st_s = ([state_s5_re[l].reshape(bs, S5_N) for l in range(DEPTH)],
            [state_s5_im[l].reshape(bs, S5_N) for l in range(DEPTH)],
            [state_lru[l] for l in range(DEPTH)],
            [tm_major(cache_lru_conv[l]) for l in range(DEPTH)],
            list(range(DEPTH)),
            [tm_major(cache_ffn_conv[l]) for l in range(DEPTH)])
    chunk_s = 16
    pad_s = chunk_s - ds
    tabs_s = _ret_tables(np.concatenate([np.zeros(pad_s), PAST_LEN + np.arange(ds)]), chunk_s, ds)

    def retention_s(z, layer, ng):
        zr = z[:, Z_RET:Z_RET + Z_RET_W].reshape(ds, bs, Z_RET_W).transpose(1, 0, 2)
        zr = jnp.pad(zr, ((0, 0), (pad_s, 0), (0, 0)))
        yc, s1 = _ret_mixer_bm(zr, state_ret, layer, ng, tabs_s, SUBLANES)
        return yc[:, pad_s:].transpose(1, 0, 2).reshape(ds * bs, RET_V), s1

    tiles_s = {'tm_in': ds * bs, 'tn_in': 1408, 'rows_scan': ds * bs, 'tm': ds * bs, 'tf': 1408}
    ys, new_s = _run_group(xs, st_s, params, bs, ds, retention_s, tiles_s)
    y_sample = jnp.swapaxes(ys.reshape(ds, bs, D_MODEL), 0, 1)

    def pack(new, b):
        s5r, s5i, lh, lc, rs, fc = [jnp.stack(v) for v in new]
        return (s5r.reshape(DEPTH, b, S5_GROUPS, S5_STATE), s5i.reshape(DEPTH, b, S5_GROUPS, S5_STATE), lh,
                jnp.swapaxes(lc.reshape(DEPTH, LRU_CONV - 1, b, LRU_WIDTH), 1, 2), rs,
                jnp.swapaxes(fc.reshape(DEPTH, FFN_CONV - 1, b, D_FF), 1, 2))

    return (y_prompt, y_sample) + pack(new_p, bp) + pack(new_s, bs)
```

```python
import functools
import math

import jax
import jax.numpy as jnp
import numpy as np
from jax import lax
from jax.experimental import pallas as pl
from jax.experimental.pallas import tpu as pltpu

F32 = jnp.float32
BF16 = jnp.bfloat16

D_MODEL = 1024
DEPTH = 2
PAST_LEN = 16384
N_META = 16
S5_WIDTH = 768
S5_GROUP = 16
S5_GROUPS = S5_WIDTH // S5_GROUP
S5_STATE = 64
S5_N = S5_GROUPS * S5_STATE
LRU_WIDTH = 768
LRU_BLOCK = 64
LRU_BLOCKS = LRU_WIDTH // LRU_BLOCK
LRU_CONV = 4
LRU_C = 8.0
RET_HEADS = 8
RET_DK = 64
RET_DV = 128
RET_QK = RET_HEADS * RET_DK
RET_V = RET_HEADS * RET_DV
RET_CHUNK = 128
ROPE_BASE = 10000.0
N_BRANCH = 3
D_FF = 2816
FFN_CONV = 3
EPS = 1e-6
D_IN = S5_WIDTH + 2 * LRU_WIDTH + 2 * RET_QK + 2 * RET_V + N_BRANCH * D_MODEL

Z_UA = 0
Z_XB = Z_UA + S5_WIDTH
Z_GB = Z_XB + LRU_WIDTH
Z_RET = Z_GB + LRU_WIDTH
Z_RET_W = 2 * RET_QK + 2 * RET_V
Z_GATES = Z_RET + Z_RET_W
RQ, RK, RV, RG = 0, RET_QK, 2 * RET_QK, 2 * RET_QK + RET_V

MXU_K = 256
S5_KT = S5_WIDTH // MXU_K
S5_NT = S5_N // S5_KT
LRU_KT = LRU_WIDTH // MXU_K

SUBLANES = 8
VMEM_LIMIT = 56 * 1024 * 1024


def _cparams(sem):
    return pltpu.CompilerParams(dimension_semantics=sem, vmem_limit_bytes=VMEM_LIMIT)


def _pick_rows(n_steps, bb, target_rows):
    best = None
    for tt in range(1, n_steps + 1):
        if n_steps % tt:
            continue
        rows = tt * bb
        if rows % 16 == 0 and rows <= target_rows:
            best = rows
    if best is None:
        best = n_steps * bb
    return best


def _rms(x, g):
    return x * lax.rsqrt(jnp.mean(x * x, axis=-1, keepdims=True) + EPS) * g


def _sigmoid(x):
    return 0.5 * jnp.tanh(0.5 * x) + 0.5


def _norm_matmul_kernel(x_ref, g_ref, w_ref, o_ref, xn_ref):
    @pl.when(pl.program_id(1) == 0)
    def _():
        xn_ref[...] = _rms(x_ref[...], g_ref[...]).astype(BF16)

    o_ref[...] = jnp.dot(xn_ref[...], w_ref[...],
                         preferred_element_type=F32).astype(o_ref.dtype)


def _norm_matmul(x, g, w, tm, tn):
    rows, d = x.shape
    n = w.shape[1]
    return pl.pallas_call(
        _norm_matmul_kernel,
        grid=(rows // tm, n // tn),
        in_specs=[pl.BlockSpec((tm, d), lambda i, j: (i, 0)),
                  pl.BlockSpec((1, d), lambda i, j: (0, 0)),
                  pl.BlockSpec((d, tn), lambda i, j: (0, j))],
        out_specs=pl.BlockSpec((tm, tn), lambda i, j: (i, j)),
        out_shape=jax.ShapeDtypeStruct((rows, n), BF16),
        scratch_shapes=[pltpu.VMEM((tm, d), BF16)],
        compiler_params=_cparams(("arbitrary", "arbitrary")),
        name="norm_in_proj",
    )(x, g, w)


def _s5_disc_kernel(lre_ref, lim_ref, lstep_ref, bre_ref, bim_ref, wb_ref, are_ref, aim_ref):
    dt = jnp.exp(lstep_ref[...])
    lr = lre_ref[...]
    li = lim_ref[...]
    mag = jnp.exp(lr * dt)
    a_re = mag * jnp.cos(li * dt)
    a_im = mag * jnp.sin(li * dt)
    den = lr * lr + li * li
    nr = a_re - 1.0
    cr = (nr * lr + a_im * li) / den
    ci = (a_im * lr - nr * li) / den
    are_ref[...] = jnp.broadcast_to(a_re, are_ref.shape)
    aim_ref[...] = jnp.broadcast_to(a_im, aim_ref.shape)
    b_r = bre_ref[0]
    b_i = bim_ref[0]
    wb_ref[0, :, 0:S5_NT] = (cr * b_r - ci * b_i).astype(BF16)
    wb_ref[0, :, S5_NT:2 * S5_NT] = (cr * b_i + ci * b_r).astype(BF16)


def _s5_discretize(p):
    vec = pl.BlockSpec((1, S5_NT), lambda k: (0, k))
    blk = pl.BlockSpec((1, MXU_K, S5_NT), lambda k: (k, 0, 0))
    return pl.pallas_call(
        _s5_disc_kernel,
        grid=(S5_KT,),
        in_specs=[vec, vec, vec, blk, blk],
        out_specs=[pl.BlockSpec((1, MXU_K, 2 * S5_NT), lambda k: (k, 0, 0)),
                   pl.BlockSpec((SUBLANES, S5_NT), lambda k: (0, k)),
                   pl.BlockSpec((SUBLANES, S5_NT), lambda k: (0, k))],
        out_shape=[jax.ShapeDtypeStruct((S5_KT, MXU_K, 2 * S5_NT), BF16),
                   jax.ShapeDtypeStruct((SUBLANES, S5_N), F32),
                   jax.ShapeDtypeStruct((SUBLANES, S5_N), F32)],
        compiler_params=_cparams(("arbitrary",)),
        name="s5_discretize",
    )(p['s5_lre'], p['s5_lim'], p['s5_lstep'], p['s5_bre'], p['s5_bim'])


def _s5_kernel(u_ref, wb_ref, are_ref, aim_ref, cre_ref, cim_ref,
               d_ref, wglu_ref, bglu_ref, h0re_ref, h0im_ref,
               y_ref, hre_out, him_out,
               hre_ref, him_ref, sre_ref, sim_ref, *, bb, tc):
    @pl.when(pl.program_id(0) == 0)
    def _():
        hre_ref[...] = h0re_ref[...]
        him_ref[...] = h0im_ref[...]

    u = u_ref[...]
    for kt in range(S5_KT):
        r = jnp.dot(u[:, kt * MXU_K:(kt + 1) * MXU_K], wb_ref[kt], preferred_element_type=F32)
        sre_ref[:, kt * S5_NT:(kt + 1) * S5_NT] = r[:, 0:S5_NT]
        sim_ref[:, kt * S5_NT:(kt + 1) * S5_NT] = r[:, S5_NT:2 * S5_NT]

    def scan_tile(rt, carry):
        r0 = rt * SUBLANES
        for kt in range(S5_KT):
            sl = slice(kt * S5_NT, (kt + 1) * S5_NT)
            ar = are_ref[:, sl]
            ai = aim_ref[:, sl]
            hr0 = hre_ref[pl.ds(r0, SUBLANES), sl]
            hi0 = him_ref[pl.ds(r0, SUBLANES), sl]

            def step(t, h):
                hr, hi = h
                row = pl.multiple_of(t * bb + r0, SUBLANES)
                nr_ = ar * hr - ai * hi + sre_ref[pl.ds(row, SUBLANES), sl]
                ni_ = ar * hi + ai * hr + sim_ref[pl.ds(row, SUBLANES), sl]
                sre_ref[pl.ds(row, SUBLANES), sl] = nr_
                sim_ref[pl.ds(row, SUBLANES), sl] = ni_
                return nr_, ni_

            hr1, hi1 = lax.fori_loop(0, tc, step, (hr0, hi0), unroll=2 if tc % 2 == 0 else 1)
            hre_ref[pl.ds(r0, SUBLANES), sl] = hr1
            him_ref[pl.ds(r0, SUBLANES), sl] = hi1
        return carry

    if bb == SUBLANES:
        scan_tile(0, 0)
    else:
        lax.fori_loop(0, bb // SUBLANES, scan_tile, 0)

    hre_out[...] = hre_ref[...]
    him_out[...] = him_ref[...]

    ys = []
    for kt in range(S5_KT):
        sl = slice(kt * S5_NT, (kt + 1) * S5_NT)
        ys.append(jnp.dot(sre_ref[:, sl].astype(BF16), cre_ref[kt], preferred_element_type=F32)
                  - jnp.dot(sim_ref[:, sl].astype(BF16), cim_ref[kt], preferred_element_type=F32))
    y = jnp.concatenate(ys, axis=-1) + d_ref[...] * u.astype(F32)
    z = jax.nn.gelu(y)
    gate = jnp.dot(z.astype(BF16), wglu_ref[...], preferred_element_type=F32) + bglu_ref[...]
    y_ref[...] = (z * _sigmoid(gate)).astype(y_ref.dtype)


def _s5_mixer(z, h0_re, h0_im, p, bb, n_steps, target_rows):
    rows_total = z.shape[0]
    rows = _pick_rows(n_steps, bb, target_rows)
    tc = rows // bb
    const2 = lambda i: (0, 0)
    const3 = lambda i: (0, 0, 0)
    kern = functools.partial(_s5_kernel, bb=bb, tc=tc)
    return pl.pallas_call(
        kern,
        grid=(rows_total // rows,),
        in_specs=[pl.BlockSpec((rows, S5_WIDTH), lambda i: (i, Z_UA // S5_WIDTH)),
                  pl.BlockSpec((S5_KT, MXU_K, 2 * S5_NT), const3),
                  pl.BlockSpec((SUBLANES, S5_N), const2),
                  pl.BlockSpec((SUBLANES, S5_N), const2),
                  pl.BlockSpec((S5_KT, S5_NT, MXU_K), const3),
                  pl.BlockSpec((S5_KT, S5_NT, MXU_K), const3),
                  pl.BlockSpec((1, S5_WIDTH), const2),
                  pl.BlockSpec((S5_WIDTH, S5_WIDTH), const2),
                  pl.BlockSpec((1, S5_WIDTH), const2),
                  pl.BlockSpec((bb, S5_N), const2),
                  pl.BlockSpec((bb, S5_N), const2)],
        out_specs=[pl.BlockSpec((rows, S5_WIDTH), lambda i: (i, 0)),
                   pl.BlockSpec((bb, S5_N), const2),
                   pl.BlockSpec((bb, S5_N), const2)],
        out_shape=[jax.ShapeDtypeStruct((rows_total, S5_WIDTH), BF16),
                   jax.ShapeDtypeStruct((bb, S5_N), F32),
                   jax.ShapeDtypeStruct((bb, S5_N), F32)],
        scratch_shapes=[pltpu.VMEM((bb, S5_N), F32),
                        pltpu.VMEM((bb, S5_N), F32),
                        pltpu.VMEM((rows, S5_N), F32),
                        pltpu.VMEM((rows, S5_N), F32)],
        compiler_params=_cparams(("arbitrary",)),
        name="s5_mixer",
    )(z, p['s5_wb'], p['s5_are'], p['s5_aim'], p['s5_cre'], p['s5_cim'],
      p['s5_d'], p['s5_w_glu'], p['s5_b_glu'], h0_re, h0_im)


def _lru_kernel(xb_ref, gb_ref, cache_ref, cw_ref, cb_ref, wg_ref, ba_ref, bx_ref, lam_ref, h0_ref,
                y_ref, h_out, cache_out,
                ext_ref, a_ref, b_ref, h_ref, *, bb, tc):
    rows = tc * bb
    halo = (LRU_CONV - 1) * bb

    @pl.when(pl.program_id(0) == 0)
    def _():
        ext_ref[0:halo, :] = cache_ref[...]
        h_ref[...] = h0_ref[...]

    ext_ref[halo:halo + rows, :] = xb_ref[...].astype(F32)
    xc = cb_ref[...] + cw_ref[0:1, :] * ext_ref[0:rows, :]
    for j in range(1, LRU_CONV):
        xc = xc + cw_ref[j:j + 1, :] * ext_ref[j * bb:j * bb + rows, :]
    new_cache = ext_ref[rows:rows + halo, :]
    cache_out[...] = new_cache
    ext_ref[0:halo, :] = new_cache

    xcb = xc.astype(BF16)
    rs, is_ = [], []
    for kt in range(LRU_KT):
        g = jnp.dot(xcb[:, kt * MXU_K:(kt + 1) * MXU_K], wg_ref[kt], preferred_element_type=F32)
        rs.append(g[:, 0:MXU_K])
        is_.append(g[:, MXU_K:2 * MXU_K])
    r = _sigmoid(jnp.concatenate(rs, axis=-1) + ba_ref[...])
    ig = _sigmoid(jnp.concatenate(is_, axis=-1) + bx_ref[...])
    nl = -lam_ref[...]
    softplus = jnp.maximum(nl, 0.0) + jnp.log1p(jnp.exp(-jnp.abs(nl)))
    log_a = (-LRU_C) * r * softplus
    a = jnp.exp(log_a)
    mult = jnp.sqrt(-jnp.tanh(log_a) * (a * a + 1.0))
    a_ref[...] = a
    b_ref[...] = mult * (ig * xc)

    def scan_tile(rt, carry):
        r0 = rt * SUBLANES
        h0 = h_ref[pl.ds(r0, SUBLANES), :]

        def step(t, h):
            row = pl.multiple_of(t * bb + r0, SUBLANES)
            h = a_ref[pl.ds(row, SUBLANES), :] * h + b_ref[pl.ds(row, SUBLANES), :]
            b_ref[pl.ds(row, SUBLANES), :] = h
            return h

        h1 = lax.fori_loop(0, tc, step, h0, unroll=2 if tc % 2 == 0 else 1)
        h_ref[pl.ds(r0, SUBLANES), :] = h1
        return carry

    if bb == SUBLANES:
        scan_tile(0, 0)
    else:
        lax.fori_loop(0, bb // SUBLANES, scan_tile, 0)

    h_out[...] = h_ref[...]
    y_ref[...] = (b_ref[...] * jax.nn.gelu(gb_ref[...].astype(F32))).astype(y_ref.dtype)


def _lru_mixer(z, h0, cache, p, bb, n_steps, target_rows):
    rows_total = z.shape[0]
    rows = _pick_rows(n_steps, bb, target_rows)
    tc = rows // bb
    halo = (LRU_CONV - 1) * bb
    const2 = lambda i: (0, 0)
    vec = pl.BlockSpec((1, LRU_WIDTH), const2)
    kern = functools.partial(_lru_kernel, bb=bb, tc=tc)
    return pl.pallas_call(
        kern,
        grid=(rows_total // rows,),
        in_specs=[pl.BlockSpec((rows, LRU_WIDTH), lambda i: (i, Z_XB // LRU_WIDTH)),
                  pl.BlockSpec((rows, LRU_WIDTH), lambda i: (i, Z_GB // LRU_WIDTH)),
                  pl.BlockSpec((halo, LRU_WIDTH), const2),
                  pl.BlockSpec((LRU_CONV, LRU_WIDTH), const2),
                  vec,
                  pl.BlockSpec((LRU_KT, MXU_K, 2 * MXU_K), lambda i: (0, 0, 0)),
                  vec, vec, vec,
                  pl.BlockSpec((bb, LRU_WIDTH), const2)],
        out_specs=[pl.BlockSpec((rows, LRU_WIDTH), lambda i: (i, 0)),
                   pl.BlockSpec((bb, LRU_WIDTH), const2),
                   pl.BlockSpec((halo, LRU_WIDTH), const2)],
        out_shape=[jax.ShapeDtypeStruct((rows_total, LRU_WIDTH), BF16),
                   jax.ShapeDtypeStruct((bb, LRU_WIDTH), F32),
                   jax.ShapeDtypeStruct((halo, LRU_WIDTH), F32)],
        scratch_shapes=[pltpu.VMEM((rows + halo, LRU_WIDTH), F32),
                        pltpu.VMEM((rows, LRU_WIDTH), F32),
                        pltpu.VMEM((rows, LRU_WIDTH), F32),
                        pltpu.VMEM((bb, LRU_WIDTH), F32)],
        compiler_params=_cparams(("arbitrary",)),
        name="rglru_mixer",
    )(z, z, cache, p['lru_conv_w'], p['lru_conv_b'], p['lru_wg'], p['lru_b_a'], p['lru_b_x'],
      p['lru_lam'], h0)


def _ret_sequence(q, k, v, state, cos, sin, decay_ref, kdec, cross, sdec, ng):
    half = RET_DK // 2
    lane = lax.broadcasted_iota(jnp.int32, q.shape, 1)
    first = (lane % RET_DK) < half
    mean = lambda x: jnp.mean(x, axis=-1, keepdims=True)

    def rope(x):
        swapped = jnp.where(first, pltpu.roll(x, RET_QK - half, 1), pltpu.roll(x, half, 1))
        return x * cos + swapped * sin

    qr = rope(q).astype(BF16)
    kr = rope(k) * (RET_DK ** -0.5)
    kd = (kr * kdec).astype(BF16)
    kr = kr.astype(BF16)
    outs, new_states = [], []
    for h in range(RET_HEADS):
        qh = qr[:, h * RET_DK:(h + 1) * RET_DK]
        kh = kr[:, h * RET_DK:(h + 1) * RET_DK]
        kdh = kd[:, h * RET_DK:(h + 1) * RET_DK]
        vh = v[:, h * RET_DV:(h + 1) * RET_DV]
        s = state(h)
        sc = lax.dot_general(qh, kh, (((1,), (1,)), ((), ())), preferred_element_type=F32)
        sc = (sc * decay_ref[h]).astype(BF16)
        o = jnp.dot(sc, vh, preferred_element_type=F32)
        o = o + jnp.dot(qh, s.astype(BF16), preferred_element_type=F32) * cross[:, h * RET_DV:(h + 1) * RET_DV]
        new_states.append(sdec[h:h + 1, :] * s + lax.dot_general(kdh, vh, (((0,), (0,)), ((), ())),
                                                                 preferred_element_type=F32))
        oc = o - mean(o)
        outs.append(oc * lax.rsqrt(mean(oc * oc) + EPS) * ng[:, h * RET_DV:(h + 1) * RET_DV])
    return jnp.concatenate(outs, axis=-1), new_states


def _ret_bm_kernel(z_ref, cos_ref, sin_ref, decay_ref, kdec_ref, cross_ref, sdec_ref, ng_ref, s0_ref,
                   y_ref, s_out):
    cos, sin, kdec, cross, sdec, ng = (cos_ref[...], sin_ref[...], kdec_ref[...], cross_ref[...],
                                       sdec_ref[...], ng_ref[...])

    def one(i, carry):
        blk = z_ref[i]
        v = blk[:, RV:RV + RET_V]
        g = blk[:, RG:RG + RET_V].astype(F32)
        q = blk[:, RQ:RQ + RET_QK].astype(F32)
        k = blk[:, RK:RK + RET_QK].astype(F32)
        on, new_states = _ret_sequence(q, k, v, lambda h: s0_ref[i, h], cos, sin, decay_ref, kdec, cross,
                                       sdec, ng)
        for h in range(RET_HEADS):
            s_out[i, h] = new_states[h]
        y_ref[i] = (g * _sigmoid(g) * on).astype(y_ref.dtype)
        return carry

    lax.fori_loop(0, z_ref.shape[0], one, 0, unroll=True)


def _ret_mixer_bm(zr, s0_layers, layer, norm_g, tabs, group):
    nb, c, _ = zr.shape
    const2 = lambda i: (0, 0)
    return pl.pallas_call(
        _ret_bm_kernel,
        grid=(nb // group,),
        in_specs=[pl.BlockSpec((group, c, Z_RET_W), lambda i: (i, 0, 0)),
                  pl.BlockSpec((c, RET_QK), const2),
                  pl.BlockSpec((c, RET_QK), const2),
                  pl.BlockSpec((RET_HEADS, c, c), lambda i: (0, 0, 0)),
                  pl.BlockSpec((c, RET_QK), const2),
                  pl.BlockSpec((c, RET_V), const2),
                  pl.BlockSpec((RET_HEADS, RET_DV), const2),
                  pl.BlockSpec((1, RET_V), const2),
                  pl.BlockSpec((None, group, RET_HEADS, RET_DK, RET_DV), lambda i: (layer, i, 0, 0, 0))],
        out_specs=[pl.BlockSpec((group, c, RET_V), lambda i: (i, 0, 0)),
                   pl.BlockSpec((group, RET_HEADS, RET_DK, RET_DV), lambda i: (i, 0, 0, 0))],
        out_shape=[jax.ShapeDtypeStruct((nb, c, RET_V), BF16),
                   jax.ShapeDtypeStruct((nb, RET_HEADS, RET_DK, RET_DV), F32)],
        compiler_params=_cparams(("arbitrary",)),
        name="retention_step",
    )(zr, tabs['cos'], tabs['sin'], tabs['decay'], tabs['kdec'], tabs['cross'], tabs['sdec'], norm_g,
      s0_layers)


RET_V_SLABS = RET_V // 128
RET_QK_SLABS = RET_QK // 128
RET_IN_SLABS = RET_V_SLABS + 2 * RET_QK_SLABS


def _ret_tm_kernel(*refs, steps):
    z_ref, cos_ref, sin_ref, decay_ref, kdec_ref, cross_ref, sdec_ref, ng_ref, s0_ref = refs[:9]
    y_ref, s_out, in_slab, out_slab = refs[-4:]
    nseq = SUBLANES

    @pl.when(pl.program_id(0) == 0)
    def _():
        s_out[...] = s0_ref[...]

    def fill(lo, n, x):
        for s in range(n):
            in_slab[lo + s] = x[:, s * 128:(s + 1) * 128].astype(F32)

    nv, nq = RET_V_SLABS, RET_QK_SLABS
    fill(0, nv, z_ref[:, RV:RV + RET_V])
    fill(nv, nq, z_ref[:, RQ:RQ + RET_QK])
    fill(nv + nq, nq, z_ref[:, RK:RK + RET_QK])
    cos, sin, kdec, cross, sdec, ng = (cos_ref[...], sin_ref[...], kdec_ref[...], cross_ref[...],
                                       sdec_ref[...], ng_ref[...])

    def one(b, carry):
        rows = pl.ds(b, steps, stride=nseq)
        gather = lambda lo, n: jnp.concatenate([in_slab[lo + s, rows, :] for s in range(n)], axis=-1)
        v = gather(0, nv).astype(BF16)
        q = gather(nv, nq)
        k = gather(nv + nq, nq)
        on, new_states = _ret_sequence(q, k, v, lambda h: s_out[b, h], cos, sin, decay_ref, kdec, cross,
                                       sdec, ng)
        for h in range(RET_HEADS):
            s_out[b, h] = new_states[h]
        for s in range(nv):
            out_slab[s, rows, :] = on[:, s * 128:(s + 1) * 128]
        return carry

    lax.fori_loop(0, nseq, one, 0)
    g = z_ref[:, RG:RG + RET_V].astype(F32)
    on = jnp.concatenate([out_slab[s] for s in range(nv)], axis=-1)
    y_ref[...] = (g * _sigmoid(g) * on).astype(y_ref.dtype)


def _ret_mixer_tm(z, s0, norm_g, tabs, row0, n_chunks):
    c = tabs['chunk']
    rows = c * SUBLANES
    row_at = lambda i: pl.multiple_of(row0 + rows * i, 128)
    const2 = lambda i: (0, 0)
    in_specs = [pl.BlockSpec((pl.Element(rows), pl.Element(Z_RET_W)), lambda i: (row_at(i), Z_RET)),
                pl.BlockSpec((c, RET_QK), lambda i: (i, 0)),
                pl.BlockSpec((c, RET_QK), lambda i: (i, 0)),
                pl.BlockSpec((RET_HEADS, c, c), lambda i: (0, 0, 0)),
                pl.BlockSpec((c, RET_QK), const2),
                pl.BlockSpec((c, RET_V), const2),
                pl.BlockSpec((RET_HEADS, RET_DV), const2),
                pl.BlockSpec((1, RET_V), const2),
                pl.BlockSpec((SUBLANES, RET_HEADS, RET_DK, RET_DV), lambda i: (0, 0, 0, 0))]
    args = [z, tabs['cos'], tabs['sin'], tabs['decay'], tabs['kdec'], tabs['cross'], tabs['sdec'], norm_g, s0]
    return pl.pallas_call(
        functools.partial(_ret_tm_kernel, steps=c),
        grid=(n_chunks,),
        in_specs=in_specs,
        out_specs=[pl.BlockSpec((rows, RET_V), lambda i: (i, 0)),
                   pl.BlockSpec((SUBLANES, RET_HEADS, RET_DK, RET_DV), lambda i: (0, 0, 0, 0))],
        out_shape=[jax.ShapeDtypeStruct((n_chunks * rows, RET_V), BF16),
                   jax.ShapeDtypeStruct((SUBLANES, RET_HEADS, RET_DK, RET_DV), F32)],
        scratch_shapes=[pltpu.VMEM((RET_IN_SLABS, rows, 128), F32),
                        pltpu.VMEM((RET_V_SLABS, rows, 128), F32)],
        compiler_params=_cparams(("arbitrary",)),
        name="retention_chunks",
    )(*args)


def _ret_tables(pos, chunk, n_real):
    half = RET_DK // 2
    inv = np.power(ROPE_BASE, -np.arange(half, dtype=np.float64) / half)
    ang = np.asarray(pos, np.float64)[:, None] * inv
    cos_h = np.concatenate([np.cos(ang), np.cos(ang)], axis=-1)
    sin_h = np.concatenate([-np.sin(ang), np.sin(ang)], axis=-1)
    log_gamma = np.log1p(-np.exp2(-5.0 - np.arange(RET_HEADS, dtype=np.float64)))
    n = np.arange(chunk, dtype=np.float64)
    diff = n[:, None] - n[None, :]
    decay = np.where(diff >= 0, np.exp(log_gamma[:, None, None] * np.maximum(diff, 0.0)), 0.0)
    kdec = np.exp(log_gamma[:, None] * (chunk - 1.0 - n))
    m = np.maximum(n - (chunk - n_real), 0.0)
    cross = np.exp(log_gamma[:, None] * (m + 1.0))
    const = lambda a: jnp.asarray(np.ascontiguousarray(a), F32)
    return {'chunk': chunk,
            'cos': const(np.tile(cos_h, (1, RET_HEADS))), 'sin': const(np.tile(sin_h, (1, RET_HEADS))),
            'decay': const(decay),
            'kdec': const(np.repeat(kdec.T, RET_DK, axis=1)),
            'cross': const(np.repeat(cross.T, RET_DV, axis=1)),
            'sdec': const(np.broadcast_to(np.exp(log_gamma * n_real)[:, None], (RET_HEADS, RET_DV)))}


def _merge_kernel(x_ref, gt_ref, ya_ref, yb_ref, yc_ref, *rest, head):
    if head:
        yh_ref, wa_ref, wb_ref, wc_ref, wo_ref, g_ref, o_ref, ycs_ref = rest
        tm = yc_ref.shape[0]

        @pl.when(pl.program_id(0) == 0)
        def _():
            ycs_ref[0:head, :] = yh_ref[...]
            ycs_ref[head:tm, :] = yc_ref[0:tm - head, :]

        @pl.when(pl.program_id(0) > 0)
        def _():
            ycs_ref[...] = yc_ref[...]

        yc = ycs_ref[...]
    else:
        wa_ref, wb_ref, wc_ref, wo_ref, g_ref, o_ref = rest
        yc = yc_ref[...]
    gate = lambda j: _sigmoid(gt_ref[:, j * D_MODEL:(j + 1) * D_MODEL].astype(F32))
    m = gate(0) * jnp.dot(ya_ref[...], wa_ref[...], preferred_element_type=F32)
    m = m + gate(1) * jnp.dot(yb_ref[...], wb_ref[...], preferred_element_type=F32)
    m = m + gate(2) * jnp.dot(yc, wc_ref[...], preferred_element_type=F32)
    mo = jnp.dot(m.astype(BF16), wo_ref[...], preferred_element_type=F32)
    o_ref[...] = x_ref[...] + _rms(mo, g_ref[...])


def _merge(x, z, ya, yb, yc, p, tm):
    rows = x.shape[0]
    row = lambda i: (i, 0)
    const = lambda i: (0, 0)
    if isinstance(yc, tuple):
        y_head, yc = yc
        head = y_head.shape[0]
        assert head % 16 == 0 and head < tm and head + yc.shape[0] == rows
        yc_specs = [pl.BlockSpec((pl.Element(tm), pl.Element(RET_V)),
                                 lambda i: (pl.multiple_of(jnp.maximum(i * tm - head, 0), 16), 0)),
                    pl.BlockSpec((head, RET_V), const)]
        yc_args = [yc, y_head]
        scratch = [pltpu.VMEM((tm, RET_V), BF16)]
    else:
        head = 0
        yc_specs = [pl.BlockSpec((tm, RET_V), row)]
        yc_args = [yc]
        scratch = []
    return pl.pallas_call(
        functools.partial(_merge_kernel, head=head),
        grid=(rows // tm,),
        in_specs=[pl.BlockSpec((tm, D_MODEL), row),
                  pl.BlockSpec((pl.Element(tm), pl.Element(N_BRANCH * D_MODEL)),
                               lambda i: (pl.multiple_of(i * tm, 16), Z_GATES)),
                  pl.BlockSpec((tm, S5_WIDTH), row),
                  pl.BlockSpec((tm, LRU_WIDTH), row)] + yc_specs + [
                  pl.BlockSpec((S5_WIDTH, D_MODEL), const),
                  pl.BlockSpec((LRU_WIDTH, D_MODEL), const),
                  pl.BlockSpec((RET_V, D_MODEL), const),
                  pl.BlockSpec((D_MODEL, D_MODEL), const),
                  pl.BlockSpec((1, D_MODEL), const)],
        out_specs=pl.BlockSpec((tm, D_MODEL), row),
        out_shape=jax.ShapeDtypeStruct((rows, D_MODEL), F32),
        scratch_shapes=scratch,
        compiler_params=_cparams(("arbitrary",)),
        name="branch_merge",
    )(x, z, ya, yb, *yc_args, p['w_branch_a'], p['w_branch_b'], p['w_branch_c'], p['w_out'], p['g_post_mix'])


def _ffn_kernel(x_ref, gpre_ref, wg_ref, wu_ref, cw_ref, cb_ref, wd_ref, gpost_ref, cache_ref,
                o_ref, cache_out, xn_ref, acc_ref, ext_ref, *, bb, n_chunks):
    i = pl.program_id(0)
    c = pl.program_id(1)
    rows = x_ref.shape[0]
    halo = (FFN_CONV - 1) * bb

    @pl.when(c == 0)
    def _():
        xn_ref[...] = _rms(x_ref[...], gpre_ref[...]).astype(BF16)

    @pl.when(i == 0)
    def _():
        ext_ref[c, 0:halo, :] = cache_ref[...]

    xn = xn_ref[...]
    gf = jnp.dot(xn, wg_ref[...], preferred_element_type=F32)
    uf = jnp.dot(xn, wu_ref[...], preferred_element_type=F32)
    ext_ref[c, halo:halo + rows, :] = gf
    gc = cb_ref[...] + cw_ref[0:1, :] * ext_ref[c, 0:rows, :]
    for j in range(1, FFN_CONV):
        gc = gc + cw_ref[j:j + 1, :] * ext_ref[c, j * bb:j * bb + rows, :]
    new_cache = ext_ref[c, rows:rows + halo, :]
    cache_out[c] = new_cache
    ext_ref[c, 0:halo, :] = new_cache
    act = (jax.nn.gelu(gc) * uf).astype(BF16)
    part = jnp.dot(act, wd_ref[...], preferred_element_type=F32)

    @pl.when(c == 0)
    def _():
        acc_ref[...] = part

    @pl.when(c > 0)
    def _():
        acc_ref[...] += part

    @pl.when(c == n_chunks - 1)
    def _():
        o_ref[...] = x_ref[...] + _rms(acc_ref[...], gpost_ref[...])


def _ffn(x, cache, p, bb, tm, tf):
    rows = x.shape[0]
    halo = (FFN_CONV - 1) * bb
    n_chunks = D_FF // tf
    kern = functools.partial(_ffn_kernel, bb=bb, n_chunks=n_chunks)
    return pl.pallas_call(
        kern,
        grid=(rows // tm, n_chunks),
        in_specs=[pl.BlockSpec((tm, D_MODEL), lambda i, c: (i, 0)),
                  pl.BlockSpec((1, D_MODEL), lambda i, c: (0, 0)),
                  pl.BlockSpec((D_MODEL, tf), lambda i, c: (0, c)),
                  pl.BlockSpec((D_MODEL, tf), lambda i, c: (0, c + n_chunks)),
                  pl.BlockSpec((FFN_CONV, tf), lambda i, c: (0, c)),
                  pl.BlockSpec((1, tf), lambda i, c: (0, c)),
                  pl.BlockSpec((tf, D_MODEL), lambda i, c: (c, 0)),
                  pl.BlockSpec((1, D_MODEL), lambda i, c: (0, 0)),
                  pl.BlockSpec((halo, tf), lambda i, c: (0, c))],
        out_specs=[pl.BlockSpec((tm, D_MODEL), lambda i, c: (i, 0)),
                   pl.BlockSpec((n_chunks, halo, tf), lambda i, c: (0, 0, 0))],
        out_shape=[jax.ShapeDtypeStruct((rows, D_MODEL), F32),
                   jax.ShapeDtypeStruct((n_chunks, halo, tf), F32)],
        scratch_shapes=[pltpu.VMEM((tm, D_MODEL), BF16),
                        pltpu.VMEM((tm, D_MODEL), F32),
                        pltpu.VMEM((n_chunks, tm + halo, tf), F32)],
        compiler_params=_cparams(("arbitrary", "arbitrary")),
        name="conv_ffn",
    )(x, p['g_pre_ffn'], p['ffn_w_up'], p['ffn_w_up'], p['ffn_conv_w'], p['ffn_conv_b'],
      p['ffn_w_down'], p['g_post_ffn'], cache)


def _block_diag(w, tiles):
    nb, r, c = w.shape
    per = nb // tiles
    rows, cols = per * r, per * c
    on_diag = (np.arange(rows)[:, None] // r) == (np.arange(cols)[None, :] // c)
    return jnp.where(on_diag, jnp.tile(w.reshape(tiles, rows, c), (1, 1, per)), 0)


def _prep_layer(P, l):
    row = lambda a: a[l].reshape(1, -1).astype(F32)
    p = {
        'g_pre_mix': row(P['g_pre_mix']), 'g_post_mix': row(P['g_post_mix']),
        'g_pre_ffn': row(P['g_pre_ffn']), 'g_post_ffn': row(P['g_post_ffn']),
        'w_in': P['w_in'][l].astype(BF16),
        's5_lre': row(P['s5_a_re']), 's5_lim': row(P['s5_a_im']),
        's5_lstep': jnp.repeat(P['s5_log_step'][l], S5_STATE).reshape(1, S5_N),
        's5_bre': _block_diag(jnp.swapaxes(P['s5_b_re'][l], 1, 2), S5_KT),
        's5_bim': _block_diag(jnp.swapaxes(P['s5_b_im'][l], 1, 2), S5_KT),
        's5_cre': _block_diag(jnp.swapaxes(P['s5_c_re'][l], 1, 2), S5_KT).astype(BF16),
        's5_cim': _block_diag(jnp.swapaxes(P['s5_c_im'][l], 1, 2), S5_KT).astype(BF16),
        's5_d': row(P['s5_d']), 's5_w_glu': P['s5_w_glu'][l].astype(BF16), 's5_b_glu': row(P['s5_b_glu']),
        'lru_conv_w': P['lru_conv_w'][l], 'lru_conv_b': row(P['lru_conv_b']),
        'lru_wg': jnp.concatenate([_block_diag(P['lru_w_a'][l], LRU_KT),
                                   _block_diag(P['lru_w_x'][l], LRU_KT)], axis=-1).astype(BF16),
        'lru_b_a': row(P['lru_b_a']), 'lru_b_x': row(P['lru_b_x']), 'lru_lam': row(P['lru_lam']),
        'ret_norm_g': row(P['ret_norm_g']),
        'w_branch_a': P['w_branch_a'][l].astype(BF16), 'w_branch_b': P['w_branch_b'][l].astype(BF16),
        'w_branch_c': P['w_branch_c'][l].astype(BF16), 'w_out': P['w_out'][l].astype(BF16),
        'ffn_w_up': P['ffn_w_up'][l].astype(BF16), 'ffn_conv_w': P['ffn_conv_w'][l],
        'ffn_conv_b': row(P['ffn_conv_b']), 'ffn_w_down': P['ffn_w_down'][l].astype(BF16),
    }
    p['s5_wb'], p['s5_are'], p['s5_aim'] = _s5_discretize(p)
    return p


def _run_group(x, states, params, bb, n_steps, retention, tiles):
    s5_re, s5_im, lru_h, lru_cache, ret_s, ffn_cache = states
    new = [[] for _ in range(6)]
    for l in range(DEPTH):
        p = params[l]
        z = _norm_matmul(x, p['g_pre_mix'], p['w_in'], tiles['tm_in'], tiles['tn_in'])
        ya, hre, him = _s5_mixer(z, s5_re[l], s5_im[l], p, bb, n_steps, tiles['rows_scan'])
        yb, hl, lcache = _lru_mixer(z, lru_h[l], lru_cache[l], p, bb, n_steps, tiles['rows_scan'])
        yc, rs = retention(z, ret_s[l], p['ret_norm_g'])
        x = _merge(x, z, ya, yb, yc, p, tiles['tm'])
        x, fcache = _ffn(x, ffn_cache[l], p, bb, tiles['tm'], tiles['tf'])
        fcache = jnp.swapaxes(fcache, 0, 1).reshape(-1, D_FF)
        for lst, val in zip(new, (hre, him, hl, lcache, rs, fcache)):
            lst.append(val)
    return x, new


def kernel(x_prompt, x_sample, state_s5_re, state_s5_im, state_lru, cache_lru_conv, state_ret, cache_ffn_conv, meta_tokens, g_pre_mix, g_post_mix, g_pre_ffn, g_post_ffn, w_in, s5_a_re, s5_a_im, s5_log_step, s5_b_re, s5_b_im, s5_c_re, s5_c_im, s5_d, s5_w_glu, s5_b_glu, lru_conv_w, lru_conv_b, lru_w_a, lru_b_a, lru_w_x, lru_b_x, lru_lam, ret_norm_g, w_branch_a, w_branch_b, w_branch_c, w_out, ffn_w_up, ffn_conv_w, ffn_conv_b, ffn_w_down):
    P = {'g_pre_mix': g_pre_mix, 'g_post_mix': g_post_mix, 'g_pre_ffn': g_pre_ffn, 'g_post_ffn': g_post_ffn,
         'w_in': w_in, 's5_a_re': s5_a_re, 's5_a_im': s5_a_im, 's5_log_step': s5_log_step,
         's5_b_re': s5_b_re, 's5_b_im': s5_b_im, 's5_c_re': s5_c_re, 's5_c_im': s5_c_im,
         's5_d': s5_d, 's5_w_glu': s5_w_glu, 's5_b_glu': s5_b_glu,
         'lru_conv_w': lru_conv_w, 'lru_conv_b': lru_conv_b, 'lru_w_a': lru_w_a, 'lru_b_a': lru_b_a,
         'lru_w_x': lru_w_x, 'lru_b_x': lru_b_x, 'lru_lam': lru_lam, 'ret_norm_g': ret_norm_g,
         'w_branch_a': w_branch_a, 'w_branch_b': w_branch_b, 'w_branch_c': w_branch_c, 'w_out': w_out,
         'ffn_w_up': ffn_w_up, 'ffn_conv_w': ffn_conv_w, 'ffn_conv_b': ffn_conv_b, 'ffn_w_down': ffn_w_down}
    params = [_prep_layer(P, l) for l in range(DEPTH)]

    bp, sp, _ = x_prompt.shape
    lp = N_META + sp
    meta = jnp.broadcast_to(meta_tokens.astype(F32)[:, None, :], (N_META, bp, D_MODEL))
    xp = jnp.pad(jnp.swapaxes(x_prompt, 0, 1), ((N_META, 0), (0, 0), (0, 0)))
    xp = lax.dynamic_update_slice(xp, meta, (0, 0, 0)).reshape(lp * bp, D_MODEL)
    zeros = lambda *s: [jnp.zeros(s, F32)] * DEPTH
    st_p = (zeros(bp, S5_N), zeros(bp, S5_N), zeros(bp, LRU_WIDTH),
            zeros((LRU_CONV - 1) * bp, LRU_WIDTH), zeros(bp, RET_HEADS, RET_DK, RET_DV),
            zeros((FFN_CONV - 1) * bp, D_FF))
    assert bp == SUBLANES and sp % RET_CHUNK == 0 and (N_META * bp) % 128 == 0
    tabs_meta = _ret_tables(np.arange(N_META), N_META, N_META)
    tabs_main = _ret_tables(N_META + np.arange(sp), RET_CHUNK, RET_CHUNK)

    def retention_p(z, s0, ng):
        y_meta, s1 = _ret_mixer_tm(z, s0, ng, tabs_meta, 0, 1)
        y_main, s2 = _ret_mixer_tm(z, s1, ng, tabs_main, N_META * bp, sp // RET_CHUNK)
        return (y_meta, y_main), s2

    tiles_p = {'tm_in': _pick_rows(lp, bp, 1400), 'tn_in': 1408, 'rows_scan': 700,
               'tm': _pick_rows(lp, bp, 700), 'tf': 1408}
    yp, new_p = _run_group(xp, st_p, params, bp, lp, retention_p, tiles_p)
    y_prompt = jnp.swapaxes(yp.reshape(lp, bp, D_MODEL), 0, 1)[:, N_META:]

    bs, ds, _ = x_sample.shape
    xs = jnp.swapaxes(x_sample, 0, 1).reshape(ds * bs, D_MODEL)
    tm_major = lambda c: jnp.swapaxes(c, 0, 1).reshape(-1, c.shape[-1])
    st_s = ([state_s5_re[l].reshape(bs, S5_N) for l in range(DEPTH)],
            [state_s5_im[l].reshape(bs, S5_N) for l in range(DEPTH)],
            [state_lru[l] for l in range(DEPTH)],
            [tm_major(cache_lru_conv[l]) for l in range(DEPTH)],
            list(range(DEPTH)),
            [tm_major(cache_ffn_conv[l]) for l in range(DEPTH)])
    chunk_s = 16
    pad_s = chunk_s - ds
    tabs_s = _ret_tables(np.concatenate([np.zeros(pad_s), PAST_LEN + np.arange(ds)]), chunk_s, ds)

    def retention_s(z, layer, ng):
        zr = z[:, Z_RET:Z_RET + Z_RET_W].reshape(ds, bs, Z_RET_W).transpose(1, 0, 2)
        zr = jnp.pad(zr, ((0, 0), (pad_s, 0), (0, 0)))
        yc, s1 = _ret_mixer_bm(zr, state_ret, layer, ng, tabs_s, SUBLANES)
        return yc[:, pad_s:].transpose(1, 0, 2).reshape(ds * bs, RET_V), s1

    tiles_s = {'tm_in': ds * bs, 'tn_in': 1408, 'rows_scan': ds * bs, 'tm': ds * bs, 'tf': 1408}
    ys, new_s = _run_group(xs, st_s, params, bs, ds, retention_s, tiles_s)
    y_sample = jnp.swapaxes(ys.reshape(ds, bs, D_MODEL), 0, 1)

    def pack(new, b):
        s5r, s5i, lh, lc, rs, fc = [jnp.stack(v) for v in new]
        return (s5r.reshape(DEPTH, b, S5_GROUPS, S5_STATE), s5i.reshape(DEPTH, b, S5_GROUPS, S5_STATE), lh,
                jnp.swapaxes(lc.reshape(DEPTH, LRU_CONV - 1, b, LRU_WIDTH), 1, 2), rs,
                jnp.swapaxes(fc.reshape(DEPTH, FFN_CONV - 1, b, D_FF), 1, 2))

    return (y_prompt, y_sample) + pack(new_p, bp) + pack(new_s, bs)
```

```python
import functools
import math

import jax
import jax.numpy as jnp
import numpy as np
from jax import lax
from jax.experimental import pallas as pl
from jax.experimental.pallas import tpu as pltpu

F32 = jnp.float32
BF16 = jnp.bfloat16

D_MODEL = 1024
DEPTH = 2
PAST_LEN = 16384
N_META = 16
S5_WIDTH = 768
S5_GROUP = 16
S5_GROUPS = S5_WIDTH // S5_GROUP
S5_STATE = 64
S5_N = S5_GROUPS * S5_STATE
LRU_WIDTH = 768
LRU_BLOCK = 64
LRU_BLOCKS = LRU_WIDTH // LRU_BLOCK
LRU_CONV = 4
LRU_C = 8.0
RET_HEADS = 8
RET_DK = 64
RET_DV = 128
RET_QK = RET_HEADS * RET_DK
RET_V = RET_HEADS * RET_DV
RET_CHUNK = 128
ROPE_BASE = 10000.0
N_BRANCH = 3
D_FF = 2816
FFN_CONV = 3
EPS = 1e-6
D_IN = S5_WIDTH + 2 * LRU_WIDTH + 2 * RET_QK + 2 * RET_V + N_BRANCH * D_MODEL

Z_UA = 0
Z_XB = Z_UA + S5_WIDTH
Z_GB = Z_XB + LRU_WIDTH
Z_RET = Z_GB + LRU_WIDTH
Z_RET_W = 2 * RET_QK + 2 * RET_V
Z_GATES = Z_RET + Z_RET_W
RQ, RK, RV, RG = 0, RET_QK, 2 * RET_QK, 2 * RET_QK + RET_V

MXU_K = 256
S5_KT = S5_WIDTH // MXU_K
S5_NT = S5_N // S5_KT
LRU_KT = LRU_WIDTH // MXU_K

SUBLANES = 8
VMEM_LIMIT = 56 * 1024 * 1024


def _cparams(sem):
    return pltpu.CompilerParams(dimension_semantics=sem, vmem_limit_bytes=VMEM_LIMIT)


def _pick_rows(n_steps, bb, target_rows):
    best = None
    for tt in range(1, n_steps + 1):
        if n_steps % tt:
            continue
        rows = tt * bb
        if rows % 16 == 0 and rows <= target_rows:
            best = rows
    if best is None:
        best = n_steps * bb
    return best


def _rms(x, g):
    return x * lax.rsqrt(jnp.mean(x * x, axis=-1, keepdims=True) + EPS) * g


def _sigmoid(x):
    return 0.5 * jnp.tanh(0.5 * x) + 0.5


def _norm_matmul_kernel(x_ref, g_ref, w_ref, o_ref, xn_ref):
    @pl.when(pl.program_id(1) == 0)
    def _():
        xn_ref[...] = _rms(x_ref[...], g_ref[...]).astype(BF16)

    o_ref[...] = jnp.dot(xn_ref[...], w_ref[...],
                         preferred_element_type=F32).astype(o_ref.dtype)


def _norm_matmul(x, g, w, tm, tn):
    rows, d = x.shape
    n = w.shape[1]
    return pl.pallas_call(
        _norm_matmul_kernel,
        grid=(rows // tm, n // tn),
        in_specs=[pl.BlockSpec((tm, d), lambda i, j: (i, 0)),
                  pl.BlockSpec((1, d), lambda i, j: (0, 0)),
                  pl.BlockSpec((d, tn), lambda i, j: (0, j))],
        out_specs=pl.BlockSpec((tm, tn), lambda i, j: (i, j)),
        out_shape=jax.ShapeDtypeStruct((rows, n), BF16),
        scratch_shapes=[pltpu.VMEM((tm, d), BF16)],
        compiler_params=_cparams(("arbitrary", "arbitrary")),
        name="norm_in_proj",
    )(x, g, w)


def _s5_disc_kernel(lre_ref, lim_ref, lstep_ref, bre_ref, bim_ref, wb_ref, are_ref, aim_ref):
    dt = jnp.exp(lstep_ref[...])
    lr = lre_ref[...]
    li = lim_ref[...]
    mag = jnp.exp(lr * dt)
    a_re = mag * jnp.cos(li * dt)
    a_im = mag * jnp.sin(li * dt)
    den = lr * lr + li * li
    nr = a_re - 1.0
    cr = (nr * lr + a_im * li) / den
    ci = (a_im * lr - nr * li) / den
    are_ref[...] = jnp.broadcast_to(a_re, are_ref.shape)
    aim_ref[...] = jnp.broadcast_to(a_im, aim_ref.shape)
    b_r = bre_ref[0]
    b_i = bim_ref[0]
    wb_ref[0, :, 0:S5_NT] = (cr * b_r - ci * b_i).astype(BF16)
    wb_ref[0, :, S5_NT:2 * S5_NT] = (cr * b_i + ci * b_r).astype(BF16)


def _s5_discretize(p):
    vec = pl.BlockSpec((1, S5_NT), lambda k: (0, k))
    blk = pl.BlockSpec((1, MXU_K, S5_NT), lambda k: (k, 0, 0))
    return pl.pallas_call(
        _s5_disc_kernel,
        grid=(S5_KT,),
        in_specs=[vec, vec, vec, blk, blk],
        out_specs=[pl.BlockSpec((1, MXU_K, 2 * S5_NT), lambda k: (k, 0, 0)),
                   pl.BlockSpec((SUBLANES, S5_NT), lambda k: (0, k)),
                   pl.BlockSpec((SUBLANES, S5_NT), lambda k: (0, k))],
        out_shape=[jax.ShapeDtypeStruct((S5_KT, MXU_K, 2 * S5_NT), BF16),
                   jax.ShapeDtypeStruct((SUBLANES, S5_N), F32),
                   jax.ShapeDtypeStruct((SUBLANES, S5_N), F32)],
        compiler_params=_cparams(("arbitrary",)),
        name="s5_discretize",
    )(p['s5_lre'], p['s5_lim'], p['s5_lstep'], p['s5_bre'], p['s5_bim'])


def _s5_kernel(u_ref, wb_ref, are_ref, aim_ref, cre_ref, cim_ref,
               d_ref, wglu_ref, bglu_ref, h0re_ref, h0im_ref,
               y_ref, hre_out, him_out,
               hre_ref, him_ref, sre_ref, sim_ref, *, bb, tc):
    @pl.when(pl.program_id(0) == 0)
    def _():
        hre_ref[...] = h0re_ref[...]
        him_ref[...] = h0im_ref[...]

    u = u_ref[...]
    for kt in range(S5_KT):
        r = jnp.dot(u[:, kt * MXU_K:(kt + 1) * MXU_K], wb_ref[kt], preferred_element_type=F32)
        sre_ref[:, kt * S5_NT:(kt + 1) * S5_NT] = r[:, 0:S5_NT]
        sim_ref[:, kt * S5_NT:(kt + 1) * S5_NT] = r[:, S5_NT:2 * S5_NT]

    def scan_tile(rt, carry):
        r0 = rt * SUBLANES
        for kt in range(S5_KT):
            sl = slice(kt * S5_NT, (kt + 1) * S5_NT)
            ar = are_ref[:, sl]
            ai = aim_ref[:, sl]
            hr0 = hre_ref[pl.ds(r0, SUBLANES), sl]
            hi0 = him_ref[pl.ds(r0, SUBLANES), sl]

            def step(t, h):
                hr, hi = h
                row = pl.multiple_of(t * bb + r0, SUBLANES)
                nr_ = ar * hr - ai * hi + sre_ref[pl.ds(row, SUBLANES), sl]
                ni_ = ar * hi + ai * hr + sim_ref[pl.ds(row, SUBLANES), sl]
                sre_ref[pl.ds(row, SUBLANES), sl] = nr_
                sim_ref[pl.ds(row, SUBLANES), sl] = ni_
                return nr_, ni_

            hr1, hi1 = lax.fori_loop(0, tc, step, (hr0, hi0), unroll=2 if tc % 2 == 0 else 1)
            hre_ref[pl.ds(r0, SUBLANES), sl] = hr1
            him_ref[pl.ds(r0, SUBLANES), sl] = hi1
        return carry

    if bb == SUBLANES:
        scan_tile(0, 0)
    else:
        lax.fori_loop(0, bb // SUBLANES, scan_tile, 0)

    hre_out[...] = hre_ref[...]
    him_out[...] = him_ref[...]

    ys = []
    for kt in range(S5_KT):
        sl = slice(kt * S5_NT, (kt + 1) * S5_NT)
        ys.append(jnp.dot(sre_ref[:, sl].astype(BF16), cre_ref[kt], preferred_element_type=F32)
                  - jnp.dot(sim_ref[:, sl].astype(BF16), cim_ref[kt], preferred_element_type=F32))
    y = jnp.concatenate(ys, axis=-1) + d_ref[...] * u.astype(F32)
    z = jax.nn.gelu(y)
    gate = jnp.dot(z.astype(BF16), wglu_ref[...], preferred_element_type=F32) + bglu_ref[...]
    y_ref[...] = (z * _sigmoid(gate)).astype(y_ref.dtype)


def _s5_mixer(z, h0_re, h0_im, p, bb, n_steps, target_rows):
    rows_total = z.shape[0]
    rows = _pick_rows(n_steps, bb, target_rows)
    tc = rows // bb
    const2 = lambda i: (0, 0)
    const3 = lambda i: (0, 0, 0)
    kern = functools.partial(_s5_kernel, bb=bb, tc=tc)
    return pl.pallas_call(
        kern,
        grid=(rows_total // rows,),
        in_specs=[pl.BlockSpec((rows, S5_WIDTH), lambda i: (i, Z_UA // S5_WIDTH)),
                  pl.BlockSpec((S5_KT, MXU_K, 2 * S5_NT), const3),
                  pl.BlockSpec((SUBLANES, S5_N), const2),
                  pl.BlockSpec((SUBLANES, S5_N), const2),
                  pl.BlockSpec((S5_KT, S5_NT, MXU_K), const3),
                  pl.BlockSpec((S5_KT, S5_NT, MXU_K), const3),
                  pl.BlockSpec((1, S5_WIDTH), const2),
                  pl.BlockSpec((S5_WIDTH, S5_WIDTH), const2),
                  pl.BlockSpec((1, S5_WIDTH), const2),
                  pl.BlockSpec((bb, S5_N), const2),
                  pl.BlockSpec((bb, S5_N), const2)],
        out_specs=[pl.BlockSpec((rows, S5_WIDTH), lambda i: (i, 0)),
                   pl.BlockSpec((bb, S5_N), const2),
                   pl.BlockSpec((bb, S5_N), const2)],
        out_shape=[jax.ShapeDtypeStruct((rows_total, S5_WIDTH), BF16),
                   jax.ShapeDtypeStruct((bb, S5_N), F32),
                   jax.ShapeDtypeStruct((bb, S5_N), F32)],
        scratch_shapes=[pltpu.VMEM((bb, S5_N), F32),
                        pltpu.VMEM((bb, S5_N), F32),
                        pltpu.VMEM((rows, S5_N), F32),
                        pltpu.VMEM((rows, S5_N), F32)],
        compiler_params=_cparams(("arbitrary",)),
        name="s5_mixer",
    )(z, p['s5_wb'], p['s5_are'], p['s5_aim'], p['s5_cre'], p['s5_cim'],
      p['s5_d'], p['s5_w_glu'], p['s5_b_glu'], h0_re, h0_im)


def _lru_kernel(xb_ref, gb_ref, cache_ref, cw_ref, cb_ref, wg_ref, ba_ref, bx_ref, lam_ref, h0_ref,
                y_ref, h_out, cache_out,
                ext_ref, a_ref, b_ref, h_ref, *, bb, tc):
    rows = tc * bb
    halo = (LRU_CONV - 1) * bb

    @pl.when(pl.program_id(0) == 0)
    def _():
        ext_ref[0:halo, :] = cache_ref[...]
        h_ref[...] = h0_ref[...]

    ext_ref[halo:halo + rows, :] = xb_ref[...].astype(F32)
    xc = cb_ref[...] + cw_ref[0:1, :] * ext_ref[0:rows, :]
    for j in range(1, LRU_CONV):
        xc = xc + cw_ref[j:j + 1, :] * ext_ref[j * bb:j * bb + rows, :]
    new_cache = ext_ref[rows:rows + halo, :]
    cache_out[...] = new_cache
    ext_ref[0:halo, :] = new_cache

    xcb = xc.astype(BF16)
    rs, is_ = [], []
    for kt in range(LRU_KT):
        g = jnp.dot(xcb[:, kt * MXU_K:(kt + 1) * MXU_K], wg_ref[kt], preferred_element_type=F32)
        rs.append(g[:, 0:MXU_K])
        is_.append(g[:, MXU_K:2 * MXU_K])
    r = _sigmoid(jnp.concatenate(rs, axis=-1) + ba_ref[...])
    ig = _sigmoid(jnp.concatenate(is_, axis=-1) + bx_ref[...])
    nl = -lam_ref[...]
    softplus = jnp.maximum(nl, 0.0) + jnp.log1p(jnp.exp(-jnp.abs(nl)))
    log_a = (-LRU_C) * r * softplus
    a = jnp.exp(log_a)
    mult = jnp.sqrt(-jnp.tanh(log_a) * (a * a + 1.0))
    a_ref[...] = a
    b_ref[...] = mult * (ig * xc)

    def scan_tile(rt, carry):
        r0 = rt * SUBLANES
        h0 = h_ref[pl.ds(r0, SUBLANES), :]

        def step(t, h):
            row = pl.multiple_of(t * bb + r0, SUBLANES)
            h = a_ref[pl.ds(row, SUBLANES), :] * h + b_ref[pl.ds(row, SUBLANES), :]
            b_ref[pl.ds(row, SUBLANES), :] = h
            return h

        h1 = lax.fori_loop(0, tc, step, h0, unroll=2 if tc % 2 == 0 else 1)
        h_ref[pl.ds(r0, SUBLANES), :] = h1
        return carry

    if bb == SUBLANES:
        scan_tile(0, 0)
    else:
        lax.fori_loop(0, bb // SUBLANES, scan_tile, 0)

    h_out[...] = h_ref[...]
    y_ref[...] = (b_ref[...] * jax.nn.gelu(gb_ref[...].astype(F32))).astype(y_ref.dtype)


def _lru_mixer(z, h0, cache, p, bb, n_steps, target_rows):
    rows_total = z.shape[0]
    rows = _pick_rows(n_steps, bb, target_rows)
    tc = rows // bb
    halo = (LRU_CONV - 1) * bb
    const2 = lambda i: (0, 0)
    vec = pl.BlockSpec((1, LRU_WIDTH), const2)
    kern = functools.partial(_lru_kernel, bb=bb, tc=tc)
    return pl.pallas_call(
        kern,
        grid=(rows_total // rows,),
        in_specs=[pl.BlockSpec((rows, LRU_WIDTH), lambda i: (i, Z_XB // LRU_WIDTH)),
                  pl.BlockSpec((rows, LRU_WIDTH), lambda i: (i, Z_GB // LRU_WIDTH)),
                  pl.BlockSpec((halo, LRU_WIDTH), const2),
                  pl.BlockSpec((LRU_CONV, LRU_WIDTH), const2),
                  vec,
                  pl.BlockSpec((LRU_KT, MXU_K, 2 * MXU_K), lambda i: (0, 0, 0)),
                  vec, vec, vec,
                  pl.BlockSpec((bb, LRU_WIDTH), const2)],
        out_specs=[pl.BlockSpec((rows, LRU_WIDTH), lambda i: (i, 0)),
                   pl.BlockSpec((bb, LRU_WIDTH), const2),
                   pl.BlockSpec((halo, LRU_WIDTH), const2)],
        out_shape=[jax.ShapeDtypeStruct((rows_total, LRU_WIDTH), BF16),
                   jax.ShapeDtypeStruct((bb, LRU_WIDTH), F32),
                   jax.ShapeDtypeStruct((halo, LRU_WIDTH), F32)],
        scratch_shapes=[pltpu.VMEM((rows + halo, LRU_WIDTH), F32),
                        pltpu.VMEM((rows, LRU_WIDTH), F32),
                        pltpu.VMEM((rows, LRU_WIDTH), F32),
                        pltpu.VMEM((bb, LRU_WIDTH), F32)],
        compiler_params=_cparams(("arbitrary",)),
        name="rglru_mixer",
    )(z, z, cache, p['lru_conv_w'], p['lru_conv_b'], p['lru_wg'], p['lru_b_a'], p['lru_b_x'],
      p['lru_lam'], h0)


def _ret_sequence(q, k, v, state, cos, sin, decay_ref, kdec, cross, sdec, ng):
    half = RET_DK // 2
    lane = lax.broadcasted_iota(jnp.int32, q.shape, 1)
    first = (lane % RET_DK) < half
    mean = lambda x: jnp.mean(x, axis=-1, keepdims=True)

    def rope(x):
        swapped = jnp.where(first, pltpu.roll(x, RET_QK - half, 1), pltpu.roll(x, half, 1))
        return x * cos + swapped * sin

    qr = rope(q).astype(BF16)
    kr = rope(k) * (RET_DK ** -0.5)
    kd = (kr * kdec).astype(BF16)
    kr = kr.astype(BF16)
    heads = range(RET_HEADS)
    dk = lambda x, h: x[:, h * RET_DK:(h + 1) * RET_DK]
    dv = lambda x, h: x[:, h * RET_DV:(h + 1) * RET_DV]
    states = [state(h) for h in heads]
    scores = [lax.dot_general(dk(qr, h), dk(kr, h), (((1,), (1,)), ((), ())), preferred_element_type=F32)
              for h in heads]
    carried = [jnp.dot(dk(qr, h), states[h].astype(BF16), preferred_element_type=F32) for h in heads]
    updates = [lax.dot_general(dk(kd, h), dv(v, h), (((0,), (0,)), ((), ())), preferred_element_type=F32)
               for h in heads]
    scores = [(scores[h] * decay_ref[h]).astype(BF16) for h in heads]
    outs = [jnp.dot(scores[h], dv(v, h), preferred_element_type=F32) + carried[h] * dv(cross, h) for h in heads]
    new_states = [sdec[h:h + 1, :] * states[h] + updates[h] for h in heads]
    normed = []
    for h in heads:
        oc = outs[h] - mean(outs[h])
        normed.append(oc * lax.rsqrt(mean(oc * oc) + EPS) * dv(ng, h))
    return jnp.concatenate(normed, axis=-1), new_states


def _ret_bm_kernel(z_ref, cos_ref, sin_ref, decay_ref, kdec_ref, cross_ref, sdec_ref, ng_ref, s0_ref,
                   y_ref, s_out):
    cos, sin, kdec, cross, sdec, ng = (cos_ref[...], sin_ref[...], kdec_ref[...], cross_ref[...],
                                       sdec_ref[...], ng_ref[...])

    def one(i, carry):
        blk = z_ref[i]
        v = blk[:, RV:RV + RET_V]
        g = blk[:, RG:RG + RET_V].astype(F32)
        q = blk[:, RQ:RQ + RET_QK].astype(F32)
        k = blk[:, RK:RK + RET_QK].astype(F32)
        on, new_states = _ret_sequence(q, k, v, lambda h: s0_ref[i, h], cos, sin, decay_ref, kdec, cross,
                                       sdec, ng)
        for h in range(RET_HEADS):
            s_out[i, h] = new_states[h]
        y_ref[i] = (g * _sigmoid(g) * on).astype(y_ref.dtype)
        return carry

    lax.fori_loop(0, z_ref.shape[0], one, 0, unroll=True)


def _ret_mixer_bm(zr, s0_layers, layer, norm_g, tabs, group):
    nb, c, _ = zr.shape
    const2 = lambda i: (0, 0)
    return pl.pallas_call(
        _ret_bm_kernel,
        grid=(nb // group,),
        in_specs=[pl.BlockSpec((group, c, Z_RET_W), lambda i: (i, 0, 0)),
                  pl.BlockSpec((c, RET_QK), const2),
                  pl.BlockSpec((c, RET_QK), const2),
                  pl.BlockSpec((RET_HEADS, c, c), lambda i: (0, 0, 0)),
                  pl.BlockSpec((c, RET_QK), const2),
                  pl.BlockSpec((c, RET_V), const2),
                  pl.BlockSpec((RET_HEADS, RET_DV), const2),
                  pl.BlockSpec((1, RET_V), const2),
                  pl.BlockSpec((None, group, RET_HEADS, RET_DK, RET_DV), lambda i: (layer, i, 0, 0, 0))],
        out_specs=[pl.BlockSpec((group, c, RET_V), lambda i: (i, 0, 0)),
                   pl.BlockSpec((group, RET_HEADS, RET_DK, RET_DV), lambda i: (i, 0, 0, 0))],
        out_shape=[jax.ShapeDtypeStruct((nb, c, RET_V), BF16),
                   jax.ShapeDtypeStruct((nb, RET_HEADS, RET_DK, RET_DV), F32)],
        compiler_params=_cparams(("arbitrary",)),
        name="retention_step",
    )(zr, tabs['cos'], tabs['sin'], tabs['decay'], tabs['kdec'], tabs['cross'], tabs['sdec'], norm_g,
      s0_layers)


RET_V_SLABS = RET_V // 128
RET_QK_SLABS = RET_QK // 128
RET_IN_SLABS = RET_V_SLABS + 2 * RET_QK_SLABS


def _ret_tm_kernel(*refs, steps):
    z_ref, cos_ref, sin_ref, decay_ref, kdec_ref, cross_ref, sdec_ref, ng_ref, s0_ref = refs[:9]
    y_ref, s_out, in_slab, out_slab = refs[-4:]
    nseq = SUBLANES

    @pl.when(pl.program_id(0) == 0)
    def _():
        s_out[...] = s0_ref[...]

    def fill(lo, n, x):
        for s in range(n):
            in_slab[lo + s] = x[:, s * 128:(s + 1) * 128].astype(F32)

    nv, nq = RET_V_SLABS, RET_QK_SLABS
    fill(0, nv, z_ref[:, RV:RV + RET_V])
    fill(nv, nq, z_ref[:, RQ:RQ + RET_QK])
    fill(nv + nq, nq, z_ref[:, RK:RK + RET_QK])
    cos, sin, kdec, cross, sdec, ng = (cos_ref[...], sin_ref[...], kdec_ref[...], cross_ref[...],
                                       sdec_ref[...], ng_ref[...])

    def one(b, carry):
        rows = pl.ds(b, steps, stride=nseq)
        gather = lambda lo, n: jnp.concatenate([in_slab[lo + s, rows, :] for s in range(n)], axis=-1)
        v = gather(0, nv).astype(BF16)
        q = gather(nv, nq)
        k = gather(nv + nq, nq)
        on, new_states = _ret_sequence(q, k, v, lambda h: s_out[b, h], cos, sin, decay_ref, kdec, cross,
                                       sdec, ng)
        for h in range(RET_HEADS):
            s_out[b, h] = new_states[h]
        for s in range(nv):
            out_slab[s, rows, :] = on[:, s * 128:(s + 1) * 128]
        return carry

    lax.fori_loop(0, nseq, one, 0)
    g = z_ref[:, RG:RG + RET_V].astype(F32)
    on = jnp.concatenate([out_slab[s] for s in range(nv)], axis=-1)
    y_ref[...] = (g * _sigmoid(g) * on).astype(y_ref.dtype)


def _ret_mixer_tm(z, s0, norm_g, tabs, row0, n_chunks):
    c = tabs['chunk']
    rows = c * SUBLANES
    row_at = lambda i: pl.multiple_of(row0 + rows * i, 128)
    const2 = lambda i: (0, 0)
    in_specs = [pl.BlockSpec((pl.Element(rows), pl.Element(Z_RET_W)), lambda i: (row_at(i), Z_RET)),
                pl.BlockSpec((c, RET_QK), lambda i: (i, 0)),
                pl.BlockSpec((c, RET_QK), lambda i: (i, 0)),
                pl.BlockSpec((RET_HEADS, c, c), lambda i: (0, 0, 0)),
                pl.BlockSpec((c, RET_QK), const2),
                pl.BlockSpec((c, RET_V), const2),
                pl.BlockSpec((RET_HEADS, RET_DV), const2),
                pl.BlockSpec((1, RET_V), const2),
                pl.BlockSpec((SUBLANES, RET_HEADS, RET_DK, RET_DV), lambda i: (0, 0, 0, 0))]
    args = [z, tabs['cos'], tabs['sin'], tabs['decay'], tabs['kdec'], tabs['cross'], tabs['sdec'], norm_g, s0]
    return pl.pallas_call(
        functools.partial(_ret_tm_kernel, steps=c),
        grid=(n_chunks,),
        in_specs=in_specs,
        out_specs=[pl.BlockSpec((rows, RET_V), lambda i: (i, 0)),
                   pl.BlockSpec((SUBLANES, RET_HEADS, RET_DK, RET_DV), lambda i: (0, 0, 0, 0))],
        out_shape=[jax.ShapeDtypeStruct((n_chunks * rows, RET_V), BF16),
                   jax.ShapeDtypeStruct((SUBLANES, RET_HEADS, RET_DK, RET_DV), F32)],
        scratch_shapes=[pltpu.VMEM((RET_IN_SLABS, rows, 128), F32),
                        pltpu.VMEM((RET_V_SLABS, rows, 128), F32)],
        compiler_params=_cparams(("arbitrary",)),
        name="retention_chunks",
    )(*args)


def _ret_tables(pos, chunk, n_real):
    half = RET_DK // 2
    inv = np.power(ROPE_BASE, -np.arange(half, dtype=np.float64) / half)
    ang = np.asarray(pos, np.float64)[:, None] * inv
    cos_h = np.concatenate([np.cos(ang), np.cos(ang)], axis=-1)
    sin_h = np.concatenate([-np.sin(ang), np.sin(ang)], axis=-1)
    log_gamma = np.log1p(-np.exp2(-5.0 - np.arange(RET_HEADS, dtype=np.float64)))
    n = np.arange(chunk, dtype=np.float64)
    diff = n[:, None] - n[None, :]
    decay = np.where(diff >= 0, np.exp(log_gamma[:, None, None] * np.maximum(diff, 0.0)), 0.0)
    kdec = np.exp(log_gamma[:, None] * (chunk - 1.0 - n))
    m = np.maximum(n - (chunk - n_real), 0.0)
    cross = np.exp(log_gamma[:, None] * (m + 1.0))
    const = lambda a: jnp.asarray(np.ascontiguousarray(a), F32)
    return {'chunk': chunk,
            'cos': const(np.tile(cos_h, (1, RET_HEADS))), 'sin': const(np.tile(sin_h, (1, RET_HEADS))),
            'decay': const(decay),
            'kdec': const(np.repeat(kdec.T, RET_DK, axis=1)),
            'cross': const(np.repeat(cross.T, RET_DV, axis=1)),
            'sdec': const(np.broadcast_to(np.exp(log_gamma * n_real)[:, None], (RET_HEADS, RET_DV)))}


def _merge_kernel(x_ref, gt_ref, ya_ref, yb_ref, yc_ref, *rest, head):
    if head:
        yh_ref, wa_ref, wb_ref, wc_ref, wo_ref, g_ref, o_ref, ycs_ref = rest
        tm = yc_ref.shape[0]

        @pl.when(pl.program_id(0) == 0)
        def _():
            ycs_ref[0:head, :] = yh_ref[...]
            ycs_ref[head:tm, :] = yc_ref[0:tm - head, :]

        @pl.when(pl.program_id(0) > 0)
        def _():
            ycs_ref[...] = yc_ref[...]

        yc = ycs_ref[...]
    else:
        wa_ref, wb_ref, wc_ref, wo_ref, g_ref, o_ref = rest
        yc = yc_ref[...]
    gate = lambda j: _sigmoid(gt_ref[:, j * D_MODEL:(j + 1) * D_MODEL].astype(F32))
    m = gate(0) * jnp.dot(ya_ref[...], wa_ref[...], preferred_element_type=F32)
    m = m + gate(1) * jnp.dot(yb_ref[...], wb_ref[...], preferred_element_type=F32)
    m = m + gate(2) * jnp.dot(yc, wc_ref[...], preferred_element_type=F32)
    mo = jnp.dot(m.astype(BF16), wo_ref[...], preferred_element_type=F32)
    o_ref[...] = x_ref[...] + _rms(mo, g_ref[...])


def _merge(x, z, ya, yb, yc, p, tm):
    rows = x.shape[0]
    row = lambda i: (i, 0)
    const = lambda i: (0, 0)
    if isinstance(yc, tuple):
        y_head, yc = yc
        head = y_head.shape[0]
        assert head % 16 == 0 and head < tm and head + yc.shape[0] == rows
        yc_specs = [pl.BlockSpec((pl.Element(tm), pl.Element(RET_V)),
                                 lambda i: (pl.multiple_of(jnp.maximum(i * tm - head, 0), 16), 0)),
                    pl.BlockSpec((head, RET_V), const)]
        yc_args = [yc, y_head]
        scratch = [pltpu.VMEM((tm, RET_V), BF16)]
    else:
        head = 0
        yc_specs = [pl.BlockSpec((tm, RET_V), row)]
        yc_args = [yc]
        scratch = []
    return pl.pallas_call(
        functools.partial(_merge_kernel, head=head),
        grid=(rows // tm,),
        in_specs=[pl.BlockSpec((tm, D_MODEL), row),
                  pl.BlockSpec((pl.Element(tm), pl.Element(N_BRANCH * D_MODEL)),
                               lambda i: (pl.multiple_of(i * tm, 16), Z_GATES)),
                  pl.BlockSpec((tm, S5_WIDTH), row),
                  pl.BlockSpec((tm, LRU_WIDTH), row)] + yc_specs + [
                  pl.BlockSpec((S5_WIDTH, D_MODEL), const),
                  pl.BlockSpec((LRU_WIDTH, D_MODEL), const),
                  pl.BlockSpec((RET_V, D_MODEL), const),
                  pl.BlockSpec((D_MODEL, D_MODEL), const),
                  pl.BlockSpec((1, D_MODEL), const)],
        out_specs=pl.BlockSpec((tm, D_MODEL), row),
        out_shape=jax.ShapeDtypeStruct((rows, D_MODEL), F32),
        scratch_shapes=scratch,
        compiler_params=_cparams(("arbitrary",)),
        name="branch_merge",
    )(x, z, ya, yb, *yc_args, p['w_branch_a'], p['w_branch_b'], p['w_branch_c'], p['w_out'], p['g_post_mix'])


def _ffn_kernel(x_ref, gpre_ref, wg_ref, wu_ref, cw_ref, cb_ref, wd_ref, gpost_ref, cache_ref,
                o_ref, cache_out, xn_ref, acc_ref, ext_ref, *, bb, n_chunks):
    i = pl.program_id(0)
    c = pl.program_id(1)
    rows = x_ref.shape[0]
    halo = (FFN_CONV - 1) * bb

    @pl.when(c == 0)
    def _():
        xn_ref[...] = _rms(x_ref[...], gpre_ref[...]).astype(BF16)

    @pl.when(i == 0)
    def _():
        ext_ref[c, 0:halo, :] = cache_ref[...]

    xn = xn_ref[...]
    gf = jnp.dot(xn, wg_ref[...], preferred_element_type=F32)
    uf = jnp.dot(xn, wu_ref[...], preferred_element_type=F32)
    ext_ref[c, halo:halo + rows, :] = gf
    gc = cb_ref[...] + cw_ref[0:1, :] * ext_ref[c, 0:rows, :]
    for j in range(1, FFN_CONV):
        gc = gc + cw_ref[j:j + 1, :] * ext_ref[c, j * bb:j * bb + rows, :]
    new_cache = ext_ref[c, rows:rows + halo, :]
    cache_out[c] = new_cache
    ext_ref[c, 0:halo, :] = new_cache
    act = (jax.nn.gelu(gc) * uf).astype(BF16)
    part = jnp.dot(act, wd_ref[...], preferred_element_type=F32)

    @pl.when(c == 0)
    def _():
        acc_ref[...] = part

    @pl.when(c > 0)
    def _():
        acc_ref[...] += part

    @pl.when(c == n_chunks - 1)
    def _():
        o_ref[...] = x_ref[...] + _rms(acc_ref[...], gpost_ref[...])


def _ffn(x, cache, p, bb, tm, tf):
    rows = x.shape[0]
    halo = (FFN_CONV - 1) * bb
    n_chunks = D_FF // tf
    kern = functools.partial(_ffn_kernel, bb=bb, n_chunks=n_chunks)
    return pl.pallas_call(
        kern,
        grid=(rows // tm, n_chunks),
        in_specs=[pl.BlockSpec((tm, D_MODEL), lambda i, c: (i, 0)),
                  pl.BlockSpec((1, D_MODEL), lambda i, c: (0, 0)),
                  pl.BlockSpec((D_MODEL, tf), lambda i, c: (0, c)),
                  pl.BlockSpec((D_MODEL, tf), lambda i, c: (0, c + n_chunks)),
                  pl.BlockSpec((FFN_CONV, tf), lambda i, c: (0, c)),
                  pl.BlockSpec((1, tf), lambda i, c: (0, c)),
                  pl.BlockSpec((tf, D_MODEL), lambda i, c: (c, 0)),
                  pl.BlockSpec((1, D_MODEL), lambda i, c: (0, 0)),
                  pl.BlockSpec((halo, tf), lambda i, c: (0, c))],
        out_specs=[pl.BlockSpec((tm, D_MODEL), lambda i, c: (i, 0)),
                   pl.BlockSpec((n_chunks, halo, tf), lambda i, c: (0, 0, 0))],
        out_shape=[jax.ShapeDtypeStruct((rows, D_MODEL), F32),
                   jax.ShapeDtypeStruct((n_chunks, halo, tf), F32)],
        scratch_shapes=[pltpu.VMEM((tm, D_MODEL), BF16),
                        pltpu.VMEM((tm, D_MODEL), F32),
                        pltpu.VMEM((n_chunks, tm + halo, tf), F32)],
        compiler_params=_cparams(("arbitrary", "arbitrary")),
        name="conv_ffn",
    )(x, p['g_pre_ffn'], p['ffn_w_up'], p['ffn_w_up'], p['ffn_conv_w'], p['ffn_conv_b'],
      p['ffn_w_down'], p['g_post_ffn'], cache)


def _block_diag(w, tiles):
    nb, r, c = w.shape
    per = nb // tiles
    rows, cols = per * r, per * c
    on_diag = (np.arange(rows)[:, None] // r) == (np.arange(cols)[None, :] // c)
    return jnp.where(on_diag, jnp.tile(w.reshape(tiles, rows, c), (1, 1, per)), 0)


def _prep_layer(P, l):
    row = lambda a: a[l].reshape(1, -1).astype(F32)
    p = {
        'g_pre_mix': row(P['g_pre_mix']), 'g_post_mix': row(P['g_post_mix']),
        'g_pre_ffn': row(P['g_pre_ffn']), 'g_post_ffn': row(P['g_post_ffn']),
        'w_in': P['w_in'][l].astype(BF16),
        's5_lre': row(P['s5_a_re']), 's5_lim': row(P['s5_a_im']),
        's5_lstep': jnp.repeat(P['s5_log_step'][l], S5_STATE).reshape(1, S5_N),
        's5_bre': _block_diag(jnp.swapaxes(P['s5_b_re'][l], 1, 2), S5_KT),
        's5_bim': _block_diag(jnp.swapaxes(P['s5_b_im'][l], 1, 2), S5_KT),
        's5_cre': _block_diag(jnp.swapaxes(P['s5_c_re'][l], 1, 2), S5_KT).astype(BF16),
        's5_cim': _block_diag(jnp.swapaxes(P['s5_c_im'][l], 1, 2), S5_KT).astype(BF16),
        's5_d': row(P['s5_d']), 's5_w_glu': P['s5_w_glu'][l].astype(BF16), 's5_b_glu': row(P['s5_b_glu']),
        'lru_conv_w': P['lru_conv_w'][l], 'lru_conv_b': row(P['lru_conv_b']),
        'lru_wg': jnp.concatenate([_block_diag(P['lru_w_a'][l], LRU_KT),
                                   _block_diag(P['lru_w_x'][l], LRU_KT)], axis=-1).astype(BF16),
        'lru_b_a': row(P['lru_b_a']), 'lru_b_x': row(P['lru_b_x']), 'lru_lam': row(P['lru_lam']),
        'ret_norm_g': row(P['ret_norm_g']),
        'w_branch_a': P['w_branch_a'][l].astype(BF16), 'w_branch_b': P['w_branch_b'][l].astype(BF16),
        'w_branch_c': P['w_branch_c'][l].astype(BF16), 'w_out': P['w_out'][l].astype(BF16),
        'ffn_w_up': P['ffn_w_up'][l].astype(BF16), 'ffn_conv_w': P['ffn_conv_w'][l],
        'ffn_conv_b': row(P['ffn_conv_b']), 'ffn_w_down': P['ffn_w_down'][l].astype(BF16),
    }
    p['s5_wb'], p['s5_are'], p['s5_aim'] = _s5_discretize(p)
    return p


def _run_group(x, states, params, bb, n_steps, retention, tiles):
    s5_re, s5_im, lru_h, lru_cache, ret_s, ffn_cache = states
    new = [[] for _ in range(6)]
    for l in range(DEPTH):
        p = params[l]
        z = _norm_matmul(x, p['g_pre_mix'], p['w_in'], tiles['tm_in'], tiles['tn_in'])
        ya, hre, him = _s5_mixer(z, s5_re[l], s5_im[l], p, bb, n_steps, tiles['rows_scan'])
        yb, hl, lcache = _lru_mixer(z, lru_h[l], lru_cache[l], p, bb, n_steps, tiles['rows_scan'])
        yc, rs = retention(z, ret_s[l], p['ret_norm_g'])
        x = _merge(x, z, ya, yb, yc, p, tiles['tm'])
        x, fcache = _ffn(x, ffn_cache[l], p, bb, tiles['tm'], tiles['tf'])
        fcache = jnp.swapaxes(fcache, 0, 1).reshape(-1, D_FF)
        for lst, val in zip(new, (hre, him, hl, lcache, rs, fcache)):
            lst.append(val)
    return x, new


def kernel(x_prompt, x_sample, state_s5_re, state_s5_im, state_lru, cache_lru_conv, state_ret, cache_ffn_conv, meta_tokens, g_pre_mix, g_post_mix, g_pre_ffn, g_post_ffn, w_in, s5_a_re, s5_a_im, s5_log_step, s5_b_re, s5_b_im, s5_c_re, s5_c_im, s5_d, s5_w_glu, s5_b_glu, lru_conv_w, lru_conv_b, lru_w_a, lru_b_a, lru_w_x, lru_b_x, lru_lam, ret_norm_g, w_branch_a, w_branch_b, w_branch_c, w_out, ffn_w_up, ffn_conv_w, ffn_conv_b, ffn_w_down):
    P = {'g_pre_mix': g_pre_mix, 'g_post_mix': g_post_mix, 'g_pre_ffn': g_pre_ffn, 'g_post_ffn': g_post_ffn,
         'w_in': w_in, 's5_a_re': s5_a_re, 's5_a_im': s5_a_im, 's5_log_step': s5_log_step,
         's5_b_re': s5_b_re, 's5_b_im': s5_b_im, 's5_c_re': s5_c_re, 's5_c_im': s5_c_im,
         's5_d': s5_d, 's5_w_glu': s5_w_glu, 's5_b_glu': s5_b_glu,
         'lru_conv_w': lru_conv_w, 'lru_conv_b': lru_conv_b, 'lru_w_a': lru_w_a, 'lru_b_a': lru_b_a,
         'lru_w_x': lru_w_x, 'lru_b_x': lru_b_x, 'lru_lam': lru_lam, 'ret_norm_g': ret_norm_g,
         'w_branch_a': w_branch_a, 'w_branch_b': w_branch_b, 'w_branch_c': w_branch_c, 'w_out': w_out,
         'ffn_w_up': ffn_w_up, 'ffn_conv_w': ffn_conv_w, 'ffn_conv_b': ffn_conv_b, 'ffn_w_down': ffn_w_down}
    params = [_prep_layer(P, l) for l in range(DEPTH)]

    bp, sp, _ = x_prompt.shape
    lp = N_META + sp
    meta = jnp.broadcast_to(meta_tokens.astype(F32)[:, None, :], (N_META, bp, D_MODEL))
    xp = jnp.pad(jnp.swapaxes(x_prompt, 0, 1), ((N_META, 0), (0, 0), (0, 0)))
    xp = lax.dynamic_update_slice(xp, meta, (0, 0, 0)).reshape(lp * bp, D_MODEL)
    zeros = lambda *s: [jnp.zeros(s, F32)] * DEPTH
    st_p = (zeros(bp, S5_N), zeros(bp, S5_N), zeros(bp, LRU_WIDTH),
            zeros((LRU_CONV - 1) * bp, LRU_WIDTH), zeros(bp, RET_HEADS, RET_DK, RET_DV),
            zeros((FFN_CONV - 1) * bp, D_FF))
    assert bp == SUBLANES and sp % RET_CHUNK == 0 and (N_META * bp) % 128 == 0
    tabs_meta = _ret_tables(np.arange(N_META), N_META, N_META)
    tabs_main = _ret_tables(N_META + np.arange(sp), RET_CHUNK, RET_CHUNK)

    def retention_p(z, s0, ng):
        y_meta, s1 = _ret_mixer_tm(z, s0, ng, tabs_meta, 0, 1)
        y_main, s2 = _ret_mixer_tm(z, s1, ng, tabs_main, N_META * bp, sp // RET_CHUNK)
        return (y_meta, y_main), s2

    tiles_p = {'tm_in': _pick_rows(lp, bp, 1400), 'tn_in': 1408, 'rows_scan': 700,
               'tm': _pick_rows(lp, bp, 700), 'tf': 1408}
    yp, new_p = _run_group(xp, st_p, params, bp, lp, retention_p, tiles_p)
    y_prompt = jnp.swapaxes(yp.reshape(lp, bp, D_MODEL), 0, 1)[:, N_META:]

    bs, ds, _ = x_sample.shape
    xs = jnp.swapaxes(x_sample, 0, 1).reshape(ds * bs, D_MODEL)
    tm_major = lambda c: jnp.swapaxes(c, 0, 1).reshape(-1, c.shape[-1])
    st_s = ([state_s5_re[l].reshape(bs, S5_N) for l in range(DEPTH)],
            [state_s5_im[l].reshape(bs, S5_N) for l in range(DEPTH)],
            [state_lru[l] for l in range(DEPTH)],
            [tm_major(cache_lru_conv[l]) for l in range(DEPTH)],
            list(range(DEPTH)),
            [tm_major(cache_ffn_conv[l]) for l in range(DEPTH)])
    chunk_s = 16
    pad_s = chunk_s - ds
    tabs_s = _ret_tables(np.concatenate([np.zeros(pad_s), PAST_LEN + np.arange(ds)]), chunk_s, ds)

    def retention_s(z, layer, ng):
        zr = z[:, Z_RET:Z_RET + Z_RET_W].reshape(ds, bs, Z_RET_W).transpose(1, 0, 2)
        zr = jnp.pad(zr, ((0, 0), (pad_s, 0), (0, 0)))
        yc, s1 = _ret_mixer_bm(zr, state_ret, layer, ng, tabs_s, SUBLANES)
        return yc[:, pad_s:].transpose(1, 0, 2).reshape(ds * bs, RET_V), s1

    tiles_s = {'tm_in': ds * bs, 'tn_in': 1408, 'rows_scan': ds * bs, 'tm': ds * bs, 'tf': 1408}
    ys, new_s = _run_group(xs, st_s, params, bs, ds, retention_s, tiles_s)
    y_sample = jnp.swapaxes(ys.reshape(ds, bs, D_MODEL), 0, 1)

    def pack(new, b):
        s5r, s5i, lh, lc, rs, fc = [jnp.stack(v) for v in new]
        return (s5r.reshape(DEPTH, b, S5_GROUPS, S5_STATE), s5i.reshape(DEPTH, b, S5_GROUPS, S5_STATE), lh,
                jnp.swapaxes(lc.reshape(DEPTH, LRU_CONV - 1, b, LRU_WIDTH), 1, 2), rs,
                jnp.swapaxes(fc.reshape(DEPTH, FFN_CONV - 1, b, D_FF), 1, 2))

    return (y_prompt, y_sample) + pack(new_p, bp) + pack(new_s, bs)
```
